```python
import math
import jax
import jax.numpy as jnp
from jax import lax
import numpy as np

D_MODEL = 1024
BATCH = 4
SEQ = 4096
DEPTH = 1
DEC_BATCH = 128
DEC_SEQ = 8
PAST_LEN = 16384
PAGE_SIZE = 128

HEAD_DIM = 64
N_Q_HEADS = D_MODEL // HEAD_DIM
N_KV_HEADS = N_Q_HEADS // 4
Q_PER_KV = N_Q_HEADS // N_KV_HEADS
WINDOW = 128
ROPE_THETA = 500000.0
ROPE_DIM = HEAD_DIM // 4
ATTN_SCALE = HEAD_DIM ** -0.5
RW_HEAD = 64
RW_HEADS = D_MODEL // RW_HEAD
W_LORA = 64
A_LORA = 64
G_LORA = 128
LNX_EPS = 64e-5
N_EXPERTS = 32
TOP_K = 4
D_EXPERT = D_MODEL
SWIGLU_LIMIT = 7.0
SWIGLU_ALPHA = 1.702
RMS_EPS = 1e-5

ATT_Q = N_Q_HEADS * HEAD_DIM
ATT_KV = N_KV_HEADS * HEAD_DIM
RW_WIDTH = RW_HEADS * RW_HEAD
RW_COLS = 3 * RW_WIDTH + W_LORA + A_LORA + G_LORA
GATE_COLS = 2 * D_MODEL
IN_COLS = ATT_Q + 2 * ATT_KV + RW_COLS + GATE_COLS

kernel_name = "hybrid_swa_sink_rwkv7_moe_step"


def rmsnorm(x, g):
    xf = x.astype(jnp.float32)
    y = xf * lax.rsqrt(jnp.mean(xf * xf, axis=-1, keepdims=True) + RMS_EPS)
    return (y * g.astype(jnp.float32)).astype(x.dtype)


def partial_rope(x, positions):
    half = ROPE_DIM // 2
    inv_freq = ROPE_THETA ** (-jnp.arange(half, dtype=jnp.float32) / half)
    ang = positions.astype(jnp.float32)[:, None] * inv_freq[None, :]
    cos = jnp.cos(ang)[:, None, :]
    sin = jnp.sin(ang)[:, None, :]
    xr = x[..., :ROPE_DIM].astype(jnp.float32)
    x1, x2 = xr[..., :half], xr[..., half:]
    rot = jnp.concatenate([x1 * cos - x2 * sin, x2 * cos + x1 * sin], axis=-1)
    return jnp.concatenate([rot.astype(x.dtype), x[..., ROPE_DIM:]], axis=-1)


def sink_softmax(s, sink):
    m = jnp.maximum(jnp.max(s, axis=-1, keepdims=True), sink)
    e = jnp.exp(s - m)
    return e / (jnp.sum(e, axis=-1, keepdims=True) + jnp.exp(sink - m))


def attention_prompt(q, k, v, sinks):
    b, t = q.shape[0], q.shape[1]
    nb = t // WINDOW
    qb = q.reshape(b, nb, WINDOW, N_KV_HEADS, Q_PER_KV, HEAD_DIM)
    kb = k.reshape(b, nb, WINDOW, N_KV_HEADS, HEAD_DIM)
    vb = v.reshape(b, nb, WINDOW, N_KV_HEADS, HEAD_DIM)
    pad = ((0, 0), (1, 0), (0, 0), (0, 0), (0, 0))
    k2 = jnp.concatenate([jnp.pad(kb[:, :-1], pad), kb], axis=2)
    v2 = jnp.concatenate([jnp.pad(vb[:, :-1], pad), vb], axis=2)
    s = jnp.einsum("bnqkgd,bnjkd->bnkgqj", qb, k2).astype(jnp.float32) * ATTN_SCALE
    rel = WINDOW + jnp.arange(WINDOW)[:, None] - jnp.arange(2 * WINDOW)[None, :]
    band = (rel >= 0) & (rel < WINDOW)
    kpos = jnp.arange(nb)[:, None] * WINDOW - WINDOW + jnp.arange(2 * WINDOW)[None, :]
    mask = band[None] & (kpos >= 0)[:, None, :]
    s = jnp.where(mask[None, :, None, None], s, -jnp.inf)
    p = sink_softmax(s, sinks.astype(jnp.float32).reshape(N_KV_HEADS, Q_PER_KV, 1, 1))
    o = jnp.einsum("bnkgqj,bnjkd->bnqkgd", p.astype(v.dtype), v2)
    return o.reshape(b, t, ATT_Q), k[:, t - WINDOW:], v[:, t - WINDOW:]


def attention_sample(q, k, v, sinks, cache_k, cache_v):
    b, t = q.shape[0], q.shape[1]
    n_past = cache_k.shape[1]
    keys = jnp.concatenate([cache_k.astype(k.dtype), k], axis=1)
    vals = jnp.concatenate([cache_v.astype(v.dtype), v], axis=1)
    qg = q.reshape(b, t, N_KV_HEADS, Q_PER_KV, HEAD_DIM)
    s = jnp.einsum("bqkgd,bjkd->bkgqj", qg, keys).astype(jnp.float32) * ATTN_SCALE
    qpos = PAST_LEN + jnp.arange(t)
    kpos = PAST_LEN - n_past + jnp.arange(n_past + t)
    rel = qpos[:, None] - kpos[None, :]
    mask = (rel >= 0) & (rel < WINDOW)
    s = jnp.where(mask, s, -jnp.inf)
    p = sink_softmax(s, sinks.astype(jnp.float32).reshape(N_KV_HEADS, Q_PER_KV, 1, 1))
    o = jnp.einsum("bkgqj,bjkd->bqkgd", p.astype(v.dtype), vals)
    return o.reshape(b, t, ATT_Q), keys[:, t:], vals[:, t:]


def wkv_step(S, inp):
    r_t, w_t, k_t, v_t, a_t, b_t = inp
    sa = jnp.einsum("bhij,bhj->bhi", S, a_t)
    S = S * w_t[:, :, None, :] + sa[..., None] * b_t[:, :, None, :] + v_t[..., None] * k_t[:, :, None, :]
    y = jnp.einsum("bhij,bhj->bhi", S, r_t)
    return S, y


def rwkv7_time_mix(z, S0, shift0, lw):
    b, t = z.shape[0], z.shape[1]
    z_prev = jnp.concatenate([shift0[:, None, :].astype(z.dtype), z[:, :-1]], axis=1)
    zs = (z + (z_prev - z) * lw["rw_mu"]).astype(jnp.float32)
    r, k, v, xw, xa, xg = jnp.split(
        zs, [RW_WIDTH, 2 * RW_WIDTH, 3 * RW_WIDTH, 3 * RW_WIDTH + W_LORA, 3 * RW_WIDTH + W_LORA + A_LORA], axis=-1)
    f = lambda name: lw[name].astype(jnp.float32)
    w_log = -jax.nn.softplus(-(f("rw_w0") + jnp.tanh(xw) @ f("rw_w_up"))) - 0.5
    decay = jnp.exp(-jnp.exp(w_log))
    a = jax.nn.sigmoid(f("rw_a0") + xa @ f("rw_a_up"))
    g = jax.nn.sigmoid(xg) @ f("rw_g_up")
    heads = lambda u: u.reshape(b, t, RW_HEADS, RW_HEAD)
    kk = heads(k * f("rw_k_k"))
    kk = kk / jnp.maximum(jnp.sqrt(jnp.sum(kk * kk, axis=-1, keepdims=True)), 1e-12)
    k = k * (1.0 + (a - 1.0) * f("rw_k_a"))
    rh, kh, vh, wh, ah = heads(r), heads(k), heads(v), heads(decay), heads(a)
    seq = tuple(jnp.moveaxis(u, 1, 0) for u in (rh, wh, kh, vh, -kk, kk * ah))
    S_fin, y = lax.scan(wkv_step, S0.astype(jnp.float32), seq)
    y = jnp.moveaxis(y, 0, 1)
    mu = jnp.mean(y, axis=-1, keepdims=True)
    var = jnp.mean((y - mu) ** 2, axis=-1, keepdims=True)
    y = ((y - mu) * lax.rsqrt(var + LNX_EPS)).reshape(b, t, RW_WIDTH) * f("rw_lnx_g") + f("rw_lnx_b")
    bonus = jnp.sum(rh * kh * f("rw_r_k"), axis=-1, keepdims=True) * vh
    y = (y + bonus.reshape(b, t, RW_WIDTH)) * g
    return y.astype(z.dtype), S_fin.astype(S0.dtype), z[:, -1]


def moe_ffn(h, lw):
    shp = h.shape
    hf = h.reshape(-1, D_MODEL)
    logits = (hf @ lw["w_router"] + lw["b_router"]).astype(jnp.float32)
    top_v, top_i = lax.top_k(logits, TOP_K)
    top_w = jax.nn.softmax(top_v, axis=-1)
    comb = jnp.sum(jax.nn.one_hot(top_i, N_EXPERTS, dtype=jnp.float32) * top_w[..., None], axis=1)
    out = jnp.zeros(hf.shape, jnp.float32)
    for e in range(N_EXPERTS):
        gu = hf @ lw["w_gate_up"][e] + lw["b_gate_up"][e]
        glu = jnp.minimum(gu[:, 0::2], SWIGLU_LIMIT)
        lin = jnp.clip(gu[:, 1::2], -SWIGLU_LIMIT, SWIGLU_LIMIT)
        act = glu * jax.nn.sigmoid(SWIGLU_ALPHA * glu) * (lin + 1.0)
        y = act @ lw["w_down"][e] + lw["b_down"][e]
        out = out + comb[:, e:e + 1] * y.astype(jnp.float32)
    return out.astype(h.dtype).reshape(shp)


def layer_forward(x, positions, cache_k, cache_v, S0, shift0, lw):
    b, t = x.shape[0], x.shape[1]
    h = rmsnorm(x, lw["norm_mix_g"])
    zin = h @ lw["w_in"]
    q, k, v, zr, zg = jnp.split(
        zin, [ATT_Q, ATT_Q + ATT_KV, ATT_Q + 2 * ATT_KV, ATT_Q + 2 * ATT_KV + RW_COLS], axis=-1)
    q = partial_rope(q.reshape(b, t, N_Q_HEADS, HEAD_DIM), positions)
    k = partial_rope(k.reshape(b, t, N_KV_HEADS, HEAD_DIM), positions)
    v = v.reshape(b, t, N_KV_HEADS, HEAD_DIM)
    if cache_k is None:
        o_att, k_win, v_win = attention_prompt(q, k, v, lw["attn_sinks"])
    else:
        o_att, k_win, v_win = attention_sample(q, k, v, lw["attn_sinks"], cache_k, cache_v)
    o_rw, S_new, shift_new = rwkv7_time_mix(zr, S0, shift0, lw)
    gates = jax.nn.sigmoid(zg.astype(jnp.float32))
    merged = (gates[..., :D_MODEL] * (o_att @ lw["w_branch_attn"]).astype(jnp.float32)
              + gates[..., D_MODEL:] * (o_rw @ lw["w_branch_rwkv"]).astype(jnp.float32))
    x = x + merged.astype(x.dtype) @ lw["w_out"]
    x = x + moe_ffn(rmsnorm(x, lw["norm_ffn_g"]), lw)
    return x, k_win, v_win, S_new, shift_new


def setup_inputs(seed: int = 0) -> dict:
    key = jax.random.key(seed)
    ks = jax.random.split(key, 40)
    ctr = [0]

    def nxt():
        kk = ks[ctr[0]]
        ctr[0] += 1
        return kk

    def nrm(shape, scale):
        return jax.random.normal(nxt(), shape, jnp.float32) * scale

    def unif(shape, lo, hi):
        return jax.random.uniform(nxt(), shape, jnp.float32, lo, hi)

    L = DEPTH
    return {
        "x_prompt": nrm((BATCH, SEQ, D_MODEL), 1.0),
        "x_sample": nrm((DEC_BATCH, DEC_SEQ, D_MODEL), 1.0),
        "cache_k": nrm((L, DEC_BATCH, WINDOW, N_KV_HEADS, HEAD_DIM), 1.0),
        "cache_v": nrm((L, DEC_BATCH, WINDOW, N_KV_HEADS, HEAD_DIM), 1.0),
        "state_wkv": nrm((L, DEC_BATCH, RW_HEADS, RW_HEAD, RW_HEAD), 0.3),
        "state_shift": nrm((L, DEC_BATCH, RW_COLS), 1.0),
        "norm_mix_g": 1.0 + nrm((L, D_MODEL), 0.05),
        "w_in": nrm((L, D_MODEL, IN_COLS), D_MODEL ** -0.5),
        "attn_sinks": nrm((L, N_Q_HEADS), 0.5),
        "rw_mu": unif((L, RW_COLS), 0.0, 1.0),
        "rw_w0": unif((L, RW_WIDTH), -5.0, 0.0),
        "rw_w_up": nrm((L, W_LORA, RW_WIDTH), W_LORA ** -0.5),
        "rw_a0": nrm((L, RW_WIDTH), 0.1),
        "rw_a_up": nrm((L, A_LORA, RW_WIDTH), A_LORA ** -0.5),
        "rw_g_up": nrm((L, G_LORA, RW_WIDTH), G_LORA ** -0.5),
        "rw_k_k": 0.85 + nrm((L, RW_WIDTH), 0.05),
        "rw_k_a": 1.0 + nrm((L, RW_WIDTH), 0.05),
        "rw_r_k": nrm((L, RW_HEADS, RW_HEAD), 0.1),
        "rw_lnx_g": 1.0 + nrm((L, RW_WIDTH), 0.05),
        "rw_lnx_b": nrm((L, RW_WIDTH), 0.01),
        "w_branch_attn": nrm((L, ATT_Q, D_MODEL), ATT_Q ** -0.5),
        "w_branch_rwkv": nrm((L, RW_WIDTH, D_MODEL), RW_WIDTH ** -0.5),
        "w_out": nrm((L, D_MODEL, D_MODEL), D_MODEL ** -0.5),
        "norm_ffn_g": 1.0 + nrm((L, D_MODEL), 0.05),
        "w_router": nrm((L, D_MODEL, N_EXPERTS), D_MODEL ** -0.5),
        "b_router": nrm((L, N_EXPERTS), 0.01),
        "w_gate_up": nrm((L, N_EXPERTS, D_MODEL, 2 * D_EXPERT), D_MODEL ** -0.5),
        "b_gate_up": nrm((L, N_EXPERTS, 2 * D_EXPERT), 0.01),
        "w_down": nrm((L, N_EXPERTS, D_EXPERT, D_MODEL), D_EXPERT ** -0.5),
        "b_down": nrm((L, N_EXPERTS, D_MODEL), 0.01),
        "norm_final_g": 1.0 + nrm((D_MODEL,), 0.05),
    }


def reference(x_prompt, x_sample, cache_k, cache_v, state_wkv, state_shift,
              norm_mix_g, w_in, attn_sinks, rw_mu, rw_w0, rw_w_up, rw_a0, rw_a_up, rw_g_up,
              rw_k_k, rw_k_a, rw_r_k, rw_lnx_g, rw_lnx_b, w_branch_attn, w_branch_rwkv, w_out,
              norm_ffn_g, w_router, b_router, w_gate_up, b_gate_up, w_down, b_down, norm_final_g):
    bp = x_prompt.shape[0]
    pos_p = jnp.arange(x_prompt.shape[1], dtype=jnp.int32)
    pos_s = PAST_LEN + jnp.arange(x_sample.shape[1], dtype=jnp.int32)
    h_p, h_s = x_prompt, x_sample
    kp_l, vp_l, sp_l, shp_l = [], [], [], []
    ks_l, vs_l, ss_l, shs_l = [], [], [], []
    for l in range(DEPTH):
        lw = {
            "norm_mix_g": norm_mix_g[l], "w_in": w_in[l], "attn_sinks": attn_sinks[l],
            "rw_mu": rw_mu[l], "rw_w0": rw_w0[l], "rw_w_up": rw_w_up[l], "rw_a0": rw_a0[l],
            "rw_a_up": rw_a_up[l], "rw_g_up": rw_g_up[l], "rw_k_k": rw_k_k[l], "rw_k_a": rw_k_a[l],
            "rw_r_k": rw_r_k[l], "rw_lnx_g": rw_lnx_g[l], "rw_lnx_b": rw_lnx_b[l],
            "w_branch_attn": w_branch_attn[l], "w_branch_rwkv": w_branch_rwkv[l], "w_out": w_out[l],
            "norm_ffn_g": norm_ffn_g[l], "w_router": w_router[l], "b_router": b_router[l],
            "w_gate_up": w_gate_up[l], "b_gate_up": b_gate_up[l], "w_down": w_down[l], "b_down": b_down[l],
        }
        S0p = jnp.zeros((bp, RW_HEADS, RW_HEAD, RW_HEAD), state_wkv.dtype)
        sh0p = jnp.zeros((bp, RW_COLS), state_shift.dtype)
        h_p, kp, vp, sp, shp = layer_forward(h_p, pos_p, None, None, S0p, sh0p, lw)
        h_s, kss, vss, sss, shs = layer_forward(h_s, pos_s, cache_k[l], cache_v[l],
                                                state_wkv[l], state_shift[l], lw)
        kp_l.append(kp); vp_l.append(vp); sp_l.append(sp); shp_l.append(shp)
        ks_l.append(kss); vs_l.append(vss); ss_l.append(sss); shs_l.append(shs)
    y_prompt = rmsnorm(h_p, norm_final_g)
    y_sample = rmsnorm(h_s, norm_final_g)
    return (y_prompt, y_sample,
            jnp.stack(kp_l), jnp.stack(vp_l), jnp.stack(sp_l), jnp.stack(shp_l),
            jnp.stack(ks_l), jnp.stack(vs_l), jnp.stack(ss_l), jnp.stack(shs_l))
```

```python
import functools
import math

import jax
import jax.numpy as jnp
from jax import lax
from jax.experimental import pallas as pl
from jax.experimental.pallas import tpu as pltpu

F32 = jnp.float32
BF16 = jnp.bfloat16

LANES = 128
SUBLANES = 8
VMEM_LIMIT_BYTES = 56 * 1024 * 1024

D_MODEL = 1024
HEAD_DIM = 64
N_Q_HEADS = 16
N_KV_HEADS = 4
Q_PER_KV = 4
WINDOW = 128
ROPE_THETA = 500000.0
ROPE_DIM = 16
ROPE_HALF = 8
ATTN_SCALE = HEAD_DIM ** -0.5
PAST_LEN = 16384
RW_HEAD = 64
RW_HEADS = 16
RW_PAIRS = RW_HEADS // 2
W_LORA = 64
A_LORA = 64
G_LORA = 128
LNX_EPS = 64e-5
N_EXPERTS = 32
TOP_K = 4
D_EXPERT = 1024
SWIGLU_LIMIT = 7.0
SWIGLU_ALPHA = 1.702
RMS_EPS = 1e-5
ATT_Q = N_Q_HEADS * HEAD_DIM
ATT_KV = N_KV_HEADS * HEAD_DIM
RW_WIDTH = RW_HEADS * RW_HEAD
RW_COLS = 3 * RW_WIDTH + W_LORA + A_LORA + G_LORA
GATE_COLS = 2 * D_MODEL
IN_COLS = ATT_Q + 2 * ATT_KV + RW_COLS + GATE_COLS
DECAY_SCALE = math.exp(-0.5)


def _nn(a, b):
    return jnp.dot(a, b, preferred_element_type=F32)


def _nt(a, b):
    return lax.dot_general(a, b, (((1,), (1,)), ((), ())), preferred_element_type=F32)


def _tn(a, b):
    return lax.dot_general(a, b, (((0,), (0,)), ((), ())), preferred_element_type=F32)


def _bf(x):
    return x.astype(BF16)


def _sigmoid(x):
    return 1.0 / (1.0 + jnp.exp(-x))


def _split2(x):
    hi = x.astype(BF16)
    lo = (x - hi.astype(F32)).astype(BF16)
    return hi, lo


def _split3(x):
    hi = x.astype(BF16)
    r1 = x - hi.astype(F32)
    mid = r1.astype(BF16)
    lo = (r1 - mid.astype(F32)).astype(BF16)
    return hi, mid, lo


def _params(n_axes):
    return pltpu.CompilerParams(
        dimension_semantics=("arbitrary",) * n_axes, vmem_limit_bytes=VMEM_LIMIT_BYTES)


_IN_CHUNK = 512


def _inproj_kernel(x_ref, g_ref, w_ref, cos_ref, sa_ref, sb_ref,
                   q_ref, k_ref, v_ref, zr_ref, zg_ref):
    x = x_ref[...]
    ms = jnp.mean(x * x, axis=-1, keepdims=True)
    h = _bf(x * lax.rsqrt(ms + RMS_EPS) * g_ref[...])
    cos, sa, sb = cos_ref[...], sa_ref[...], sb_ref[...]

    def rope(y):
        return (y * cos + pltpu.roll(y, LANES - ROPE_HALF, axis=1) * sa
                + pltpu.roll(y, ROPE_HALF, axis=1) * sb)

    def project(out_ref, col0, width, with_rope):
        for c in range(0, width, _IN_CHUNK):
            cw = min(_IN_CHUNK, width - c)
            acc = _nn(h, w_ref[:, col0 + c:col0 + c + cw])
            if with_rope:
                for j in range(0, cw, LANES):
                    out_ref[:, c + j:c + j + LANES] = rope(acc[:, j:j + LANES])
            else:
                out_ref[:, c:c + cw] = acc

    project(q_ref, 0, ATT_Q, True)
    project(k_ref, ATT_Q, ATT_KV, True)
    project(v_ref, ATT_Q + ATT_KV, ATT_KV, False)
    project(zr_ref, ATT_Q + 2 * ATT_KV, RW_COLS, False)
    project(zg_ref, ATT_Q + 2 * ATT_KV + RW_COLS, GATE_COLS, False)


def _rope_tables(positions):
    inv_freq = ROPE_THETA ** (-jnp.arange(ROPE_HALF, dtype=F32) / ROPE_HALF)
    ang = positions.astype(F32)[:, None] * inv_freq[None, :]
    cos, sin = jnp.cos(ang), jnp.sin(ang)
    t = positions.shape[0]
    one = jnp.ones((t, HEAD_DIM - ROPE_DIM), F32)
    zero = jnp.zeros((t, HEAD_DIM - ROPE_DIM), F32)
    z8 = jnp.zeros((t, ROPE_HALF), F32)
    cos_h = jnp.concatenate([cos, cos, one], axis=1)
    sa_h = jnp.concatenate([-sin, z8, zero], axis=1)
    sb_h = jnp.concatenate([z8, sin, zero], axis=1)
    two = lambda u: jnp.concatenate([u, u], axis=1)
    return two(cos_h), two(sa_h), two(sb_h)


def _inproj(x2d, g, w_bf, tables, tm):
    n = x2d.shape[0]
    cos, sa, sb = tables
    nper = cos.shape[0] // tm
    row = lambda i: (i, 0)
    tab = lambda i: (i % nper, 0)
    const = lambda i: (0, 0)
    out_shapes = [jax.ShapeDtypeStruct((n, w), F32) for w in (ATT_Q, ATT_KV, ATT_KV, RW_COLS, GATE_COLS)]
    return pl.pallas_call(
        _inproj_kernel,
        grid=(n // tm,),
        in_specs=[
            pl.BlockSpec((tm, D_MODEL), row),
            pl.BlockSpec((1, D_MODEL), const),
            pl.BlockSpec((D_MODEL, IN_COLS), const, pipeline_mode=pl.Buffered(1)),
            pl.BlockSpec((tm, LANES), tab),
            pl.BlockSpec((tm, LANES), tab),
            pl.BlockSpec((tm, LANES), tab),
        ],
        out_specs=[pl.BlockSpec((tm, w), row) for w in (ATT_Q, ATT_KV, ATT_KV, RW_COLS, GATE_COLS)],
        out_shape=out_shapes,
        compiler_params=_params(1),
    )(x2d, g, w_bf, cos, sa, sb)


def _attn_prompt_kernel(sink_ref, q_ref, kp_ref, kc_ref, vp_ref, vc_ref, o_ref, kw_ref, vw_ref):
    n = pl.program_id(1)
    w = WINDOW
    lane = lax.broadcasted_iota(jnp.int32, (w, LANES), 1)
    first_half = lane < HEAD_DIM
    qi = lax.broadcasted_iota(jnp.int32, (Q_PER_KV * w, 2 * w), 0) % w
    kj = lax.broadcasted_iota(jnp.int32, (Q_PER_KV * w, 2 * w), 1)
    first_key = jnp.where(n > 0, 0, w)
    band4 = (kj > qi) & (kj <= qi + w) & (kj >= first_key)

    kw_ref[...] = kc_ref[...]
    vw_ref[...] = vc_ref[...]

    for grp in range(ATT_KV // LANES):
        gs = slice(grp * LANES, (grp + 1) * LANES)
        k2 = _bf(jnp.concatenate([kp_ref[:, gs], kc_ref[:, gs]], axis=0))
        v2 = _bf(jnp.concatenate([vp_ref[:, gs], vc_ref[:, gs]], axis=0))
        for jh in range(2):
            kv = grp * 2 + jh
            keep = first_half if jh == 0 else jnp.logical_not(first_half)
            rows, sinks = [], []
            for gq in range(Q_PER_KV):
                hq = kv * Q_PER_KV + gq
                qg = q_ref[:, (hq // 2) * LANES:(hq // 2 + 1) * LANES]
                if hq % 2 != jh:
                    qg = pltpu.roll(qg, HEAD_DIM, axis=1)
                rows.append(jnp.where(keep, qg, 0.0))
                sinks.append(jnp.full((w, 1), sink_ref[hq], F32))
            qs = _bf(jnp.concatenate(rows, axis=0))
            sink = jnp.concatenate(sinks, axis=0)
            s = _nt(qs, k2) * ATTN_SCALE
            s = jnp.where(band4, s, -jnp.inf)
            m = jnp.maximum(jnp.max(s, axis=-1, keepdims=True), sink)
            e = jnp.exp(s - m)
            denom = jnp.sum(e, axis=-1, keepdims=True) + jnp.exp(sink - m)
            pv = _nn(_bf(e / denom), v2)
            for go in range(Q_PER_KV // 2):
                pa, pb = pv[2 * go * w:(2 * go + 1) * w], pv[(2 * go + 1) * w:(2 * go + 2) * w]
                if jh == 0:
                    pb = pltpu.roll(pb, HEAD_DIM, axis=1)
                else:
                    pa = pltpu.roll(pa, HEAD_DIM, axis=1)
                og = kv * (Q_PER_KV // 2) + go
                o_ref[:, og * LANES:(og + 1) * LANES] = jnp.where(first_half, pa, pb).astype(o_ref.dtype)


def _attn_prompt(q, k, v, sinks):
    b, t, _ = q.shape
    nb = t // WINDOW
    cur = lambda bi, n: (bi, n, 0)
    prev = lambda bi, n: (bi, jnp.maximum(n - 1, 0), 0)
    win = lambda bi, n: (bi, 0, 0)
    return pl.pallas_call(
        _attn_prompt_kernel,
        grid=(b, nb),
        in_specs=[
            pl.BlockSpec(memory_space=pltpu.SMEM),
            pl.BlockSpec((None, WINDOW, ATT_Q), cur),
            pl.BlockSpec((None, WINDOW, ATT_KV), prev),
            pl.BlockSpec((None, WINDOW, ATT_KV), cur),
            pl.BlockSpec((None, WINDOW, ATT_KV), prev),
            pl.BlockSpec((None, WINDOW, ATT_KV), cur),
        ],
        out_specs=[
            pl.BlockSpec((None, WINDOW, ATT_Q), cur),
            pl.BlockSpec((None, WINDOW, ATT_KV), win),
            pl.BlockSpec((None, WINDOW, ATT_KV), win),
        ],
        out_shape=[
            jax.ShapeDtypeStruct((b, t, ATT_Q), BF16),
            jax.ShapeDtypeStruct((b, WINDOW, ATT_KV), F32),
            jax.ShapeDtypeStruct((b, WINDOW, ATT_KV), F32),
        ],
        compiler_params=_params(2),
    )(sinks, q, k, k, v, v)


_SAMPLE_BT = 8


def _attn_sample_kernel(sink_ref, q_ref, k_ref, v_ref, ck_ref, cv_ref, o_ref, nk_ref, nv_ref, *, t):
    w = WINDOW
    rows_per_grp = 2 * Q_PER_KV * t
    lane = lax.broadcasted_iota(jnp.int32, (t, LANES), 1)
    first_half = lane < HEAD_DIM
    r_c = lax.broadcasted_iota(jnp.int32, (rows_per_grp, w), 0) % t
    c_c = lax.broadcasted_iota(jnp.int32, (rows_per_grp, w), 1)
    mask_c = c_c > r_c
    r_n = lax.broadcasted_iota(jnp.int32, (rows_per_grp, t), 0) % t
    c_n = lax.broadcasted_iota(jnp.int32, (rows_per_grp, t), 1)
    mask_n = c_n <= r_n

    for bi in range(_SAMPLE_BT):
        nk_ref[bi, 0:w - t, :] = ck_ref[bi, t:w, :]
        nk_ref[bi, w - t:w, :] = k_ref[bi]
        nv_ref[bi, 0:w - t, :] = cv_ref[bi, t:w, :]
        nv_ref[bi, w - t:w, :] = v_ref[bi]
        for grp in range(ATT_KV // LANES):
            gs = slice(grp * LANES, (grp + 1) * LANES)
            kc, vc = _bf(ck_ref[bi, :, gs]), _bf(cv_ref[bi, :, gs])
            kn, vn = _bf(k_ref[bi, :, gs]), _bf(v_ref[bi, :, gs])
            rows, sinks = [], []
            for jh in range(2):
                kv = grp * 2 + jh
                keep = first_half if jh == 0 else jnp.logical_not(first_half)
                for gq in range(Q_PER_KV):
                    hq = kv * Q_PER_KV + gq
                    qg = q_ref[bi, :, (hq // 2) * LANES:(hq // 2 + 1) * LANES]
                    if hq % 2 != jh:
                        qg = pltpu.roll(qg, HEAD_DIM, axis=1)
                    rows.append(jnp.where(keep, qg, 0.0))
                    sinks.append(jnp.full((t, 1), sink_ref[hq], F32))
            qs = _bf(jnp.concatenate(rows, axis=0))
            sink = jnp.concatenate(sinks, axis=0)
            s_c = jnp.where(mask_c, _nt(qs, kc) * ATTN_SCALE, -jnp.inf)
            s_n = jnp.where(mask_n, _nt(qs, kn) * ATTN_SCALE, -jnp.inf)
            m = jnp.maximum(jnp.maximum(jnp.max(s_c, axis=-1, keepdims=True),
                                        jnp.max(s_n, axis=-1, keepdims=True)), sink)
            e_c, e_n = jnp.exp(s_c - m), jnp.exp(s_n - m)
            denom = (jnp.sum(e_c, axis=-1, keepdims=True) + jnp.sum(e_n, axis=-1, keepdims=True)
                     + jnp.exp(sink - m))
            pv = _nn(_bf(e_c / denom), vc) + _nn(_bf(e_n / denom), vn)
            for jh in range(2):
                for go in range(Q_PER_KV // 2):
                    r0 = (jh * Q_PER_KV + 2 * go) * t
                    pa, pb = pv[r0:r0 + t], pv[r0 + t:r0 + 2 * t]
                    if jh == 0:
                        pb = pltpu.roll(pb, HEAD_DIM, axis=1)
                    else:
                        pa = pltpu.roll(pa, HEAD_DIM, axis=1)
                    og = (grp * 2 + jh) * (Q_PER_KV // 2) + go
                    o_ref[bi, :, og * LANES:(og + 1) * LANES] = jnp.where(first_half, pa, pb)


def _attn_sample(q, k, v, sinks, cache_k, cache_v):
    b, t, _ = q.shape
    bt = _SAMPLE_BT
    blk = lambda i: (i, 0, 0)
    return pl.pallas_call(
        functools.partial(_attn_sample_kernel, t=t),
        grid=(b // bt,),
        in_specs=[
            pl.BlockSpec(memory_space=pltpu.SMEM),
            pl.BlockSpec((bt, t, ATT_Q), blk),
            pl.BlockSpec((bt, t, ATT_KV), blk),
            pl.BlockSpec((bt, t, ATT_KV), blk),
            pl.BlockSpec((bt, WINDOW, ATT_KV), blk),
            pl.BlockSpec((bt, WINDOW, ATT_KV), blk),
        ],
        out_specs=[
            pl.BlockSpec((bt, t, ATT_Q), blk),
            pl.BlockSpec((bt, WINDOW, ATT_KV), blk),
            pl.BlockSpec((bt, WINDOW, ATT_KV), blk),
        ],
        out_shape=[
            jax.ShapeDtypeStruct((b, t, ATT_Q), F32),
            jax.ShapeDtypeStruct((b, WINDOW, ATT_KV), F32),
            jax.ShapeDtypeStruct((b, WINDOW, ATT_KV), F32),
        ],
        compiler_params=_params(1),
    )(sinks, q, k, v, cache_k, cache_v)


def _rwkv_kernel(z_ref, s0_ref, shift0_ref, mu_ref, w0_ref, wa_up_ref, a0_ref, g_up_ref,
                 kk_ref, ka_ref, rk_ref, lng_ref, lnb_ref, o_ref, s_ref, prev_ref, *, c):
    step = pl.program_id(1)

    @pl.when(step == 0)
    def _():
        s_ref[...] = s0_ref[...]
        prev_ref[...] = shift0_ref[...]

    z = z_ref[...]
    row1 = lax.broadcasted_iota(jnp.int32, (c, 1), 0)
    zprev = jnp.where(row1 == 0, prev_ref[...], pltpu.roll(z, 1, axis=0))
    prev_ref[...] = z[c - 1:c, :]
    zs = z + (zprev - z) * mu_ref[...]

    w3 = 3 * RW_WIDTH
    r, k, v = zs[:, 0:RW_WIDTH], zs[:, RW_WIDTH:2 * RW_WIDTH], zs[:, 2 * RW_WIDTH:w3]
    xwa = zs[:, w3:w3 + LANES]
    xg = zs[:, w3 + LANES:w3 + 2 * LANES]
    lane = lax.broadcasted_iota(jnp.int32, (c, LANES), 1)
    head0 = lane < RW_HEAD
    lora = _nn(_bf(jnp.where(head0, jnp.tanh(xwa), xwa)), wa_up_ref[...])
    lw = -DECAY_SCALE * _sigmoid(w0_ref[...] + lora[:, 0:RW_WIDTH])
    a_sig = _sigmoid(a0_ref[...] + lora[:, RW_WIDTH:2 * RW_WIDTH])
    g = _nn(_bf(_sigmoid(xg)), g_up_ref[...])
    kk = k * kk_ref[...]
    k = k * (1.0 + (a_sig - 1.0) * ka_ref[...])
    rkr = r * k * rk_ref[...]

    ti = lax.broadcasted_iota(jnp.int32, (c, c), 0)
    tj = lax.broadcasted_iota(jnp.int32, (c, c), 1)
    tri = jnp.where(tj <= ti, 1.0, 0.0).astype(BF16)
    lw_hi, lw_mid, lw_lo = _split3(lw)
    cum = _nn(tri, lw_hi) + _nn(tri, lw_mid) + _nn(tri, lw_lo)
    c_end = cum[c - 1:c, :]
    e_c, e_ci, e_cm = jnp.exp(cum), jnp.exp(-cum), jnp.exp(cum - lw)
    e_ce, w_end = jnp.exp(c_end - cum), jnp.exp(c_end)

    gi = lax.broadcasted_iota(jnp.int32, (2 * LANES, LANES), 0) % LANES
    gj = lax.broadcasted_iota(jnp.int32, (2 * LANES, LANES), 1)
    ones2 = jnp.where((gi // RW_HEAD) == (gj // RW_HEAD), 1.0, 0.0).astype(BF16)
    bi_ = lax.broadcasted_iota(jnp.int32, (LANES, LANES), 0)
    bj_ = lax.broadcasted_iota(jnp.int32, (LANES, LANES), 1)
    same_head = (bi_ // RW_HEAD) == (bj_ // RW_HEAD)
    ci = lax.broadcasted_iota(jnp.int32, (c, 2 * c), 0)
    cj = lax.broadcasted_iota(jnp.int32, (c, 2 * c), 1)
    strict = (cj % c) < ci
    incl = (cj % c) <= ci
    eye_cat = jnp.where((cj % c) == ci, 1.0, 0.0)
    left = cj < c

    def seg_sum(x):
        hi, lo = _split2(x)
        return _nn(jnp.concatenate([hi, lo], axis=1), ones2)

    def rows2(x):
        return jnp.concatenate([jnp.where(head0, x, 0.0), jnp.where(head0, 0.0, x)], axis=0)

    def bd(cat):
        return jnp.concatenate([jnp.where(left, cat, 0.0), jnp.where(left, 0.0, cat)], axis=0)

    def pair_mm(cat, x):
        return _nn(_bf(cat), _bf(rows2(x)))

    n_double = int(math.log2(c)) - 1
    for p in range(RW_PAIRS):
        sl = slice(p * LANES, (p + 1) * LANES)
        kk_p = kk[:, sl]
        kkn = kk_p / jnp.maximum(jnp.sqrt(seg_sum(kk_p * kk_p)), 1e-12)
        a_p, r_p, k_p, v_p = a_sig[:, sl], r[:, sl], k[:, sl], v[:, sl]
        bv = kkn * a_p
        rt, kt = r_p * e_c[:, sl], k_p * e_ci[:, sl]
        at, bt = -kkn * e_cm[:, sl], bv * e_ci[:, sl]
        bh, kh = bv * e_ce[:, sl], k_p * e_ce[:, sl]

        ar = _bf(jnp.concatenate([at, rt], axis=0))
        xb = _nt(ar, _bf(rows2(bt)))
        xk = _nt(ar, _bf(rows2(kt)))
        l_ab = jnp.where(strict, xb[0:c], 0.0)
        l_ak = jnp.where(strict, xk[0:c], 0.0)
        m_rb = jnp.where(incl, xb[c:2 * c], 0.0)
        m_rk = jnp.where(incl, xk[c:2 * c], 0.0)

        t_inv = l_ab + eye_cat
        pw = l_ab
        for _ in range(n_double):
            pw = _nn(_bf(pw), _bf(bd(pw)))
            t_inv = t_inv + _nn(_bf(t_inv), _bf(bd(pw)))

        a_hat = pair_mm(t_inv, at)
        u0 = pair_mm(t_inv, pair_mm(l_ak, v_p))
        y1 = pair_mm(m_rk, v_p)

        s_old = s_ref[p]
        pp = _nt(_bf(jnp.concatenate([a_hat, rt], axis=0)), _bf(s_old))
        u = pp[0:c] + u0
        y = pp[c:2 * c] + pair_mm(m_rb, u) + y1
        upd = _tn(_bf(jnp.concatenate([u, v_p], axis=0)), _bf(jnp.concatenate([bh, kh], axis=0)))
        s_ref[p] = s_old * w_end[:, sl] + jnp.where(same_head, upd, 0.0)

        mean = seg_sum(y) * (1.0 / RW_HEAD)
        d = y - mean
        var = seg_sum(d * d) * (1.0 / RW_HEAD)
        yn = d * lax.rsqrt(var + LNX_EPS) * lng_ref[:, sl] + lnb_ref[:, sl]
        bonus = seg_sum(rkr[:, sl]) * v_p
        o_ref[:, sl] = ((yn + bonus) * g[:, sl]).astype(o_ref.dtype)


def _rwkv(zr, s0_bd, shift0, lw, c):
    b, t, _ = zr.shape
    nchunk = t // c
    vec = lambda name: lw[name].reshape(1, -1).astype(F32)
    cst = lambda bi, i: (0, 0)
    vspec = lambda wd: pl.BlockSpec((1, wd), cst)
    return pl.pallas_call(
        functools.partial(_rwkv_kernel, c=c),
        grid=(b, nchunk),
        in_specs=[
            pl.BlockSpec((None, c, RW_COLS), lambda bi, i: (bi, i, 0)),
            pl.BlockSpec((None, RW_PAIRS, LANES, LANES), lambda bi, i: (bi, 0, 0, 0)),
            pl.BlockSpec((None, 1, RW_COLS), lambda bi, i: (bi, 0, 0)),
            vspec(RW_COLS), vspec(RW_WIDTH),
            pl.BlockSpec((LANES, 2 * RW_WIDTH), cst),
            vspec(RW_WIDTH),
            pl.BlockSpec((G_LORA, RW_WIDTH), cst),
            vspec(RW_WIDTH), vspec(RW_WIDTH), vspec(RW_WIDTH), vspec(RW_WIDTH), vspec(RW_WIDTH),
        ],
        out_specs=[
            pl.BlockSpec((None, c, RW_WIDTH), lambda bi, i: (bi, i, 0)),
            pl.BlockSpec((None, RW_PAIRS, LANES, LANES), lambda bi, i: (bi, 0, 0, 0)),
        ],
        out_shape=[
            jax.ShapeDtypeStruct((b, t, RW_WIDTH), BF16 if c % (2 * SUBLANES) == 0 else F32),
            jax.ShapeDtypeStruct((b, RW_PAIRS, LANES, LANES), F32),
        ],
        scratch_shapes=[pltpu.VMEM((1, RW_COLS), F32)],
        compiler_params=_params(2),
    )(zr, s0_bd, shift0.reshape(b, 1, RW_COLS), vec("rw_mu"), vec("rw_w0"), lw["wa_up"], vec("rw_a0"),
      lw["g_up"], vec("rw_k_k"), vec("rw_k_a"), vec("rw_r_k"), vec("rw_lnx_g"), vec("rw_lnx_b"))


def _state_to_bd(s):
    b = s.shape[0]
    s5 = s.reshape(b, RW_PAIRS, 2, RW_HEAD, RW_HEAD)
    eye = jnp.eye(2, dtype=s.dtype)
    bd = s5[:, :, :, :, None, :] * eye[None, None, :, None, :, None]
    return bd.reshape(b, RW_PAIRS, LANES, LANES)


def _state_from_bd(bd):
    b = bd.shape[0]
    s6 = bd.reshape(b, RW_PAIRS, 2, RW_HEAD, 2, RW_HEAD)
    d = jnp.stack([s6[:, :, 0, :, 0, :], s6[:, :, 1, :, 1, :]], axis=2)
    return d.reshape(b, RW_HEADS, RW_HEAD, RW_HEAD)


def _mix_kernel(x_ref, oa_ref, or_ref, zg_ref, wba_ref, wbr_ref, wo_ref, g_ref, wr_ref, br_ref,
                x1_ref, hn_ref, comb_ref):
    ya = _nn(_bf(oa_ref[...]), wba_ref[...])
    yr = _nn(_bf(or_ref[...]), wbr_ref[...])
    merged = _sigmoid(zg_ref[:, 0:D_MODEL]) * ya + _sigmoid(zg_ref[:, D_MODEL:2 * D_MODEL]) * yr
    x1 = x_ref[...] + _nn(_bf(merged), wo_ref[...])
    x1_ref[...] = x1
    ms = jnp.mean(x1 * x1, axis=-1, keepdims=True)
    hn = x1 * lax.rsqrt(ms + RMS_EPS) * g_ref[...]
    hn_ref[...] = _bf(hn)
    logits = _nn(_bf(hn), wr_ref[...]) + br_ref[...]
    lane = lax.broadcasted_iota(jnp.int32, logits.shape, 1).astype(F32)
    work = logits
    top = None
    for _ in range(TOP_K):
        m = jnp.max(work, axis=-1, keepdims=True)
        if top is None:
            top = m
        idx = jnp.min(jnp.where(work == m, lane, float(LANES)), axis=-1, keepdims=True)
        work = jnp.where(lane == idx, -jnp.inf, work)
    e = jnp.where(work != logits, jnp.exp(logits - top), 0.0)
    comb_ref[...] = e / jnp.sum(e, axis=-1, keepdims=True)


def _mix(x2d, oa, orw, zg, lw, tm):
    n = x2d.shape[0]
    row = lambda i: (i, 0)
    cst = lambda i: (0, 0)
    wspec = pl.BlockSpec((D_MODEL, D_MODEL), cst)
    return pl.pallas_call(
        _mix_kernel,
        grid=(n // tm,),
        in_specs=[
            pl.BlockSpec((tm, D_MODEL), row), pl.BlockSpec((tm, D_MODEL), row), pl.BlockSpec((tm, D_MODEL), row),
            pl.BlockSpec((tm, GATE_COLS), row),
            wspec, wspec, wspec,
            pl.BlockSpec((1, D_MODEL), cst),
            pl.BlockSpec((D_MODEL, LANES), cst),
            pl.BlockSpec((1, LANES), cst),
        ],
        out_specs=[pl.BlockSpec((tm, D_MODEL), row), pl.BlockSpec((tm, D_MODEL), row),
                   pl.BlockSpec((tm, LANES), row)],
        out_shape=[jax.ShapeDtypeStruct((n, D_MODEL), F32), jax.ShapeDtypeStruct((n, D_MODEL), BF16),
                   jax.ShapeDtypeStruct((n, LANES), F32)],
        compiler_params=_params(1),
    )(x2d, oa, orw, zg, lw["w_ba"], lw["w_br"], lw["w_o"], lw["norm_ffn_g"], lw["w_r"], lw["b_r"])


def _moe_kernel(hn_ref, comb_ref, x1_ref, wg_ref, wu_ref, bg_ref, bu_ref, wd_ref, bd_ref, gf_ref,
                y_ref, acc_ref):
    e = pl.program_id(1)

    @pl.when(e == 0)
    def _():
        acc_ref[...] = x1_ref[...]

    h = hn_ref[...]
    glu = jnp.minimum(_nn(h, wg_ref[...]) + bg_ref[...], SWIGLU_LIMIT)
    lin = jnp.clip(_nn(h, wu_ref[...]) + bu_ref[...], -SWIGLU_LIMIT, SWIGLU_LIMIT)
    act = glu * _sigmoid(SWIGLU_ALPHA * glu) * (lin + 1.0)
    y = _nn(_bf(act), wd_ref[...]) + bd_ref[...]
    lane = lax.broadcasted_iota(jnp.int32, comb_ref.shape, 1)
    wcol = jnp.sum(jnp.where(lane == e, comb_ref[...], 0.0), axis=-1, keepdims=True)
    acc_ref[...] += wcol * y

    @pl.when(e == N_EXPERTS - 1)
    def _():
        xo = acc_ref[...]
        ms = jnp.mean(xo * xo, axis=-1, keepdims=True)
        y_ref[...] = xo * lax.rsqrt(ms + RMS_EPS) * gf_ref[...]


def _moe(hn, comb, x1, lw, gf, tm):
    n = hn.shape[0]
    row = lambda i, e: (i, 0)
    cst = lambda i, e: (0, 0)
    ex = lambda i, e: (e, 0, 0)
    return pl.pallas_call(
        _moe_kernel,
        grid=(n // tm, N_EXPERTS),
        in_specs=[
            pl.BlockSpec((tm, D_MODEL), row), pl.BlockSpec((tm, LANES), row), pl.BlockSpec((tm, D_MODEL), row),
            pl.BlockSpec((None, D_MODEL, D_EXPERT), ex), pl.BlockSpec((None, D_MODEL, D_EXPERT), ex),
            pl.BlockSpec((None, 1, D_EXPERT), ex), pl.BlockSpec((None, 1, D_EXPERT), ex),
            pl.BlockSpec((None, D_EXPERT, D_MODEL), ex), pl.BlockSpec((None, 1, D_MODEL), ex),
            pl.BlockSpec((1, D_MODEL), cst),
        ],
        out_specs=pl.BlockSpec((tm, D_MODEL), row),
        out_shape=jax.ShapeDtypeStruct((n, D_MODEL), F32),
        scratch_shapes=[pltpu.VMEM((tm, D_MODEL), F32)],
        compiler_params=_params(2),
    )(hn, comb, x1, lw["w_gate"], lw["w_up"], lw["b_gate"], lw["b_up"], lw["w_down"], lw["b_down"], gf)


def _prep_layer(l, norm_mix_g, w_in, attn_sinks, rw_mu, rw_w0, rw_w_up, rw_a0, rw_a_up, rw_g_up,
                rw_k_k, rw_k_a, rw_r_k, rw_lnx_g, rw_lnx_b, w_branch_attn, w_branch_rwkv, w_out,
                norm_ffn_g, w_router, b_router, w_gate_up, b_gate_up, w_down, b_down):
    zeros = jnp.zeros((W_LORA, RW_WIDTH), F32)
    wa_up = jnp.concatenate([jnp.concatenate([rw_w_up[l], zeros], axis=1),
                             jnp.concatenate([zeros, rw_a_up[l]], axis=1)], axis=0)
    pad = LANES - N_EXPERTS
    return {
        "norm_mix_g": norm_mix_g[l].reshape(1, D_MODEL),
        "w_in": _bf(w_in[l]),
        "attn_sinks": attn_sinks[l].astype(F32),
        "rw_mu": rw_mu[l], "rw_w0": rw_w0[l], "rw_a0": rw_a0[l], "rw_k_k": rw_k_k[l], "rw_k_a": rw_k_a[l],
        "rw_r_k": rw_r_k[l], "rw_lnx_g": rw_lnx_g[l], "rw_lnx_b": rw_lnx_b[l],
        "wa_up": _bf(wa_up), "g_up": _bf(rw_g_up[l]),
        "w_ba": _bf(w_branch_attn[l]), "w_br": _bf(w_branch_rwkv[l]), "w_o": _bf(w_out[l]),
        "norm_ffn_g": norm_ffn_g[l].reshape(1, D_MODEL),
        "w_r": _bf(jnp.pad(w_router[l], ((0, 0), (0, pad)))),
        "b_r": jnp.pad(b_router[l], (0, pad), constant_values=-jnp.inf).reshape(1, LANES),
        "w_gate": _bf(w_gate_up[l][:, :, 0::2]), "w_up": _bf(w_gate_up[l][:, :, 1::2]),
        "b_gate": b_gate_up[l][:, None, 0::2], "b_up": b_gate_up[l][:, None, 1::2],
        "w_down": _bf(w_down[l]), "b_down": b_down[l][:, None, :],
    }


def _tile(n, pref):
    tm = pref
    while n % tm:
        tm //= 2
    return tm


def _layer(x, tables, cache_k, cache_v, s0, shift0, lw, gf, chunk):
    b, t, _ = x.shape
    n = b * t
    x2d = x.reshape(n, D_MODEL)
    tm = _tile(n, 256)
    q, k, v, zr, zg = _inproj(x2d, lw["norm_mix_g"], lw["w_in"], tables, tm)
    q3, k3, v3 = q.reshape(b, t, ATT_Q), k.reshape(b, t, ATT_KV), v.reshape(b, t, ATT_KV)
    if cache_k is None:
        o_att, k_win, v_win = _attn_prompt(q3, k3, v3, lw["attn_sinks"])
    else:
        o_att, k_win, v_win = _attn_sample(q3, k3, v3, lw["attn_sinks"],
                                           cache_k.reshape(b, WINDOW, ATT_KV), cache_v.reshape(b, WINDOW, ATT_KV))
    zr3 = zr.reshape(b, t, RW_COLS)
    o_rw, s_bd = _rwkv(zr3, _state_to_bd(s0), shift0, lw, chunk)
    x1, hn, comb = _mix(x2d, o_att.reshape(n, ATT_Q), o_rw.reshape(n, RW_WIDTH), zg, lw, tm)
    y = _moe(hn, comb, x1, lw, gf, _tile(n, 512))
    kv_shape = (b, WINDOW, N_KV_HEADS, HEAD_DIM)
    return (y.reshape(b, t, D_MODEL), k_win.reshape(kv_shape), v_win.reshape(kv_shape),
            _state_from_bd(s_bd), zr3[:, t - 1, :])


def kernel(x_prompt, x_sample, cache_k, cache_v, state_wkv, state_shift, norm_mix_g, w_in, attn_sinks, rw_mu, rw_w0, rw_w_up, rw_a0, rw_a_up, rw_g_up, rw_k_k, rw_k_a, rw_r_k, rw_lnx_g, rw_lnx_b, w_branch_attn, w_branch_rwkv, w_out, norm_ffn_g, w_router, b_router, w_gate_up, b_gate_up, w_down, b_down, norm_final_g):
    assert w_in.shape[0] == 1, "single-layer trunk"
    bp, tp, _ = x_prompt.shape
    bs, ts, _ = x_sample.shape
    lw = _prep_layer(0, norm_mix_g, w_in, attn_sinks, rw_mu, rw_w0, rw_w_up, rw_a0, rw_a_up, rw_g_up,
                     rw_k_k, rw_k_a, rw_r_k, rw_lnx_g, rw_lnx_b, w_branch_attn, w_branch_rwkv, w_out,
                     norm_ffn_g, w_router, b_router, w_gate_up, b_gate_up, w_down, b_down)
    gf = norm_final_g.reshape(1, D_MODEL)

    tm_p = _tile(bp * tp, 256)
    tab_p = _rope_tables(jnp.arange(tp, dtype=jnp.int32))
    tm_s = _tile(bs * ts, 256)
    pos_s = PAST_LEN + jnp.arange(ts, dtype=jnp.int32)
    tab_s = tuple(jnp.tile(u, (tm_s // ts, 1)) for u in _rope_tables(pos_s))
    del tm_p

    s0p = jnp.zeros((bp, RW_HEADS, RW_HEAD, RW_HEAD), state_wkv.dtype)
    sh0p = jnp.zeros((bp, RW_COLS), state_shift.dtype)
    yp, kp, vp, sp, shp = _layer(x_prompt, tab_p, None, None, s0p, sh0p, lw, gf, 64)
    ys, ks, vs, ss, shs = _layer(x_sample, tab_s, cache_k[0], cache_v[0], state_wkv[0], state_shift[0],
                                 lw, gf, ts)
    ex = lambda u: u[None]
    return (yp, ys, ex(kp), ex(vp), ex(sp), ex(shp), ex(ks), ex(vs), ex(ss), ex(shs))
```

```python
import functools
import math

import jax
import jax.numpy as jnp
from jax import lax
from jax.experimental import pallas as pl
from jax.experimental.pallas import tpu as pltpu

F32 = jnp.float32
BF16 = jnp.bfloat16

LANES = 128
SUBLANES = 8
VMEM_LIMIT_BYTES = 56 * 1024 * 1024

D_MODEL = 1024
HEAD_DIM = 64
N_Q_HEADS = 16
N_KV_HEADS = 4
Q_PER_KV = 4
WINDOW = 128
ROPE_THETA = 500000.0
ROPE_DIM = 16
ROPE_HALF = 8
ATTN_SCALE = HEAD_DIM ** -0.5
PAST_LEN = 16384
RW_HEAD = 64
RW_HEADS = 16
RW_PAIRS = RW_HEADS // 2
W_LORA = 64
A_LORA = 64
G_LORA = 128
LNX_EPS = 64e-5
N_EXPERTS = 32
TOP_K = 4
D_EXPERT = 1024
SWIGLU_LIMIT = 7.0
SWIGLU_ALPHA = 1.702
RMS_EPS = 1e-5
ATT_Q = N_Q_HEADS * HEAD_DIM
ATT_KV = N_KV_HEADS * HEAD_DIM
RW_WIDTH = RW_HEADS * RW_HEAD
RW_COLS = 3 * RW_WIDTH + W_LORA + A_LORA + G_LORA
GATE_COLS = 2 * D_MODEL
IN_COLS = ATT_Q + 2 * ATT_KV + RW_COLS + GATE_COLS
DECAY_SCALE = math.exp(-0.5)
RW_CHUNK = 64


def _nn(a, b):
    return jnp.dot(a, b, preferred_element_type=F32)


def _nt(a, b):
    return lax.dot_general(a, b, (((1,), (1,)), ((), ())), preferred_element_type=F32)


def _tn(a, b):
    return lax.dot_general(a, b, (((0,), (0,)), ((), ())), preferred_element_type=F32)


def _bf(x):
    return x.astype(BF16)


def _sigmoid(x):
    return 1.0 / (1.0 + jnp.exp(-x))


def _split2(x):
    hi = x.astype(BF16)
    lo = (x - hi.astype(F32)).astype(BF16)
    return hi, lo


def _split3(x):
    hi = x.astype(BF16)
    r1 = x - hi.astype(F32)
    mid = r1.astype(BF16)
    lo = (r1 - mid.astype(F32)).astype(BF16)
    return hi, mid, lo


def _params(n_axes):
    return pltpu.CompilerParams(
        dimension_semantics=("arbitrary",) * n_axes, vmem_limit_bytes=VMEM_LIMIT_BYTES)


_IN_CHUNK = 512


def _inproj_kernel(x_ref, g_ref, w_ref, cos_ref, sa_ref, sb_ref,
                   q_ref, k_ref, v_ref, zr_ref, zg_ref):
    x = x_ref[...]
    ms = jnp.mean(x * x, axis=-1, keepdims=True)
    h = _bf(x * lax.rsqrt(ms + RMS_EPS) * g_ref[...])
    cos, sa, sb = cos_ref[...], sa_ref[...], sb_ref[...]

    def rope(y):
        return (y * cos + pltpu.roll(y, LANES - ROPE_HALF, axis=1) * sa
                + pltpu.roll(y, ROPE_HALF, axis=1) * sb)

    def project(out_ref, col0, width, with_rope):
        for c in range(0, width, _IN_CHUNK):
            cw = min(_IN_CHUNK, width - c)
            acc = _nn(h, w_ref[:, col0 + c:col0 + c + cw])
            if with_rope:
                for j in range(0, cw, LANES):
                    out_ref[:, c + j:c + j + LANES] = rope(acc[:, j:j + LANES])
            else:
                out_ref[:, c:c + cw] = acc

    project(q_ref, 0, ATT_Q, True)
    project(k_ref, ATT_Q, ATT_KV, True)
    project(v_ref, ATT_Q + ATT_KV, ATT_KV, False)
    project(zr_ref, ATT_Q + 2 * ATT_KV, RW_COLS, False)
    project(zg_ref, ATT_Q + 2 * ATT_KV + RW_COLS, GATE_COLS, False)


def _rope_tables(positions):
    inv_freq = ROPE_THETA ** (-jnp.arange(ROPE_HALF, dtype=F32) / ROPE_HALF)
    ang = positions.astype(F32)[:, None] * inv_freq[None, :]
    cos, sin = jnp.cos(ang), jnp.sin(ang)
    t = positions.shape[0]
    one = jnp.ones((t, HEAD_DIM - ROPE_DIM), F32)
    zero = jnp.zeros((t, HEAD_DIM - ROPE_DIM), F32)
    z8 = jnp.zeros((t, ROPE_HALF), F32)
    cos_h = jnp.concatenate([cos, cos, one], axis=1)
    sa_h = jnp.concatenate([-sin, z8, zero], axis=1)
    sb_h = jnp.concatenate([z8, sin, zero], axis=1)
    two = lambda u: jnp.concatenate([u, u], axis=1)
    return two(cos_h), two(sa_h), two(sb_h)


def _inproj(x2d, g, w_bf, tables, tm):
    n = x2d.shape[0]
    cos, sa, sb = tables
    nper = cos.shape[0] // tm
    row = lambda i: (i, 0)
    tab = lambda i: (i % nper, 0)
    const = lambda i: (0, 0)
    out_shapes = [jax.ShapeDtypeStruct((n, w), F32) for w in (ATT_Q, ATT_KV, ATT_KV, RW_COLS, GATE_COLS)]
    return pl.pallas_call(
        _inproj_kernel,
        grid=(n // tm,),
        in_specs=[
            pl.BlockSpec((tm, D_MODEL), row),
            pl.BlockSpec((1, D_MODEL), const),
            pl.BlockSpec((D_MODEL, IN_COLS), const, pipeline_mode=pl.Buffered(1)),
            pl.BlockSpec((tm, LANES), tab),
            pl.BlockSpec((tm, LANES), tab),
            pl.BlockSpec((tm, LANES), tab),
        ],
        out_specs=[pl.BlockSpec((tm, w), row) for w in (ATT_Q, ATT_KV, ATT_KV, RW_COLS, GATE_COLS)],
        out_shape=out_shapes,
        compiler_params=_params(1),
    )(x2d, g, w_bf, cos, sa, sb)


def _attn_prompt_kernel(sink_ref, q_ref, kp_ref, kc_ref, vp_ref, vc_ref, o_ref, kw_ref, vw_ref):
    n = pl.program_id(1)
    w = WINDOW
    lane = lax.broadcasted_iota(jnp.int32, (w, LANES), 1)
    first_half = lane < HEAD_DIM
    qi = lax.broadcasted_iota(jnp.int32, (Q_PER_KV * w, 2 * w), 0) % w
    kj = lax.broadcasted_iota(jnp.int32, (Q_PER_KV * w, 2 * w), 1)
    first_key = jnp.where(n > 0, 0, w)
    band4 = (kj > qi) & (kj <= qi + w) & (kj >= first_key)

    kw_ref[...] = kc_ref[...]
    vw_ref[...] = vc_ref[...]

    for grp in range(ATT_KV // LANES):
        gs = slice(grp * LANES, (grp + 1) * LANES)
        k2 = _bf(jnp.concatenate([kp_ref[:, gs], kc_ref[:, gs]], axis=0))
        v2 = _bf(jnp.concatenate([vp_ref[:, gs], vc_ref[:, gs]], axis=0))
        for jh in range(2):
            kv = grp * 2 + jh
            keep = first_half if jh == 0 else jnp.logical_not(first_half)
            rows, sinks = [], []
            for gq in range(Q_PER_KV):
                hq = kv * Q_PER_KV + gq
                qg = q_ref[:, (hq // 2) * LANES:(hq // 2 + 1) * LANES]
                if hq % 2 != jh:
                    qg = pltpu.roll(qg, HEAD_DIM, axis=1)
                rows.append(jnp.where(keep, qg, 0.0))
                sinks.append(jnp.full((w, 1), sink_ref[hq], F32))
            qs = _bf(jnp.concatenate(rows, axis=0))
            sink = jnp.concatenate(sinks, axis=0)
            s = _nt(qs, k2) * ATTN_SCALE
            s = jnp.where(band4, s, -jnp.inf)
            m = jnp.maximum(jnp.max(s, axis=-1, keepdims=True), sink)
            e = jnp.exp(s - m)
            denom = jnp.sum(e, axis=-1, keepdims=True) + jnp.exp(sink - m)
            pv = _nn(_bf(e / denom), v2)
            for go in range(Q_PER_KV // 2):
                pa, pb = pv[2 * go * w:(2 * go + 1) * w], pv[(2 * go + 1) * w:(2 * go + 2) * w]
                if jh == 0:
                    pb = pltpu.roll(pb, HEAD_DIM, axis=1)
                else:
                    pa = pltpu.roll(pa, HEAD_DIM, axis=1)
                og = kv * (Q_PER_KV // 2) + go
                o_ref[:, og * LANES:(og + 1) * LANES] = jnp.where(first_half, pa, pb).astype(o_ref.dtype)


def _attn_prompt(q, k, v, sinks):
    b, t, _ = q.shape
    nb = t // WINDOW
    cur = lambda bi, n: (bi, n, 0)
    prev = lambda bi, n: (bi, jnp.maximum(n - 1, 0), 0)
    win = lambda bi, n: (bi, 0, 0)
    return pl.pallas_call(
        _attn_prompt_kernel,
        grid=(b, nb),
        in_specs=[
            pl.BlockSpec(memory_space=pltpu.SMEM),
            pl.BlockSpec((None, WINDOW, ATT_Q), cur),
            pl.BlockSpec((None, WINDOW, ATT_KV), prev),
            pl.BlockSpec((None, WINDOW, ATT_KV), cur),
            pl.BlockSpec((None, WINDOW, ATT_KV), prev),
            pl.BlockSpec((None, WINDOW, ATT_KV), cur),
        ],
        out_specs=[
            pl.BlockSpec((None, WINDOW, ATT_Q), cur),
            pl.BlockSpec((None, WINDOW, ATT_KV), win),
            pl.BlockSpec((None, WINDOW, ATT_KV), win),
        ],
        out_shape=[
            jax.ShapeDtypeStruct((b, t, ATT_Q), BF16),
            jax.ShapeDtypeStruct((b, WINDOW, ATT_KV), F32),
            jax.ShapeDtypeStruct((b, WINDOW, ATT_KV), F32),
        ],
        compiler_params=_params(2),
    )(sinks, q, k, k, v, v)


_SAMPLE_BT = 8


def _attn_sample_kernel(sink_ref, q_ref, k_ref, v_ref, ck_ref, cv_ref, o_ref, nk_ref, nv_ref, *, t):
    w = WINDOW
    rows_per_grp = 2 * Q_PER_KV * t
    lane = lax.broadcasted_iota(jnp.int32, (t, LANES), 1)
    first_half = lane < HEAD_DIM
    r_c = lax.broadcasted_iota(jnp.int32, (rows_per_grp, w), 0) % t
    c_c = lax.broadcasted_iota(jnp.int32, (rows_per_grp, w), 1)
    mask_c = c_c > r_c
    r_n = lax.broadcasted_iota(jnp.int32, (rows_per_grp, t), 0) % t
    c_n = lax.broadcasted_iota(jnp.int32, (rows_per_grp, t), 1)
    mask_n = c_n <= r_n

    for bi in range(_SAMPLE_BT):
        nk_ref[bi, 0:w - t, :] = ck_ref[bi, t:w, :]
        nk_ref[bi, w - t:w, :] = k_ref[bi]
        nv_ref[bi, 0:w - t, :] = cv_ref[bi, t:w, :]
        nv_ref[bi, w - t:w, :] = v_ref[bi]
        for grp in range(ATT_KV // LANES):
            gs = slice(grp * LANES, (grp + 1) * LANES)
            kc, vc = _bf(ck_ref[bi, :, gs]), _bf(cv_ref[bi, :, gs])
            kn, vn = _bf(k_ref[bi, :, gs]), _bf(v_ref[bi, :, gs])
            rows, sinks = [], []
            for jh in range(2):
                kv = grp * 2 + jh
                keep = first_half if jh == 0 else jnp.logical_not(first_half)
                for gq in range(Q_PER_KV):
                    hq = kv * Q_PER_KV + gq
                    qg = q_ref[bi, :, (hq // 2) * LANES:(hq // 2 + 1) * LANES]
                    if hq % 2 != jh:
                        qg = pltpu.roll(qg, HEAD_DIM, axis=1)
                    rows.append(jnp.where(keep, qg, 0.0))
                    sinks.append(jnp.full((t, 1), sink_ref[hq], F32))
            qs = _bf(jnp.concatenate(rows, axis=0))
            sink = jnp.concatenate(sinks, axis=0)
            s_c = jnp.where(mask_c, _nt(qs, kc) * ATTN_SCALE, -jnp.inf)
            s_n = jnp.where(mask_n, _nt(qs, kn) * ATTN_SCALE, -jnp.inf)
            m = jnp.maximum(jnp.maximum(jnp.max(s_c, axis=-1, keepdims=True),
                                        jnp.max(s_n, axis=-1, keepdims=True)), sink)
            e_c, e_n = jnp.exp(s_c - m), jnp.exp(s_n - m)
            denom = (jnp.sum(e_c, axis=-1, keepdims=True) + jnp.sum(e_n, axis=-1, keepdims=True)
                     + jnp.exp(sink - m))
            pv = _nn(_bf(e_c / denom), vc) + _nn(_bf(e_n / denom), vn)
            for jh in range(2):
                for go in range(Q_PER_KV // 2):
                    r0 = (jh * Q_PER_KV + 2 * go) * t
                    pa, pb = pv[r0:r0 + t], pv[r0 + t:r0 + 2 * t]
                    if jh == 0:
                        pb = pltpu.roll(pb, HEAD_DIM, axis=1)
                    else:
                        pa = pltpu.roll(pa, HEAD_DIM, axis=1)
                    og = (grp * 2 + jh) * (Q_PER_KV // 2) + go
                    o_ref[bi, :, og * LANES:(og + 1) * LANES] = jnp.where(first_half, pa, pb)


def _attn_sample(q, k, v, sinks, cache_k, cache_v):
    b, t, _ = q.shape
    bt = _SAMPLE_BT
    blk = lambda i: (i, 0, 0)
    return pl.pallas_call(
        functools.partial(_attn_sample_kernel, t=t),
        grid=(b // bt,),
        in_specs=[
            pl.BlockSpec(memory_space=pltpu.SMEM),
            pl.BlockSpec((bt, t, ATT_Q), blk),
            pl.BlockSpec((bt, t, ATT_KV), blk),
            pl.BlockSpec((bt, t, ATT_KV), blk),
            pl.BlockSpec((bt, WINDOW, ATT_KV), blk),
            pl.BlockSpec((bt, WINDOW, ATT_KV), blk),
        ],
        out_specs=[
            pl.BlockSpec((bt, t, ATT_Q), blk),
            pl.BlockSpec((bt, WINDOW, ATT_KV), blk),
            pl.BlockSpec((bt, WINDOW, ATT_KV), blk),
        ],
        out_shape=[
            jax.ShapeDtypeStruct((b, t, ATT_Q), F32),
            jax.ShapeDtypeStruct((b, WINDOW, ATT_KV), F32),
            jax.ShapeDtypeStruct((b, WINDOW, ATT_KV), F32),
        ],
        compiler_params=_params(1),
    )(sinks, q, k, v, cache_k, cache_v)


def _rwkv_kernel(z_ref, s0_ref, shift_ref, mu_ref, w0_ref, wa_up_ref, a0_ref, g_up_ref,
                 kk_ref, ka_ref, rk_ref, lng_ref, lnb_ref, o_ref, s_ref, prev_ref, *, c, nseg):
    tseg = c // nseg
    step = pl.program_id(1)
    pairs = range(RW_PAIRS)
    sls = [slice(p * LANES, (p + 1) * LANES) for p in pairs]

    @pl.when(step == 0)
    def _():
        s_ref[...] = s0_ref[...]
        if nseg == 1:
            prev_ref[...] = shift_ref[...]

    z = z_ref[...]
    row1 = lax.broadcasted_iota(jnp.int32, (c, 1), 0)
    if nseg == 1:
        zprev = jnp.where(row1 == 0, prev_ref[...], pltpu.roll(z, 1, axis=0))
        prev_ref[...] = z[c - 1:c, :]
    else:
        zprev = jnp.where(row1 % tseg == 0, shift_ref[...], pltpu.roll(z, 1, axis=0))
    zs = z + (zprev - z) * mu_ref[...]

    w3 = 3 * RW_WIDTH
    r, k, v = zs[:, 0:RW_WIDTH], zs[:, RW_WIDTH:2 * RW_WIDTH], zs[:, 2 * RW_WIDTH:w3]
    xwa = zs[:, w3:w3 + LANES]
    xg = zs[:, w3 + LANES:w3 + 2 * LANES]
    lane = lax.broadcasted_iota(jnp.int32, (c, LANES), 1)
    head0 = lane < RW_HEAD
    lora = _nn(_bf(jnp.where(head0, jnp.tanh(xwa), xwa)), wa_up_ref[...])
    lw = -DECAY_SCALE * _sigmoid(w0_ref[...] + lora[:, 0:RW_WIDTH])
    a_sig = _sigmoid(a0_ref[...] + lora[:, RW_WIDTH:2 * RW_WIDTH])
    g = _nn(_bf(_sigmoid(xg)), g_up_ref[...])
    kk = k * kk_ref[...]
    k = k * (1.0 + (a_sig - 1.0) * ka_ref[...])
    rkr = r * k * rk_ref[...]

    ti = lax.broadcasted_iota(jnp.int32, (c, c), 0)
    tj = lax.broadcasted_iota(jnp.int32, (c, c), 1)
    same_seq = (ti // tseg) == (tj // tseg)
    tri = jnp.where((tj <= ti) & same_seq, 1.0, 0.0).astype(BF16)
    lw3 = _split3(lw)
    cum = _nn(tri, lw3[0]) + _nn(tri, lw3[1]) + _nn(tri, lw3[2])
    if nseg == 1:
        c_end = cum[c - 1:c, :]
    else:
        seq1 = jnp.where(same_seq, 1.0, 0.0).astype(BF16)
        c_end = _nn(seq1, lw3[0]) + _nn(seq1, lw3[1]) + _nn(seq1, lw3[2])
    e_c, e_ci, e_cm = jnp.exp(cum), jnp.exp(-cum), jnp.exp(cum - lw)
    e_ce, w_end = jnp.exp(c_end - cum), jnp.exp(c_end)

    gi = lax.broadcasted_iota(jnp.int32, (2 * LANES, LANES), 0) % LANES
    gj = lax.broadcasted_iota(jnp.int32, (2 * LANES, LANES), 1)
    ones2 = jnp.where((gi // RW_HEAD) == (gj // RW_HEAD), 1.0, 0.0).astype(BF16)
    bi_ = lax.broadcasted_iota(jnp.int32, (LANES, LANES), 0)
    bj_ = lax.broadcasted_iota(jnp.int32, (LANES, LANES), 1)
    same_head = (bi_ // RW_HEAD) == (bj_ // RW_HEAD)
    ci = lax.broadcasted_iota(jnp.int32, (c, 2 * c), 0)
    cj = lax.broadcasted_iota(jnp.int32, (c, 2 * c), 1)
    cjm = cj % c
    seq_ok = (ci // tseg) == (cjm // tseg)
    strict = (cjm < ci) & seq_ok
    incl = (cjm <= ci) & seq_ok
    eye_cat = jnp.where(cjm == ci, 1.0, 0.0)
    left = cj < c

    def seg_sum(x):
        hi, lo = _split2(x)
        return _nn(jnp.concatenate([hi, lo], axis=1), ones2)

    def rows2(x):
        return jnp.concatenate([jnp.where(head0, x, 0.0), jnp.where(head0, 0.0, x)], axis=0)

    def bd(cat):
        return _bf(jnp.concatenate([jnp.where(left, cat, 0.0), jnp.where(left, 0.0, cat)], axis=0))

    def pair_mm(cat, x):
        return _nn(_bf(cat), _bf(rows2(x)))

    ss = [seg_sum(kk[:, s] * kk[:, s]) for s in sls]
    kkn = [kk[:, s] / jnp.maximum(jnp.sqrt(q), 1e-12) for s, q in zip(sls, ss)]
    bv = [n_ * a_sig[:, s] for s, n_ in zip(sls, kkn)]
    rt = [r[:, s] * e_c[:, s] for s in sls]
    kt = [k[:, s] * e_ci[:, s] for s in sls]
    at = [-n_ * e_cm[:, s] for s, n_ in zip(sls, kkn)]
    bt = [b_ * e_ci[:, s] for s, b_ in zip(sls, bv)]
    bh = [b_ * e_ce[:, s] for s, b_ in zip(sls, bv)]
    kh = [k[:, s] * e_ce[:, s] for s in sls]
    vv = [v[:, s] for s in sls]

    ar = [_bf(jnp.concatenate([a_, r_], axis=0)) for a_, r_ in zip(at, rt)]
    xb = [_nt(x, _bf(rows2(b_))) for x, b_ in zip(ar, bt)]
    xk = [_nt(x, _bf(rows2(k_))) for x, k_ in zip(ar, kt)]
    l_ab = [jnp.where(strict, x[0:c], 0.0) for x in xb]
    l_ak = [jnp.where(strict, x[0:c], 0.0) for x in xk]
    m_rb = [jnp.where(incl, x[c:2 * c], 0.0) for x in xb]
    m_rk = [jnp.where(incl, x[c:2 * c], 0.0) for x in xk]

    t_inv = [l + eye_cat for l in l_ab]
    pw = l_ab
    for _ in range(int(math.log2(tseg)) - 1):
        pw = [_nn(_bf(x), bd(x)) for x in pw]
        t_inv = [t + _nn(_bf(t), bd(x)) for t, x in zip(t_inv, pw)]

    a_hat = [pair_mm(t, a_) for t, a_ in zip(t_inv, at)]
    lv = [pair_mm(l, v_) for l, v_ in zip(l_ak, vv)]
    u0 = [pair_mm(t, x) for t, x in zip(t_inv, lv)]
    y1 = [pair_mm(m, v_) for m, v_ in zip(m_rk, vv)]

    segs = range(nseg)
    rs = [slice(q * tseg, (q + 1) * tseg) for q in segs]
    s_old = [[s_ref[q, p] for q in segs] for p in pairs]
    pp = [[_nt(_bf(jnp.concatenate([a_hat[p][rs[q]], rt[p][rs[q]]], axis=0)), _bf(s_old[p][q]))
           for q in segs] for p in pairs]
    u = [jnp.concatenate([pp[p][q][0:tseg] for q in segs], axis=0) + u0[p] for p in pairs]
    y0 = [jnp.concatenate([pp[p][q][tseg:2 * tseg] for q in segs], axis=0) for p in pairs]
    y = [y0[p] + pair_mm(m_rb[p], u[p]) + y1[p] for p in pairs]
    for p in pairs:
        for q in segs:
            upd = _tn(_bf(jnp.concatenate([u[p][rs[q]], vv[p][rs[q]]], axis=0)),
                      _bf(jnp.concatenate([bh[p][rs[q]], kh[p][rs[q]]], axis=0)))
            w_q = w_end[q * tseg:q * tseg + 1, sls[p]]
            s_ref[q, p] = s_old[p][q] * w_q + jnp.where(same_head, upd, 0.0)

    mean = [seg_sum(x) * (1.0 / RW_HEAD) for x in y]
    d = [x - m for x, m in zip(y, mean)]
    var = [seg_sum(x * x) * (1.0 / RW_HEAD) for x in d]
    bonus = [seg_sum(rkr[:, s]) * v_ for s, v_ in zip(sls, vv)]
    for p in pairs:
        s = sls[p]
        yn = d[p] * lax.rsqrt(var[p] + LNX_EPS) * lng_ref[:, s] + lnb_ref[:, s]
        o_ref[:, s] = ((yn + bonus[p]) * g[:, s]).astype(o_ref.dtype)


def _rwkv(zr, s0_bd, shift0, lw, c, nseg):
    b, t, _ = zr.shape
    if nseg == 1:
        ngrp, nchunk = b, t // c
        z3 = zr
        shift = shift0.reshape(b, 1, RW_COLS)
        shift_spec = pl.BlockSpec((None, 1, RW_COLS), lambda bi, i: (bi, 0, 0))
    else:
        assert c == nseg * t
        ngrp, nchunk = b // nseg, 1
        z3 = zr.reshape(ngrp, c, RW_COLS)
        shift = jnp.pad(shift0[:, None, :], ((0, 0), (0, t - 1), (0, 0))).reshape(ngrp, c, RW_COLS)
        shift_spec = pl.BlockSpec((None, c, RW_COLS), lambda bi, i: (bi, 0, 0))
    vec = lambda name: lw[name].reshape(1, -1).astype(F32)
    cst = lambda bi, i: (0, 0)
    vspec = lambda wd: pl.BlockSpec((1, wd), cst)
    state_spec = pl.BlockSpec((nseg, RW_PAIRS, LANES, LANES), lambda bi, i: (bi, 0, 0, 0))
    o, s_bd = pl.pallas_call(
        functools.partial(_rwkv_kernel, c=c, nseg=nseg),
        grid=(ngrp, nchunk),
        in_specs=[
            pl.BlockSpec((None, c, RW_COLS), lambda bi, i: (bi, i, 0)),
            state_spec,
            shift_spec,
            vspec(RW_COLS), vspec(RW_WIDTH),
            pl.BlockSpec((LANES, 2 * RW_WIDTH), cst),
            vspec(RW_WIDTH),
            pl.BlockSpec((G_LORA, RW_WIDTH), cst),
            vspec(RW_WIDTH), vspec(RW_WIDTH), vspec(RW_WIDTH), vspec(RW_WIDTH), vspec(RW_WIDTH),
        ],
        out_specs=[
            pl.BlockSpec((None, c, RW_WIDTH), lambda bi, i: (bi, i, 0)),
            state_spec,
        ],
        out_shape=[
            jax.ShapeDtypeStruct((ngrp, nchunk * c, RW_WIDTH), BF16),
            jax.ShapeDtypeStruct((b, RW_PAIRS, LANES, LANES), F32),
        ],
        scratch_shapes=[pltpu.VMEM((1, RW_COLS), F32)],
        compiler_params=_params(2),
    )(z3, s0_bd, shift, vec("rw_mu"), vec("rw_w0"), lw["wa_up"], vec("rw_a0"),
      lw["g_up"], vec("rw_k_k"), vec("rw_k_a"), vec("rw_r_k"), vec("rw_lnx_g"), vec("rw_lnx_b"))
    return o.reshape(b, t, RW_WIDTH), s_bd


def _state_to_bd(s):
    b = s.shape[0]
    s5 = s.reshape(b, RW_PAIRS, 2, RW_HEAD, RW_HEAD)
    eye = jnp.eye(2, dtype=s.dtype)
    bd = s5[:, :, :, :, None, :] * eye[None, None, :, None, :, None]
    return bd.reshape(b, RW_PAIRS, LANES, LANES)


def _state_from_bd(bd):
    b = bd.shape[0]
    s6 = bd.reshape(b, RW_PAIRS, 2, RW_HEAD, 2, RW_HEAD)
    d = jnp.stack([s6[:, :, 0, :, 0, :], s6[:, :, 1, :, 1, :]], axis=2)
    return d.reshape(b, RW_HEADS, RW_HEAD, RW_HEAD)


def _mix_kernel(x_ref, oa_ref, or_ref, zg_ref, wba_ref, wbr_ref, wo_ref, g_ref, wr_ref, br_ref,
                x1_ref, hn_ref, comb_ref):
    ya = _nn(_bf(oa_ref[...]), wba_ref[...])
    yr = _nn(_bf(or_ref[...]), wbr_ref[...])
    merged = _sigmoid(zg_ref[:, 0:D_MODEL]) * ya + _sigmoid(zg_ref[:, D_MODEL:2 * D_MODEL]) * yr
    x1 = x_ref[...] + _nn(_bf(merged), wo_ref[...])
    x1_ref[...] = x1
    ms = jnp.mean(x1 * x1, axis=-1, keepdims=True)
    hn = x1 * lax.rsqrt(ms + RMS_EPS) * g_ref[...]
    hn_ref[...] = _bf(hn)
    logits = _nn(_bf(hn), wr_ref[...]) + br_ref[...]
    lane = lax.broadcasted_iota(jnp.int32, logits.shape, 1).astype(F32)
    work = logits
    top = None
    for _ in range(TOP_K):
        m = jnp.max(work, axis=-1, keepdims=True)
        if top is None:
            top = m
        idx = jnp.min(jnp.where(work == m, lane, float(LANES)), axis=-1, keepdims=True)
        work = jnp.where(lane == idx, -jnp.inf, work)
    e = jnp.where(work != logits, jnp.exp(logits - top), 0.0)
    comb_ref[...] = e / jnp.sum(e, axis=-1, keepdims=True)


def _mix(x2d, oa, orw, zg, lw, tm):
    n = x2d.shape[0]
    row = lambda i: (i, 0)
    cst = lambda i: (0, 0)
    wspec = pl.BlockSpec((D_MODEL, D_MODEL), cst)
    return pl.pallas_call(
        _mix_kernel,
        grid=(n // tm,),
        in_specs=[
            pl.BlockSpec((tm, D_MODEL), row), pl.BlockSpec((tm, D_MODEL), row), pl.BlockSpec((tm, D_MODEL), row),
            pl.BlockSpec((tm, GATE_COLS), row),
            wspec, wspec, wspec,
            pl.BlockSpec((1, D_MODEL), cst),
            pl.BlockSpec((D_MODEL, LANES), cst),
            pl.BlockSpec((1, LANES), cst),
        ],
        out_specs=[pl.BlockSpec((tm, D_MODEL), row), pl.BlockSpec((tm, D_MODEL), row),
                   pl.BlockSpec((tm, LANES), row)],
        out_shape=[jax.ShapeDtypeStruct((n, D_MODEL), F32), jax.ShapeDtypeStruct((n, D_MODEL), BF16),
                   jax.ShapeDtypeStruct((n, LANES), F32)],
        compiler_params=_params(1),
    )(x2d, oa, orw, zg, lw["w_ba"], lw["w_br"], lw["w_o"], lw["norm_ffn_g"], lw["w_r"], lw["b_r"])


_GU_GROUP = 2 * LANES


def _gu_regroup_kernel(w_ref, o_ref):
    src = lax.broadcasted_iota(jnp.int32, (_GU_GROUP, _GU_GROUP), 0)
    dst = lax.broadcasted_iota(jnp.int32, (_GU_GROUP, _GU_GROUP), 1)
    want = jnp.where(dst < LANES, 2 * dst, 2 * (dst - LANES) + 1)
    perm = jnp.where(src == want, 1.0, 0.0).astype(BF16)
    for j in range(0, 2 * D_EXPERT, _GU_GROUP):
        o_ref[:, j:j + _GU_GROUP] = _nn(_bf(w_ref[:, j:j + _GU_GROUP]), perm).astype(o_ref.dtype)


def _gu_regroup(w_gate_up):
    e, d, n = w_gate_up.shape
    tk = 512
    spec = pl.BlockSpec((None, tk, n), lambda i, j: (i, j, 0))
    return pl.pallas_call(
        _gu_regroup_kernel,
        grid=(e, d // tk),
        in_specs=[spec],
        out_specs=spec,
        out_shape=jax.ShapeDtypeStruct((e, d, n), BF16),
        compiler_params=_params(2),
    )(w_gate_up)


def _moe_kernel(hn_ref, comb_ref, x1_ref, wgu_ref, bgu_ref, wd_ref, bd_ref, gf_ref,
                y_ref, acc_ref, act_ref):
    e = pl.program_id(1)

    @pl.when(e == 0)
    def _():
        acc_ref[...] = x1_ref[...]

    h = hn_ref[...]
    for j in range(D_EXPERT // LANES):
        gs = slice(j * _GU_GROUP, (j + 1) * _GU_GROUP)
        gu = _nn(h, wgu_ref[:, gs]) + bgu_ref[:, gs]
        glu = jnp.minimum(gu[:, 0:LANES], SWIGLU_LIMIT)
        lin = jnp.clip(gu[:, LANES:_GU_GROUP], -SWIGLU_LIMIT, SWIGLU_LIMIT)
        act_ref[:, j * LANES:(j + 1) * LANES] = _bf(glu * _sigmoid(SWIGLU_ALPHA * glu) * (lin + 1.0))
    y = _nn(act_ref[...], wd_ref[...]) + bd_ref[...]
    lane = lax.broadcasted_iota(jnp.int32, comb_ref.shape, 1)
    wcol = jnp.sum(jnp.where(lane == e, comb_ref[...], 0.0), axis=-1, keepdims=True)
    acc_ref[...] += wcol * y

    @pl.when(e == N_EXPERTS - 1)
    def _():
        xo = acc_ref[...]
        ms = jnp.mean(xo * xo, axis=-1, keepdims=True)
        y_ref[...] = xo * lax.rsqrt(ms + RMS_EPS) * gf_ref[...]


def _moe(hn, comb, x1, lw, gf, tm):
    n = hn.shape[0]
    row = lambda i, e: (i, 0)
    cst = lambda i, e: (0, 0)
    ex = lambda i, e: (e, 0, 0)
    return pl.pallas_call(
        _moe_kernel,
        grid=(n // tm, N_EXPERTS),
        in_specs=[
            pl.BlockSpec((tm, D_MODEL), row), pl.BlockSpec((tm, LANES), row), pl.BlockSpec((tm, D_MODEL), row),
            pl.BlockSpec((None, D_MODEL, 2 * D_EXPERT), ex), pl.BlockSpec((None, 1, 2 * D_EXPERT), ex),
            pl.BlockSpec((None, D_EXPERT, D_MODEL), ex), pl.BlockSpec((None, 1, D_MODEL), ex),
            pl.BlockSpec((1, D_MODEL), cst),
        ],
        out_specs=pl.BlockSpec((tm, D_MODEL), row),
        out_shape=jax.ShapeDtypeStruct((n, D_MODEL), F32),
        scratch_shapes=[pltpu.VMEM((tm, D_MODEL), F32), pltpu.VMEM((tm, D_EXPERT), BF16)],
        compiler_params=_params(2),
    )(hn, comb, x1, lw["w_gu"], lw["b_gu"], lw["w_down"], lw["b_down"], gf)


def _prep_layer(l, norm_mix_g, w_in, attn_sinks, rw_mu, rw_w0, rw_w_up, rw_a0, rw_a_up, rw_g_up,
                rw_k_k, rw_k_a, rw_r_k, rw_lnx_g, rw_lnx_b, w_branch_attn, w_branch_rwkv, w_out,
                norm_ffn_g, w_router, b_router, w_gate_up, b_gate_up, w_down, b_down):
    zeros = jnp.zeros((W_LORA, RW_WIDTH), F32)
    wa_up = jnp.concatenate([jnp.concatenate([rw_w_up[l], zeros], axis=1),
                             jnp.concatenate([zeros, rw_a_up[l]], axis=1)], axis=0)
    pad = LANES - N_EXPERTS
    return {
        "norm_mix_g": norm_mix_g[l].reshape(1, D_MODEL),
        "w_in": _bf(w_in[l]),
        "attn_sinks": attn_sinks[l].astype(F32),
        "rw_mu": rw_mu[l], "rw_w0": rw_w0[l], "rw_a0": rw_a0[l], "rw_k_k": rw_k_k[l], "rw_k_a": rw_k_a[l],
        "rw_r_k": rw_r_k[l], "rw_lnx_g": rw_lnx_g[l], "rw_lnx_b": rw_lnx_b[l],
        "wa_up": _bf(wa_up), "g_up": _bf(rw_g_up[l]),
        "w_ba": _bf(w_branch_attn[l]), "w_br": _bf(w_branch_rwkv[l]), "w_o": _bf(w_out[l]),
        "norm_ffn_g": norm_ffn_g[l].reshape(1, D_MODEL),
        "w_r": _bf(jnp.pad(w_router[l], ((0, 0), (0, pad)))),
        "b_r": jnp.pad(b_router[l], (0, pad), constant_values=-jnp.inf).reshape(1, LANES),
        "w_gu": _gu_regroup(w_gate_up[l]),
        "b_gu": b_gate_up[l].reshape(N_EXPERTS, D_EXPERT // LANES, LANES, 2).transpose(0, 1, 3, 2)
                .reshape(N_EXPERTS, 1, 2 * D_EXPERT),
        "w_down": _bf(w_down[l]), "b_down": b_down[l][:, None, :],
    }


def _tile(n, pref):
    tm = pref
    while n % tm:
        tm //= 2
    return tm


def _layer(x, tables, cache_k, cache_v, s0, shift0, lw, gf, chunk, nseg):
    b, t, _ = x.shape
    n = b * t
    x2d = x.reshape(n, D_MODEL)
    tm = _tile(n, 256)
    q, k, v, zr, zg = _inproj(x2d, lw["norm_mix_g"], lw["w_in"], tables, tm)
    q3, k3, v3 = q.reshape(b, t, ATT_Q), k.reshape(b, t, ATT_KV), v.reshape(b, t, ATT_KV)
    if cache_k is None:
        o_att, k_win, v_win = _attn_prompt(q3, k3, v3, lw["attn_sinks"])
    else:
        o_att, k_win, v_win = _attn_sample(q3, k3, v3, lw["attn_sinks"],
                                           cache_k.reshape(b, WINDOW, ATT_KV), cache_v.reshape(b, WINDOW, ATT_KV))
    zr3 = zr.reshape(b, t, RW_COLS)
    o_rw, s_bd = _rwkv(zr3, _state_to_bd(s0), shift0, lw, chunk, nseg)
    x1, hn, comb = _mix(x2d, o_att.reshape(n, ATT_Q), o_rw.reshape(n, RW_WIDTH), zg, lw, tm)
    y = _moe(hn, comb, x1, lw, gf, _tile(n, 512))
    kv_shape = (b, WINDOW, N_KV_HEADS, HEAD_DIM)
    return (y.reshape(b, t, D_MODEL), k_win.reshape(kv_shape), v_win.reshape(kv_shape),
            _state_from_bd(s_bd), zr3[:, t - 1, :])


def kernel(x_prompt, x_sample, cache_k, cache_v, state_wkv, state_shift, norm_mix_g, w_in, attn_sinks, rw_mu, rw_w0, rw_w_up, rw_a0, rw_a_up, rw_g_up, rw_k_k, rw_k_a, rw_r_k, rw_lnx_g, rw_lnx_b, w_branch_attn, w_branch_rwkv, w_out, norm_ffn_g, w_router, b_router, w_gate_up, b_gate_up, w_down, b_down, norm_final_g):
    assert w_in.shape[0] == 1, "single-layer trunk"
    bp, tp, _ = x_prompt.shape
    bs, ts, _ = x_sample.shape
    lw = _prep_layer(0, norm_mix_g, w_in, attn_sinks, rw_mu, rw_w0, rw_w_up, rw_a0, rw_a_up, rw_g_up,
                     rw_k_k, rw_k_a, rw_r_k, rw_lnx_g, rw_lnx_b, w_branch_attn, w_branch_rwkv, w_out,
                     norm_ffn_g, w_router, b_router, w_gate_up, b_gate_up, w_down, b_down)
    gf = norm_final_g.reshape(1, D_MODEL)

    tm_p = _tile(bp * tp, 256)
    tab_p = _rope_tables(jnp.arange(tp, dtype=jnp.int32))
    tm_s = _tile(bs * ts, 256)
    pos_s = PAST_LEN + jnp.arange(ts, dtype=jnp.int32)
    tab_s = tuple(jnp.tile(u, (tm_s // ts, 1)) for u in _rope_tables(pos_s))
    del tm_p

    s0p = jnp.zeros((bp, RW_HEADS, RW_HEAD, RW_HEAD), state_wkv.dtype)
    sh0p = jnp.zeros((bp, RW_COLS), state_shift.dtype)
    yp, kp, vp, sp, shp = _layer(x_prompt, tab_p, None, None, s0p, sh0p, lw, gf, RW_CHUNK, 1)
    ys, ks, vs, ss, shs = _layer(x_sample, tab_s, cache_k[0], cache_v[0], state_wkv[0], state_shift[0],
                                 lw, gf, RW_CHUNK, RW_CHUNK // ts)
    ex = lambda u: u[None]
    return (yp, ys, ex(kp), ex(vp), ex(sp), ex(shp), ex(ks), ex(vs), ex(ss), ex(shs))
```

```python
import functools
import math

import jax
import jax.numpy as jnp
from jax import lax
from jax.experimental import pallas as pl
from jax.experimental.pallas import tpu as pltpu

F32 = jnp.float32
BF16 = jnp.bfloat16

LANES = 128
SUBLANES = 8
VMEM_LIMIT_BYTES = 56 * 1024 * 1024

D_MODEL = 1024
HEAD_DIM = 64
N_Q_HEADS = 16
N_KV_HEADS = 4
Q_PER_KV = 4
WINDOW = 128
ROPE_THETA = 500000.0
ROPE_DIM = 16
ROPE_HALF = 8
ATTN_SCALE = HEAD_DIM ** -0.5
PAST_LEN = 16384
RW_HEAD = 64
RW_HEADS = 16
RW_PAIRS = RW_HEADS // 2
W_LORA = 64
A_LORA = 64
G_LORA = 128
LNX_EPS = 64e-5
N_EXPERTS = 32
TOP_K = 4
D_EXPERT = 1024
SWIGLU_LIMIT = 7.0
SWIGLU_ALPHA = 1.702
RMS_EPS = 1e-5
ATT_Q = N_Q_HEADS * HEAD_DIM
ATT_KV = N_KV_HEADS * HEAD_DIM
RW_WIDTH = RW_HEADS * RW_HEAD
RW_COLS = 3 * RW_WIDTH + W_LORA + A_LORA + G_LORA
GATE_COLS = 2 * D_MODEL
IN_COLS = ATT_Q + 2 * ATT_KV + RW_COLS + GATE_COLS
DECAY_SCALE = math.exp(-0.5)
RW_CHUNK = 64
MOE_BLOCK = 2048


def _nn(a, b):
    return jnp.dot(a, b, preferred_element_type=F32)


def _nt(a, b):
    return lax.dot_general(a, b, (((1,), (1,)), ((), ())), preferred_element_type=F32)


def _tn(a, b):
    return lax.dot_general(a, b, (((0,), (0,)), ((), ())), preferred_element_type=F32)


def _bf(x):
    return x.astype(BF16)


def _sigmoid(x):
    return 1.0 / (1.0 + jnp.exp(-x))


def _split2(x):
    hi = x.astype(BF16)
    lo = (x - hi.astype(F32)).astype(BF16)
    return hi, lo


def _split3(x):
    hi = x.astype(BF16)
    r1 = x - hi.astype(F32)
    mid = r1.astype(BF16)
    lo = (r1 - mid.astype(F32)).astype(BF16)
    return hi, mid, lo


def _params(n_axes):
    return pltpu.CompilerParams(
        dimension_semantics=("arbitrary",) * n_axes, vmem_limit_bytes=VMEM_LIMIT_BYTES)


_IN_CHUNK = 512


def _inproj_kernel(x_ref, g_ref, w_ref, cos_ref, sa_ref, sb_ref,
                   q_ref, k_ref, v_ref, zr_ref, zg_ref):
    x = x_ref[...]
    ms = jnp.mean(x * x, axis=-1, keepdims=True)
    h = _bf(x * lax.rsqrt(ms + RMS_EPS) * g_ref[...])
    cos, sa, sb = cos_ref[...], sa_ref[...], sb_ref[...]

    def rope(y):
        return (y * cos + pltpu.roll(y, LANES - ROPE_HALF, axis=1) * sa
                + pltpu.roll(y, ROPE_HALF, axis=1) * sb)

    def project(out_ref, col0, width, with_rope):
        for c in range(0, width, _IN_CHUNK):
            cw = min(_IN_CHUNK, width - c)
            acc = _nn(h, w_ref[:, col0 + c:col0 + c + cw])
            if with_rope:
                for j in range(0, cw, LANES):
                    out_ref[:, c + j:c + j + LANES] = rope(acc[:, j:j + LANES])
            else:
                out_ref[:, c:c + cw] = acc

    project(q_ref, 0, ATT_Q, True)
    project(k_ref, ATT_Q, ATT_KV, True)
    project(v_ref, ATT_Q + ATT_KV, ATT_KV, False)
    project(zr_ref, ATT_Q + 2 * ATT_KV, RW_COLS, False)
    project(zg_ref, ATT_Q + 2 * ATT_KV + RW_COLS, GATE_COLS, False)


def _rope_tables(positions):
    inv_freq = ROPE_THETA ** (-jnp.arange(ROPE_HALF, dtype=F32) / ROPE_HALF)
    ang = positions.astype(F32)[:, None] * inv_freq[None, :]
    cos, sin = jnp.cos(ang), jnp.sin(ang)
    t = positions.shape[0]
    one = jnp.ones((t, HEAD_DIM - ROPE_DIM), F32)
    zero = jnp.zeros((t, HEAD_DIM - ROPE_DIM), F32)
    z8 = jnp.zeros((t, ROPE_HALF), F32)
    cos_h = jnp.concatenate([cos, cos, one], axis=1)
    sa_h = jnp.concatenate([-sin, z8, zero], axis=1)
    sb_h = jnp.concatenate([z8, sin, zero], axis=1)
    two = lambda u: jnp.concatenate([u, u], axis=1)
    return two(cos_h), two(sa_h), two(sb_h)


def _inproj(x2d, g, w_bf, tables, tm):
    n = x2d.shape[0]
    cos, sa, sb = tables
    nper = cos.shape[0] // tm
    row = lambda i: (i, 0)
    tab = lambda i: (i % nper, 0)
    const = lambda i: (0, 0)
    out_shapes = [jax.ShapeDtypeStruct((n, w), F32) for w in (ATT_Q, ATT_KV, ATT_KV, RW_COLS, GATE_COLS)]
    return pl.pallas_call(
        _inproj_kernel,
        grid=(n // tm,),
        in_specs=[
            pl.BlockSpec((tm, D_MODEL), row),
            pl.BlockSpec((1, D_MODEL), const),
            pl.BlockSpec((D_MODEL, IN_COLS), const, pipeline_mode=pl.Buffered(1)),
            pl.BlockSpec((tm, LANES), tab),
            pl.BlockSpec((tm, LANES), tab),
            pl.BlockSpec((tm, LANES), tab),
        ],
        out_specs=[pl.BlockSpec((tm, w), row) for w in (ATT_Q, ATT_KV, ATT_KV, RW_COLS, GATE_COLS)],
        out_shape=out_shapes,
        compiler_params=_params(1),
    )(x2d, g, w_bf, cos, sa, sb)


def _attn_prompt_kernel(sink_ref, q_ref, kp_ref, kc_ref, vp_ref, vc_ref, o_ref, kw_ref, vw_ref):
    n = pl.program_id(1)
    w = WINDOW
    lane = lax.broadcasted_iota(jnp.int32, (w, LANES), 1)
    first_half = lane < HEAD_DIM
    qi = lax.broadcasted_iota(jnp.int32, (Q_PER_KV * w, 2 * w), 0) % w
    kj = lax.broadcasted_iota(jnp.int32, (Q_PER_KV * w, 2 * w), 1)
    first_key = jnp.where(n > 0, 0, w)
    band4 = (kj > qi) & (kj <= qi + w) & (kj >= first_key)

    kw_ref[...] = kc_ref[...]
    vw_ref[...] = vc_ref[...]

    for grp in range(ATT_KV // LANES):
        gs = slice(grp * LANES, (grp + 1) * LANES)
        k2 = _bf(jnp.concatenate([kp_ref[:, gs], kc_ref[:, gs]], axis=0))
        v2 = _bf(jnp.concatenate([vp_ref[:, gs], vc_ref[:, gs]], axis=0))
        for jh in range(2):
            kv = grp * 2 + jh
            keep = first_half if jh == 0 else jnp.logical_not(first_half)
            rows, sinks = [], []
            for gq in range(Q_PER_KV):
                hq = kv * Q_PER_KV + gq
                qg = q_ref[:, (hq // 2) * LANES:(hq // 2 + 1) * LANES]
                if hq % 2 != jh:
                    qg = pltpu.roll(qg, HEAD_DIM, axis=1)
                rows.append(jnp.where(keep, qg, 0.0))
                sinks.append(jnp.full((w, 1), sink_ref[hq], F32))
            qs = _bf(jnp.concatenate(rows, axis=0))
            sink = jnp.concatenate(sinks, axis=0)
            s = _nt(qs, k2) * ATTN_SCALE
            s = jnp.where(band4, s, -jnp.inf)
            m = jnp.maximum(jnp.max(s, axis=-1, keepdims=True), sink)
            e = jnp.exp(s - m)
            denom = jnp.sum(e, axis=-1, keepdims=True) + jnp.exp(sink - m)
            pv = _nn(_bf(e / denom), v2)
            for go in range(Q_PER_KV // 2):
                pa, pb = pv[2 * go * w:(2 * go + 1) * w], pv[(2 * go + 1) * w:(2 * go + 2) * w]
                if jh == 0:
                    pb = pltpu.roll(pb, HEAD_DIM, axis=1)
                else:
                    pa = pltpu.roll(pa, HEAD_DIM, axis=1)
                og = kv * (Q_PER_KV // 2) + go
                o_ref[:, og * LANES:(og + 1) * LANES] = jnp.where(first_half, pa, pb).astype(o_ref.dtype)


def _attn_prompt(q, k, v, sinks):
    b, t, _ = q.shape
    nb = t // WINDOW
    cur = lambda bi, n: (bi, n, 0)
    prev = lambda bi, n: (bi, jnp.maximum(n - 1, 0), 0)
    win = lambda bi, n: (bi, 0, 0)
    return pl.pallas_call(
        _attn_prompt_kernel,
        grid=(b, nb),
        in_specs=[
            pl.BlockSpec(memory_space=pltpu.SMEM),
            pl.BlockSpec((None, WINDOW, ATT_Q), cur),
            pl.BlockSpec((None, WINDOW, ATT_KV), prev),
            pl.BlockSpec((None, WINDOW, ATT_KV), cur),
            pl.BlockSpec((None, WINDOW, ATT_KV), prev),
            pl.BlockSpec((None, WINDOW, ATT_KV), cur),
        ],
        out_specs=[
            pl.BlockSpec((None, WINDOW, ATT_Q), cur),
            pl.BlockSpec((None, WINDOW, ATT_KV), win),
            pl.BlockSpec((None, WINDOW, ATT_KV), win),
        ],
        out_shape=[
            jax.ShapeDtypeStruct((b, t, ATT_Q), BF16),
            jax.ShapeDtypeStruct((b, WINDOW, ATT_KV), F32),
            jax.ShapeDtypeStruct((b, WINDOW, ATT_KV), F32),
        ],
        compiler_params=_params(2),
    )(sinks, q, k, k, v, v)


_SAMPLE_BT = 8


def _attn_sample_kernel(sink_ref, q_ref, k_ref, v_ref, ck_ref, cv_ref, o_ref, nk_ref, nv_ref, *, t):
    w = WINDOW
    rows_per_grp = 2 * Q_PER_KV * t
    lane = lax.broadcasted_iota(jnp.int32, (t, LANES), 1)
    first_half = lane < HEAD_DIM
    r_c = lax.broadcasted_iota(jnp.int32, (rows_per_grp, w), 0) % t
    c_c = lax.broadcasted_iota(jnp.int32, (rows_per_grp, w), 1)
    mask_c = c_c > r_c
    r_n = lax.broadcasted_iota(jnp.int32, (rows_per_grp, t), 0) % t
    c_n = lax.broadcasted_iota(jnp.int32, (rows_per_grp, t), 1)
    mask_n = c_n <= r_n

    for bi in range(_SAMPLE_BT):
        nk_ref[bi, 0:w - t, :] = ck_ref[bi, t:w, :]
        nk_ref[bi, w - t:w, :] = k_ref[bi]
        nv_ref[bi, 0:w - t, :] = cv_ref[bi, t:w, :]
        nv_ref[bi, w - t:w, :] = v_ref[bi]
        for grp in range(ATT_KV // LANES):
            gs = slice(grp * LANES, (grp + 1) * LANES)
            kc, vc = _bf(ck_ref[bi, :, gs]), _bf(cv_ref[bi, :, gs])
            kn, vn = _bf(k_ref[bi, :, gs]), _bf(v_ref[bi, :, gs])
            rows, sinks = [], []
            for jh in range(2):
                kv = grp * 2 + jh
                keep = first_half if jh == 0 else jnp.logical_not(first_half)
                for gq in range(Q_PER_KV):
                    hq = kv * Q_PER_KV + gq
                    qg = q_ref[bi, :, (hq // 2) * LANES:(hq // 2 + 1) * LANES]
                    if hq % 2 != jh:
                        qg = pltpu.roll(qg, HEAD_DIM, axis=1)
                    rows.append(jnp.where(keep, qg, 0.0))
                    sinks.append(jnp.full((t, 1), sink_ref[hq], F32))
            qs = _bf(jnp.concatenate(rows, axis=0))
            sink = jnp.concatenate(sinks, axis=0)
            s_c = jnp.where(mask_c, _nt(qs, kc) * ATTN_SCALE, -jnp.inf)
            s_n = jnp.where(mask_n, _nt(qs, kn) * ATTN_SCALE, -jnp.inf)
            m = jnp.maximum(jnp.maximum(jnp.max(s_c, axis=-1, keepdims=True),
                                        jnp.max(s_n, axis=-1, keepdims=True)), sink)
            e_c, e_n = jnp.exp(s_c - m), jnp.exp(s_n - m)
            denom = (jnp.sum(e_c, axis=-1, keepdims=True) + jnp.sum(e_n, axis=-1, keepdims=True)
                     + jnp.exp(sink - m))
            pv = _nn(_bf(e_c / denom), vc) + _nn(_bf(e_n / denom), vn)
            for jh in range(2):
                for go in range(Q_PER_KV // 2):
                    r0 = (jh * Q_PER_KV + 2 * go) * t
                    pa, pb = pv[r0:r0 + t], pv[r0 + t:r0 + 2 * t]
                    if jh == 0:
                        pb = pltpu.roll(pb, HEAD_DIM, axis=1)
                    else:
                        pa = pltpu.roll(pa, HEAD_DIM, axis=1)
                    og = (grp * 2 + jh) * (Q_PER_KV // 2) + go
                    o_ref[bi, :, og * LANES:(og + 1) * LANES] = jnp.where(first_half, pa, pb)


def _attn_sample(q, k, v, sinks, cache_k, cache_v):
    b, t, _ = q.shape
    bt = _SAMPLE_BT
    blk = lambda i: (i, 0, 0)
    return pl.pallas_call(
        functools.partial(_attn_sample_kernel, t=t),
        grid=(b // bt,),
        in_specs=[
            pl.BlockSpec(memory_space=pltpu.SMEM),
            pl.BlockSpec((bt, t, ATT_Q), blk),
            pl.BlockSpec((bt, t, ATT_KV), blk),
            pl.BlockSpec((bt, t, ATT_KV), blk),
            pl.BlockSpec((bt, WINDOW, ATT_KV), blk),
            pl.BlockSpec((bt, WINDOW, ATT_KV), blk),
        ],
        out_specs=[
            pl.BlockSpec((bt, t, ATT_Q), blk),
            pl.BlockSpec((bt, WINDOW, ATT_KV), blk),
            pl.BlockSpec((bt, WINDOW, ATT_KV), blk),
        ],
        out_shape=[
            jax.ShapeDtypeStruct((b, t, ATT_Q), F32),
            jax.ShapeDtypeStruct((b, WINDOW, ATT_KV), F32),
            jax.ShapeDtypeStruct((b, WINDOW, ATT_KV), F32),
        ],
        compiler_params=_params(1),
    )(sinks, q, k, v, cache_k, cache_v)


def _rwkv_kernel(z_ref, s0_ref, shift_ref, mu_ref, w0_ref, wa_up_ref, a0_ref, g_up_ref,
                 kk_ref, ka_ref, rk_ref, lng_ref, lnb_ref, o_ref, s_ref, prev_ref, *, c, nseg):
    tseg = c // nseg
    step = pl.program_id(1)
    pairs = range(RW_PAIRS)
    sls = [slice(p * LANES, (p + 1) * LANES) for p in pairs]

    @pl.when(step == 0)
    def _():
        s_ref[...] = s0_ref[...]
        if nseg == 1:
            prev_ref[...] = shift_ref[...]

    z = z_ref[...]
    row1 = lax.broadcasted_iota(jnp.int32, (c, 1), 0)
    if nseg == 1:
        zprev = jnp.where(row1 == 0, prev_ref[...], pltpu.roll(z, 1, axis=0))
        prev_ref[...] = z[c - 1:c, :]
    else:
        zprev = jnp.where(row1 % tseg == 0, shift_ref[...], pltpu.roll(z, 1, axis=0))
    zs = z + (zprev - z) * mu_ref[...]

    w3 = 3 * RW_WIDTH
    r, k, v = zs[:, 0:RW_WIDTH], zs[:, RW_WIDTH:2 * RW_WIDTH], zs[:, 2 * RW_WIDTH:w3]
    xwa = zs[:, w3:w3 + LANES]
    xg = zs[:, w3 + LANES:w3 + 2 * LANES]
    lane = lax.broadcasted_iota(jnp.int32, (c, LANES), 1)
    head0 = lane < RW_HEAD
    lora = _nn(_bf(jnp.where(head0, jnp.tanh(xwa), xwa)), wa_up_ref[...])
    lw = -DECAY_SCALE * _sigmoid(w0_ref[...] + lora[:, 0:RW_WIDTH])
    a_sig = _sigmoid(a0_ref[...] + lora[:, RW_WIDTH:2 * RW_WIDTH])
    g = _nn(_bf(_sigmoid(xg)), g_up_ref[...])
    kk = k * kk_ref[...]
    k = k * (1.0 + (a_sig - 1.0) * ka_ref[...])
    rkr = r * k * rk_ref[...]

    ti = lax.broadcasted_iota(jnp.int32, (c, c), 0)
    tj = lax.broadcasted_iota(jnp.int32, (c, c), 1)
    same_seq = (ti // tseg) == (tj // tseg)
    tri = jnp.where((tj <= ti) & same_seq, 1.0, 0.0).astype(BF16)
    lw3 = _split3(lw)
    cum = _nn(tri, lw3[0]) + _nn(tri, lw3[1]) + _nn(tri, lw3[2])
    if nseg == 1:
        c_end = cum[c - 1:c, :]
    else:
        seq1 = jnp.where(same_seq, 1.0, 0.0).astype(BF16)
        c_end = _nn(seq1, lw3[0]) + _nn(seq1, lw3[1]) + _nn(seq1, lw3[2])
    e_c, e_ci, e_cm = jnp.exp(cum), jnp.exp(-cum), jnp.exp(cum - lw)
    e_ce, w_end = jnp.exp(c_end - cum), jnp.exp(c_end)

    gi = lax.broadcasted_iota(jnp.int32, (2 * LANES, LANES), 0) % LANES
    gj = lax.broadcasted_iota(jnp.int32, (2 * LANES, LANES), 1)
    ones2 = jnp.where((gi // RW_HEAD) == (gj // RW_HEAD), 1.0, 0.0).astype(BF16)
    bi_ = lax.broadcasted_iota(jnp.int32, (LANES, LANES), 0)
    bj_ = lax.broadcasted_iota(jnp.int32, (LANES, LANES), 1)
    same_head = (bi_ // RW_HEAD) == (bj_ // RW_HEAD)
    ci = lax.broadcasted_iota(jnp.int32, (c, 2 * c), 0)
    cj = lax.broadcasted_iota(jnp.int32, (c, 2 * c), 1)
    cjm = cj % c
    seq_ok = (ci // tseg) == (cjm // tseg)
    strict = (cjm < ci) & seq_ok
    incl = (cjm <= ci) & seq_ok
    eye_cat = jnp.where(cjm == ci, 1.0, 0.0)
    left = cj < c

    def seg_sum(x):
        hi, lo = _split2(x)
        return _nn(jnp.concatenate([hi, lo], axis=1), ones2)

    def rows2(x):
        return jnp.concatenate([jnp.where(head0, x, 0.0), jnp.where(head0, 0.0, x)], axis=0)

    def bd(cat):
        return _bf(jnp.concatenate([jnp.where(left, cat, 0.0), jnp.where(left, 0.0, cat)], axis=0))

    def pair_mm(cat, x):
        return _nn(_bf(cat), _bf(rows2(x)))

    ss = [seg_sum(kk[:, s] * kk[:, s]) for s in sls]
    kkn = [kk[:, s] / jnp.maximum(jnp.sqrt(q), 1e-12) for s, q in zip(sls, ss)]
    bv = [n_ * a_sig[:, s] for s, n_ in zip(sls, kkn)]
    rt = [r[:, s] * e_c[:, s] for s in sls]
    kt = [k[:, s] * e_ci[:, s] for s in sls]
    at = [-n_ * e_cm[:, s] for s, n_ in zip(sls, kkn)]
    bt = [b_ * e_ci[:, s] for s, b_ in zip(sls, bv)]
    bh = [b_ * e_ce[:, s] for s, b_ in zip(sls, bv)]
    kh = [k[:, s] * e_ce[:, s] for s in sls]
    vv = [v[:, s] for s in sls]

    ar = [_bf(jnp.concatenate([a_, r_], axis=0)) for a_, r_ in zip(at, rt)]
    xb = [_nt(x, _bf(rows2(b_))) for x, b_ in zip(ar, bt)]
    xk = [_nt(x, _bf(rows2(k_))) for x, k_ in zip(ar, kt)]
    l_ab = [jnp.where(strict, x[0:c], 0.0) for x in xb]
    l_ak = [jnp.where(strict, x[0:c], 0.0) for x in xk]
    m_rb = [jnp.where(incl, x[c:2 * c], 0.0) for x in xb]
    m_rk = [jnp.where(incl, x[c:2 * c], 0.0) for x in xk]

    t_inv = [l + eye_cat for l in l_ab]
    pw = l_ab
    for _ in range(int(math.log2(tseg)) - 1):
        pw = [_nn(_bf(x), bd(x)) for x in pw]
        t_inv = [t + _nn(_bf(t), bd(x)) for t, x in zip(t_inv, pw)]

    a_hat = [pair_mm(t, a_) for t, a_ in zip(t_inv, at)]
    lv = [pair_mm(l, v_) for l, v_ in zip(l_ak, vv)]
    u0 = [pair_mm(t, x) for t, x in zip(t_inv, lv)]
    y1 = [pair_mm(m, v_) for m, v_ in zip(m_rk, vv)]

    segs = range(nseg)
    rs = [slice(q * tseg, (q + 1) * tseg) for q in segs]
    s_old = [[s_ref[q, p] for q in segs] for p in pairs]
    pp = [[_nt(_bf(jnp.concatenate([a_hat[p][rs[q]], rt[p][rs[q]]], axis=0)), _bf(s_old[p][q]))
           for q in segs] for p in pairs]
    u = [jnp.concatenate([pp[p][q][0:tseg] for q in segs], axis=0) + u0[p] for p in pairs]
    y0 = [jnp.concatenate([pp[p][q][tseg:2 * tseg] for q in segs], axis=0) for p in pairs]
    y = [y0[p] + pair_mm(m_rb[p], u[p]) + y1[p] for p in pairs]
    for p in pairs:
        for q in segs:
            upd = _tn(_bf(jnp.concatenate([u[p][rs[q]], vv[p][rs[q]]], axis=0)),
                      _bf(jnp.concatenate([bh[p][rs[q]], kh[p][rs[q]]], axis=0)))
            w_q = w_end[q * tseg:q * tseg + 1, sls[p]]
            s_ref[q, p] = s_old[p][q] * w_q + jnp.where(same_head, upd, 0.0)

    mean = [seg_sum(x) * (1.0 / RW_HEAD) for x in y]
    d = [x - m for x, m in zip(y, mean)]
    var = [seg_sum(x * x) * (1.0 / RW_HEAD) for x in d]
    bonus = [seg_sum(rkr[:, s]) * v_ for s, v_ in zip(sls, vv)]
    for p in pairs:
        s = sls[p]
        yn = d[p] * lax.rsqrt(var[p] + LNX_EPS) * lng_ref[:, s] + lnb_ref[:, s]
        o_ref[:, s] = ((yn + bonus[p]) * g[:, s]).astype(o_ref.dtype)


def _rwkv(zr, s0_bd, shift0, lw, c, nseg):
    b, t, _ = zr.shape
    if nseg == 1:
        ngrp, nchunk = b, t // c
        z3 = zr
        shift = shift0.reshape(b, 1, RW_COLS)
        shift_spec = pl.BlockSpec((None, 1, RW_COLS), lambda bi, i: (bi, 0, 0))
    else:
        assert c == nseg * t
        ngrp, nchunk = b // nseg, 1
        z3 = zr.reshape(ngrp, c, RW_COLS)
        shift = jnp.pad(shift0[:, None, :], ((0, 0), (0, t - 1), (0, 0))).reshape(ngrp, c, RW_COLS)
        shift_spec = pl.BlockSpec((None, c, RW_COLS), lambda bi, i: (bi, 0, 0))
    vec = lambda name: lw[name].reshape(1, -1).astype(F32)
    cst = lambda bi, i: (0, 0)
    vspec = lambda wd: pl.BlockSpec((1, wd), cst)
    state_spec = pl.BlockSpec((nseg, RW_PAIRS, LANES, LANES), lambda bi, i: (bi, 0, 0, 0))
    o, s_bd = pl.pallas_call(
        functools.partial(_rwkv_kernel, c=c, nseg=nseg),
        grid=(ngrp, nchunk),
        in_specs=[
            pl.BlockSpec((None, c, RW_COLS), lambda bi, i: (bi, i, 0)),
            state_spec,
            shift_spec,
            vspec(RW_COLS), vspec(RW_WIDTH),
            pl.BlockSpec((LANES, 2 * RW_WIDTH), cst),
            vspec(RW_WIDTH),
            pl.BlockSpec((G_LORA, RW_WIDTH), cst),
            vspec(RW_WIDTH), vspec(RW_WIDTH), vspec(RW_WIDTH), vspec(RW_WIDTH), vspec(RW_WIDTH),
        ],
        out_specs=[
            pl.BlockSpec((None, c, RW_WIDTH), lambda bi, i: (bi, i, 0)),
            state_spec,
        ],
        out_shape=[
            jax.ShapeDtypeStruct((ngrp, nchunk * c, RW_WIDTH), BF16),
            jax.ShapeDtypeStruct((b, RW_PAIRS, LANES, LANES), F32),
        ],
        scratch_shapes=[pltpu.VMEM((1, RW_COLS), F32)],
        compiler_params=_params(2),
    )(z3, s0_bd, shift, vec("rw_mu"), vec("rw_w0"), lw["wa_up"], vec("rw_a0"),
      lw["g_up"], vec("rw_k_k"), vec("rw_k_a"), vec("rw_r_k"), vec("rw_lnx_g"), vec("rw_lnx_b"))
    return o.reshape(b, t, RW_WIDTH), s_bd


def _state_to_bd(s):
    b = s.shape[0]
    s5 = s.reshape(b, RW_PAIRS, 2, RW_HEAD, RW_HEAD)
    eye = jnp.eye(2, dtype=s.dtype)
    bd = s5[:, :, :, :, None, :] * eye[None, None, :, None, :, None]
    return bd.reshape(b, RW_PAIRS, LANES, LANES)


def _state_from_bd(bd):
    b = bd.shape[0]
    s6 = bd.reshape(b, RW_PAIRS, 2, RW_HEAD, 2, RW_HEAD)
    d = jnp.stack([s6[:, :, 0, :, 0, :], s6[:, :, 1, :, 1, :]], axis=2)
    return d.reshape(b, RW_HEADS, RW_HEAD, RW_HEAD)


def _mix_kernel(x_ref, oa_ref, or_ref, zg_ref, wba_ref, wbr_ref, wo_ref, g_ref, wr_ref, br_ref,
                x1_ref, hn_ref, comb_ref, cnt_ref):
    ya = _nn(_bf(oa_ref[...]), wba_ref[...])
    yr = _nn(_bf(or_ref[...]), wbr_ref[...])
    merged = _sigmoid(zg_ref[:, 0:D_MODEL]) * ya + _sigmoid(zg_ref[:, D_MODEL:2 * D_MODEL]) * yr
    x1 = x_ref[...] + _nn(_bf(merged), wo_ref[...])
    x1_ref[...] = x1
    ms = jnp.mean(x1 * x1, axis=-1, keepdims=True)
    hn = x1 * lax.rsqrt(ms + RMS_EPS) * g_ref[...]
    hn_ref[...] = _bf(hn)
    logits = _nn(_bf(hn), wr_ref[...]) + br_ref[...]
    lane = lax.broadcasted_iota(jnp.int32, logits.shape, 1).astype(F32)
    work = logits
    top = None
    for _ in range(TOP_K):
        m = jnp.max(work, axis=-1, keepdims=True)
        if top is None:
            top = m
        idx = jnp.min(jnp.where(work == m, lane, float(LANES)), axis=-1, keepdims=True)
        work = jnp.where(lane == idx, -jnp.inf, work)
    e = jnp.where(work != logits, jnp.exp(logits - top), 0.0)
    comb = e / jnp.sum(e, axis=-1, keepdims=True)
    comb_ref[...] = comb
    cnt_ref[...] = jnp.sum(jnp.where(comb > 0.0, 1.0, 0.0), axis=0, keepdims=True)


def _mix(x2d, oa, orw, zg, lw, tm):
    n = x2d.shape[0]
    row = lambda i: (i, 0)
    cst = lambda i: (0, 0)
    wspec = pl.BlockSpec((D_MODEL, D_MODEL), cst)
    return pl.pallas_call(
        _mix_kernel,
        grid=(n // tm,),
        in_specs=[
            pl.BlockSpec((tm, D_MODEL), row), pl.BlockSpec((tm, D_MODEL), row), pl.BlockSpec((tm, D_MODEL), row),
            pl.BlockSpec((tm, GATE_COLS), row),
            wspec, wspec, wspec,
            pl.BlockSpec((1, D_MODEL), cst),
            pl.BlockSpec((D_MODEL, LANES), cst),
            pl.BlockSpec((1, LANES), cst),
        ],
        out_specs=[pl.BlockSpec((tm, D_MODEL), row), pl.BlockSpec((tm, D_MODEL), row),
                   pl.BlockSpec((tm, LANES), row), pl.BlockSpec((None, 1, LANES), lambda i: (i, 0, 0))],
        out_shape=[jax.ShapeDtypeStruct((n, D_MODEL), F32), jax.ShapeDtypeStruct((n, D_MODEL), BF16),
                   jax.ShapeDtypeStruct((n, LANES), F32), jax.ShapeDtypeStruct((n // tm, 1, LANES), F32)],
        compiler_params=_params(1),
    )(x2d, oa, orw, zg, lw["w_ba"], lw["w_br"], lw["w_o"], lw["norm_ffn_g"], lw["w_r"], lw["b_r"])


_GU_GROUP = 2 * LANES


def _gu_regroup_kernel(w_ref, o_ref):
    src = lax.broadcasted_iota(jnp.int32, (_GU_GROUP, _GU_GROUP), 0)
    dst = lax.broadcasted_iota(jnp.int32, (_GU_GROUP, _GU_GROUP), 1)
    want = jnp.where(dst < LANES, 2 * dst, 2 * (dst - LANES) + 1)
    perm = jnp.where(src == want, 1.0, 0.0).astype(BF16)
    for j in range(0, 2 * D_EXPERT, _GU_GROUP):
        o_ref[:, j:j + _GU_GROUP] = _nn(_bf(w_ref[:, j:j + _GU_GROUP]), perm).astype(o_ref.dtype)


def _gu_regroup(w_gate_up):
    e, d, n = w_gate_up.shape
    tk = 512
    spec = pl.BlockSpec((None, tk, n), lambda i, j: (i, j, 0))
    return pl.pallas_call(
        _gu_regroup_kernel,
        grid=(e, d // tk),
        in_specs=[spec],
        out_specs=spec,
        out_shape=jax.ShapeDtypeStruct((e, d, n), BF16),
        compiler_params=_params(2),
    )(w_gate_up)


_MOE_DOMAIN = 1024
_MOE_ROWS = 160
_RANK_CHUNK = 256


def _moe_kernel(nsub_ref, hn_ref, comb_ref, wgu_ref, bgu_ref, wd_ref, bd_ref, out_ref,
                rank_ref, rank_t_ref, act_ref):
    blk, e = pl.program_id(0), pl.program_id(1)
    tb = hn_ref.shape[0]
    dom, rows = min(_MOE_DOMAIN, tb), _MOE_ROWS
    ndom = tb // dom

    @pl.when(e == 0)
    def _():
        out_ref[...] = jnp.zeros_like(out_ref)
        ci = lax.broadcasted_iota(jnp.int32, (_RANK_CHUNK, _RANK_CHUNK), 0)
        cj = lax.broadcasted_iota(jnp.int32, (_RANK_CHUNK, _RANK_CHUNK), 1)
        before = jnp.where(cj < ci, 1.0, 0.0).astype(BF16)
        for d0 in range(0, tb, dom):
            seen = jnp.zeros((1, LANES), F32)
            for c0 in range(d0, d0 + dom, _RANK_CHUNK):
                routed = comb_ref[c0:c0 + _RANK_CHUNK, :] > 0.0
                hot = jnp.where(routed, 1.0, 0.0)
                rank_ref[c0:c0 + _RANK_CHUNK, :] = jnp.where(routed, _nn(before, _bf(hot)) + seen, -1.0)
                seen = seen + jnp.sum(hot, axis=0, keepdims=True)
        rank_t_ref[...] = rank_ref[...].T

    lane = lax.broadcasted_iota(jnp.int32, (dom, LANES), 1)
    mine = lane == e
    slot_r = lax.broadcasted_iota(jnp.int32, (rows, dom), 0).astype(F32)
    slot_c = lax.broadcasted_iota(jnp.int32, (dom, rows), 1).astype(F32)

    for d in range(ndom):
        ds_ = slice(d * dom, (d + 1) * dom)
        rank_col = jnp.sum(jnp.where(mine, rank_ref[ds_, :], 0.0), axis=1, keepdims=True)
        w_col = jnp.sum(jnp.where(mine, comb_ref[ds_, :], 0.0), axis=1, keepdims=True)
        w_hi, w_mid, w_lo = [u.astype(F32) for u in _split3(w_col)]
        w_cols = _bf(jnp.where(lane == 0, w_hi, jnp.where(lane == 1, w_mid, jnp.where(lane == 2, w_lo, 0.0))))
        rank_row = rank_t_ref[pl.ds(e, 1), ds_]

        def sub_tile(s, carry, ds_=ds_, rank_col=rank_col, rank_row=rank_row, w_cols=w_cols):
            base = (s * rows).astype(F32)
            gather = jnp.where(rank_row - base == slot_r, 1.0, 0.0).astype(BF16)
            x = _bf(_nn(gather, hn_ref[ds_, :]))
            w3 = _nn(gather, w_cols)
            w_rows = w3[:, 0:1] + w3[:, 1:2] + w3[:, 2:3]
            for j in range(D_EXPERT // LANES):
                gs = slice(j * _GU_GROUP, (j + 1) * _GU_GROUP)
                gu = _nn(x, wgu_ref[:, gs]) + bgu_ref[:, gs]
                glu = jnp.minimum(gu[:, 0:LANES], SWIGLU_LIMIT)
                lin = jnp.clip(gu[:, LANES:_GU_GROUP], -SWIGLU_LIMIT, SWIGLU_LIMIT)
                act_ref[:, j * LANES:(j + 1) * LANES] = _bf(glu * _sigmoid(SWIGLU_ALPHA * glu) * (lin + 1.0))
            y = (_nn(act_ref[...], wd_ref[...]) + bd_ref[...]) * w_rows
            scatter = jnp.where(rank_col - base == slot_c, 1.0, 0.0).astype(BF16)
            out_ref[ds_, :] += _nn(scatter, _bf(y))
            return carry

        lax.fori_loop(0, nsub_ref[(blk * ndom + d) * N_EXPERTS + e], sub_tile, 0)


def _moe(hn, comb, tile_counts, lw, tb):
    n = hn.shape[0]
    nblk = n // tb
    ndomains = n // min(_MOE_DOMAIN, tb)
    counts = tile_counts.reshape(ndomains, -1, LANES).sum(axis=1)[:, :N_EXPERTS].astype(jnp.int32)
    nsub = ((counts + _MOE_ROWS - 1) // _MOE_ROWS).reshape(-1)
    row = lambda i, e, ns: (i, 0)
    ex = lambda i, e, ns: (e, 0, 0)
    return pl.pallas_call(
        _moe_kernel,
        grid_spec=pltpu.PrefetchScalarGridSpec(
            num_scalar_prefetch=1,
            grid=(nblk, N_EXPERTS),
            in_specs=[
                pl.BlockSpec((tb, D_MODEL), row), pl.BlockSpec((tb, LANES), row),
                pl.BlockSpec((None, D_MODEL, 2 * D_EXPERT), ex), pl.BlockSpec((None, 1, 2 * D_EXPERT), ex),
                pl.BlockSpec((None, D_EXPERT, D_MODEL), ex), pl.BlockSpec((None, 1, D_MODEL), ex),
            ],
            out_specs=pl.BlockSpec((tb, D_MODEL), row),
            scratch_shapes=[pltpu.VMEM((tb, LANES), F32), pltpu.VMEM((LANES, tb), F32),
                            pltpu.VMEM((_MOE_ROWS, D_EXPERT), BF16)],
        ),
        out_shape=jax.ShapeDtypeStruct((n, D_MODEL), F32),
        compiler_params=_params(2),
    )(nsub, hn, comb, lw["w_gu"], lw["b_gu"], lw["w_down"], lw["b_down"])


def _final_kernel(x1_ref, moe_ref, g_ref, y_ref):
    xo = x1_ref[...] + moe_ref[...]
    ms = jnp.mean(xo * xo, axis=-1, keepdims=True)
    y_ref[...] = xo * lax.rsqrt(ms + RMS_EPS) * g_ref[...]


def _final(x1, moe, gf, tm):
    n = x1.shape[0]
    row = pl.BlockSpec((tm, D_MODEL), lambda i: (i, 0))
    return pl.pallas_call(
        _final_kernel,
        grid=(n // tm,),
        in_specs=[row, row, pl.BlockSpec((1, D_MODEL), lambda i: (0, 0))],
        out_specs=row,
        out_shape=jax.ShapeDtypeStruct((n, D_MODEL), F32),
        compiler_params=_params(1),
    )(x1, moe, gf)


def _prep_layer(l, norm_mix_g, w_in, attn_sinks, rw_mu, rw_w0, rw_w_up, rw_a0, rw_a_up, rw_g_up,
                rw_k_k, rw_k_a, rw_r_k, rw_lnx_g, rw_lnx_b, w_branch_attn, w_branch_rwkv, w_out,
                norm_ffn_g, w_router, b_router, w_gate_up, b_gate_up, w_down, b_down):
    zeros = jnp.zeros((W_LORA, RW_WIDTH), F32)
    wa_up = jnp.concatenate([jnp.concatenate([rw_w_up[l], zeros], axis=1),
                             jnp.concatenate([zeros, rw_a_up[l]], axis=1)], axis=0)
    pad = LANES - N_EXPERTS
    return {
        "norm_mix_g": norm_mix_g[l].reshape(1, D_MODEL),
        "w_in": _bf(w_in[l]),
        "attn_sinks": attn_sinks[l].astype(F32),
        "rw_mu": rw_mu[l], "rw_w0": rw_w0[l], "rw_a0": rw_a0[l], "rw_k_k": rw_k_k[l], "rw_k_a": rw_k_a[l],
        "rw_r_k": rw_r_k[l], "rw_lnx_g": rw_lnx_g[l], "rw_lnx_b": rw_lnx_b[l],
        "wa_up": _bf(wa_up), "g_up": _bf(rw_g_up[l]),
        "w_ba": _bf(w_branch_attn[l]), "w_br": _bf(w_branch_rwkv[l]), "w_o": _bf(w_out[l]),
        "norm_ffn_g": norm_ffn_g[l].reshape(1, D_MODEL),
        "w_r": _bf(jnp.pad(w_router[l], ((0, 0), (0, pad)))),
        "b_r": jnp.pad(b_router[l], (0, pad), constant_values=-jnp.inf).reshape(1, LANES),
        "w_gu": _gu_regroup(w_gate_up[l]),
        "b_gu": b_gate_up[l].reshape(N_EXPERTS, D_EXPERT // LANES, LANES, 2).transpose(0, 1, 3, 2)
                .reshape(N_EXPERTS, 1, 2 * D_EXPERT),
        "w_down": _bf(w_down[l]), "b_down": b_down[l][:, None, :],
    }


def _tile(n, pref):
    tm = pref
    while n % tm:
        tm //= 2
    return tm


def _layer(x, tables, cache_k, cache_v, s0, shift0, lw, gf, chunk, nseg):
    b, t, _ = x.shape
    n = b * t
    x2d = x.reshape(n, D_MODEL)
    tm = _tile(n, 256)
    q, k, v, zr, zg = _inproj(x2d, lw["norm_mix_g"], lw["w_in"], tables, tm)
    q3, k3, v3 = q.reshape(b, t, ATT_Q), k.reshape(b, t, ATT_KV), v.reshape(b, t, ATT_KV)
    if cache_k is None:
        o_att, k_win, v_win = _attn_prompt(q3, k3, v3, lw["attn_sinks"])
    else:
        o_att, k_win, v_win = _attn_sample(q3, k3, v3, lw["attn_sinks"],
                                           cache_k.reshape(b, WINDOW, ATT_KV), cache_v.reshape(b, WINDOW, ATT_KV))
    zr3 = zr.reshape(b, t, RW_COLS)
    o_rw, s_bd = _rwkv(zr3, _state_to_bd(s0), shift0, lw, chunk, nseg)
    x1, hn, comb, tile_counts = _mix(x2d, o_att.reshape(n, ATT_Q), o_rw.reshape(n, RW_WIDTH), zg, lw, tm)
    moe = _moe(hn, comb, tile_counts, lw, _tile(n, MOE_BLOCK))
    y = _final(x1, moe, gf, tm)
    kv_shape = (b, WINDOW, N_KV_HEADS, HEAD_DIM)
    return (y.reshape(b, t, D_MODEL), k_win.reshape(kv_shape), v_win.reshape(kv_shape),
            _state_from_bd(s_bd), zr3[:, t - 1, :])


def kernel(x_prompt, x_sample, cache_k, cache_v, state_wkv, state_shift, norm_mix_g, w_in, attn_sinks, rw_mu, rw_w0, rw_w_up, rw_a0, rw_a_up, rw_g_up, rw_k_k, rw_k_a, rw_r_k, rw_lnx_g, rw_lnx_b, w_branch_attn, w_branch_rwkv, w_out, norm_ffn_g, w_router, b_router, w_gate_up, b_gate_up, w_down, b_down, norm_final_g):
    assert w_in.shape[0] == 1, "single-layer trunk"
    bp, tp, _ = x_prompt.shape
    bs, ts, _ = x_sample.shape
    lw = _prep_layer(0, norm_mix_g, w_in, attn_sinks, rw_mu, rw_w0, rw_w_up, rw_a0, rw_a_up, rw_g_up,
                     rw_k_k, rw_k_a, rw_r_k, rw_lnx_g, rw_lnx_b, w_branch_attn, w_branch_rwkv, w_out,
                     norm_ffn_g, w_router, b_router, w_gate_up, b_gate_up, w_down, b_down)
    gf = norm_final_g.reshape(1, D_MODEL)

    tm_p = _tile(bp * tp, 256)
    tab_p = _rope_tables(jnp.arange(tp, dtype=jnp.int32))
    tm_s = _tile(bs * ts, 256)
    pos_s = PAST_LEN + jnp.arange(ts, dtype=jnp.int32)
    tab_s = tuple(jnp.tile(u, (tm_s // ts, 1)) for u in _rope_tables(pos_s))
    del tm_p

    s0p = jnp.zeros((bp, RW_HEADS, RW_HEAD, RW_HEAD), state_wkv.dtype)
    sh0p = jnp.zeros((bp, RW_COLS), state_shift.dtype)
    yp, kp, vp, sp, shp = _layer(x_prompt, tab_p, None, None, s0p, sh0p, lw, gf, RW_CHUNK, 1)
    ys, ks, vs, ss, shs = _layer(x_sample, tab_s, cache_k[0], cache_v[0], state_wkv[0], state_shift[0],
                                 lw, gf, RW_CHUNK, RW_CHUNK // ts)
    ex = lambda u: u[None]
    return (yp, ys, ex(kp), ex(vp), ex(sp), ex(shp), ex(ks), ex(vs), ex(ss), ex(shs))
```

```python
import functools
import math

import jax
import jax.numpy as jnp
from jax import lax
from jax.experimental import pallas as pl
from jax.experimental.pallas import tpu as pltpu

F32 = jnp.float32
BF16 = jnp.bfloat16

LANES = 128
SUBLANES = 8
VMEM_LIMIT_BYTES = 56 * 1024 * 1024

D_MODEL = 1024
HEAD_DIM = 64
N_Q_HEADS = 16
N_KV_HEADS = 4
Q_PER_KV = 4
WINDOW = 128
ROPE_THETA = 500000.0
ROPE_DIM = 16
ROPE_HALF = 8
ATTN_SCALE = HEAD_DIM ** -0.5
PAST_LEN = 16384
RW_HEAD = 64
RW_HEADS = 16
RW_PAIRS = RW_HEADS // 2
W_LORA = 64
A_LORA = 64
G_LORA = 128
LNX_EPS = 64e-5
N_EXPERTS = 32
TOP_K = 4
D_EXPERT = 1024
SWIGLU_LIMIT = 7.0
SWIGLU_ALPHA = 1.702
RMS_EPS = 1e-5
ATT_Q = N_Q_HEADS * HEAD_DIM
ATT_KV = N_KV_HEADS * HEAD_DIM
RW_WIDTH = RW_HEADS * RW_HEAD
RW_COLS = 3 * RW_WIDTH + W_LORA + A_LORA + G_LORA
GATE_COLS = 2 * D_MODEL
IN_COLS = ATT_Q + 2 * ATT_KV + RW_COLS + GATE_COLS
DECAY_SCALE = math.exp(-0.5)
RW_CHUNK = 64
MOE_BLOCK = 2048


def _nn(a, b):
    return jnp.dot(a, b, preferred_element_type=F32)


def _nt(a, b):
    return lax.dot_general(a, b, (((1,), (1,)), ((), ())), preferred_element_type=F32)


def _tn(a, b):
    return lax.dot_general(a, b, (((0,), (0,)), ((), ())), preferred_element_type=F32)


def _bf(x):
    return x.astype(BF16)


def _sigmoid(x):
    return 1.0 / (1.0 + jnp.exp(-x))


def _split2(x):
    hi = x.astype(BF16)
    lo = (x - hi.astype(F32)).astype(BF16)
    return hi, lo


def _split3(x):
    hi = x.astype(BF16)
    r1 = x - hi.astype(F32)
    mid = r1.astype(BF16)
    lo = (r1 - mid.astype(F32)).astype(BF16)
    return hi, mid, lo


def _params(n_axes):
    return pltpu.CompilerParams(
        dimension_semantics=("arbitrary",) * n_axes, vmem_limit_bytes=VMEM_LIMIT_BYTES)


_IN_CHUNK = 512


def _inproj_kernel(x_ref, g_ref, w_ref, cos_ref, sa_ref, sb_ref,
                   q_ref, k_ref, v_ref, zr_ref, zg_ref):
    x = x_ref[...]
    ms = jnp.mean(x * x, axis=-1, keepdims=True)
    h = _bf(x * lax.rsqrt(ms + RMS_EPS) * g_ref[...])
    cos, sa, sb = cos_ref[...], sa_ref[...], sb_ref[...]

    def rope(y):
        return (y * cos + pltpu.roll(y, LANES - ROPE_HALF, axis=1) * sa
                + pltpu.roll(y, ROPE_HALF, axis=1) * sb)

    def project(out_ref, col0, width, with_rope):
        for c in range(0, width, _IN_CHUNK):
            cw = min(_IN_CHUNK, width - c)
            acc = _nn(h, w_ref[:, col0 + c:col0 + c + cw])
            if with_rope:
                for j in range(0, cw, LANES):
                    out_ref[:, c + j:c + j + LANES] = rope(acc[:, j:j + LANES])
            else:
                out_ref[:, c:c + cw] = acc

    project(q_ref, 0, ATT_Q, True)
    project(k_ref, ATT_Q, ATT_KV, True)
    project(v_ref, ATT_Q + ATT_KV, ATT_KV, False)
    project(zr_ref, ATT_Q + 2 * ATT_KV, RW_COLS, False)
    project(zg_ref, ATT_Q + 2 * ATT_KV + RW_COLS, GATE_COLS, False)


def _rope_tables(positions):
    inv_freq = ROPE_THETA ** (-jnp.arange(ROPE_HALF, dtype=F32) / ROPE_HALF)
    ang = positions.astype(F32)[:, None] * inv_freq[None, :]
    cos, sin = jnp.cos(ang), jnp.sin(ang)
    t = positions.shape[0]
    one = jnp.ones((t, HEAD_DIM - ROPE_DIM), F32)
    zero = jnp.zeros((t, HEAD_DIM - ROPE_DIM), F32)
    z8 = jnp.zeros((t, ROPE_HALF), F32)
    cos_h = jnp.concatenate([cos, cos, one], axis=1)
    sa_h = jnp.concatenate([-sin, z8, zero], axis=1)
    sb_h = jnp.concatenate([z8, sin, zero], axis=1)
    two = lambda u: jnp.concatenate([u, u], axis=1)
    return two(cos_h), two(sa_h), two(sb_h)


def _inproj(x2d, g, w_bf, tables, tm):
    n = x2d.shape[0]
    cos, sa, sb = tables
    nper = cos.shape[0] // tm
    row = lambda i: (i, 0)
    tab = lambda i: (i % nper, 0)
    const = lambda i: (0, 0)
    out_shapes = [jax.ShapeDtypeStruct((n, w), F32) for w in (ATT_Q, ATT_KV, ATT_KV, RW_COLS, GATE_COLS)]
    return pl.pallas_call(
        _inproj_kernel,
        grid=(n // tm,),
        in_specs=[
            pl.BlockSpec((tm, D_MODEL), row),
            pl.BlockSpec((1, D_MODEL), const),
            pl.BlockSpec((D_MODEL, IN_COLS), const, pipeline_mode=pl.Buffered(1)),
            pl.BlockSpec((tm, LANES), tab),
            pl.BlockSpec((tm, LANES), tab),
            pl.BlockSpec((tm, LANES), tab),
        ],
        out_specs=[pl.BlockSpec((tm, w), row) for w in (ATT_Q, ATT_KV, ATT_KV, RW_COLS, GATE_COLS)],
        out_shape=out_shapes,
        compiler_params=_params(1),
    )(x2d, g, w_bf, cos, sa, sb)


def _attn_prompt_kernel(sink_ref, q_ref, kp_ref, kc_ref, vp_ref, vc_ref, o_ref, kw_ref, vw_ref):
    n = pl.program_id(1)
    w = WINDOW
    lane = lax.broadcasted_iota(jnp.int32, (w, LANES), 1)
    first_half = lane < HEAD_DIM
    qi = lax.broadcasted_iota(jnp.int32, (Q_PER_KV * w, 2 * w), 0) % w
    kj = lax.broadcasted_iota(jnp.int32, (Q_PER_KV * w, 2 * w), 1)
    first_key = jnp.where(n > 0, 0, w)
    band4 = (kj > qi) & (kj <= qi + w) & (kj >= first_key)

    kw_ref[...] = kc_ref[...]
    vw_ref[...] = vc_ref[...]

    lane2 = lax.broadcasted_iota(jnp.int32, (2 * w, LANES), 1)
    kvs = range(N_KV_HEADS)
    k2, v2 = [], []
    for kv in kvs:
        gs = slice((kv // 2) * LANES, (kv // 2 + 1) * LANES)
        mine = (lane2 < HEAD_DIM) if kv % 2 == 0 else (lane2 >= HEAD_DIM)
        k2.append(_bf(jnp.concatenate([kp_ref[:, gs], kc_ref[:, gs]], axis=0)))
        v2.append(_bf(jnp.where(mine, jnp.concatenate([vp_ref[:, gs], vc_ref[:, gs]], axis=0), 1.0)))
    qs, sink = [], []
    for kv in kvs:
        jh = kv % 2
        keep = first_half if jh == 0 else jnp.logical_not(first_half)
        rows, sinks = [], []
        for gq in range(Q_PER_KV):
            hq = kv * Q_PER_KV + gq
            qg = q_ref[:, (hq // 2) * LANES:(hq // 2 + 1) * LANES]
            if hq % 2 != jh:
                qg = pltpu.roll(qg, HEAD_DIM, axis=1)
            rows.append(jnp.where(keep, qg * ATTN_SCALE, 0.0))
            sinks.append(jnp.full((w, 1), sink_ref[hq], F32))
        qs.append(_bf(jnp.concatenate(rows, axis=0)))
        sink.append(jnp.concatenate(sinks, axis=0))
    s = [jnp.where(band4, _nt(q_, k_), -jnp.inf) for q_, k_ in zip(qs, k2)]
    m = [jnp.maximum(jnp.max(s_, axis=-1, keepdims=True), sk) for s_, sk in zip(s, sink)]
    e = [_bf(jnp.exp(s_ - m_)) for s_, m_ in zip(s, m)]
    e_sink = [jnp.exp(sk - m_) for sk, m_ in zip(sink, m)]
    pv = [_nn(e_, v_) for e_, v_ in zip(e, v2)]
    for kv in kvs:
        jh = kv % 2
        for go in range(Q_PER_KV // 2):
            ra, rb = slice(2 * go * w, (2 * go + 1) * w), slice((2 * go + 1) * w, (2 * go + 2) * w)
            pa, pb = pv[kv][ra], pv[kv][rb]
            pa_r, pb_r = pltpu.roll(pa, HEAD_DIM, axis=1), pltpu.roll(pb, HEAD_DIM, axis=1)
            if jh == 0:
                oa, ob = pa / (pa_r + e_sink[kv][ra]), pb_r / (pb + e_sink[kv][rb])
            else:
                oa, ob = pa_r / (pa + e_sink[kv][ra]), pb / (pb_r + e_sink[kv][rb])
            og = kv * (Q_PER_KV // 2) + go
            o_ref[:, og * LANES:(og + 1) * LANES] = jnp.where(first_half, oa, ob).astype(o_ref.dtype)


def _attn_prompt(q, k, v, sinks):
    b, t, _ = q.shape
    nb = t // WINDOW
    cur = lambda bi, n: (bi, n, 0)
    prev = lambda bi, n: (bi, jnp.maximum(n - 1, 0), 0)
    win = lambda bi, n: (bi, 0, 0)
    return pl.pallas_call(
        _attn_prompt_kernel,
        grid=(b, nb),
        in_specs=[
            pl.BlockSpec(memory_space=pltpu.SMEM),
            pl.BlockSpec((None, WINDOW, ATT_Q), cur),
            pl.BlockSpec((None, WINDOW, ATT_KV), prev),
            pl.BlockSpec((None, WINDOW, ATT_KV), cur),
            pl.BlockSpec((None, WINDOW, ATT_KV), prev),
            pl.BlockSpec((None, WINDOW, ATT_KV), cur),
        ],
        out_specs=[
            pl.BlockSpec((None, WINDOW, ATT_Q), cur),
            pl.BlockSpec((None, WINDOW, ATT_KV), win),
            pl.BlockSpec((None, WINDOW, ATT_KV), win),
        ],
        out_shape=[
            jax.ShapeDtypeStruct((b, t, ATT_Q), BF16),
            jax.ShapeDtypeStruct((b, WINDOW, ATT_KV), F32),
            jax.ShapeDtypeStruct((b, WINDOW, ATT_KV), F32),
        ],
        compiler_params=_params(2),
    )(sinks, q, k, k, v, v)


_SAMPLE_BT = 8


def _attn_sample_kernel(sink_ref, q_ref, k_ref, v_ref, ck_ref, cv_ref, o_ref, nk_ref, nv_ref, *, t):
    w = WINDOW
    rows_per_grp = 2 * Q_PER_KV * t
    lane = lax.broadcasted_iota(jnp.int32, (t, LANES), 1)
    first_half = lane < HEAD_DIM
    r_c = lax.broadcasted_iota(jnp.int32, (rows_per_grp, w), 0) % t
    c_c = lax.broadcasted_iota(jnp.int32, (rows_per_grp, w), 1)
    mask_c = c_c > r_c
    r_n = lax.broadcasted_iota(jnp.int32, (rows_per_grp, t), 0) % t
    c_n = lax.broadcasted_iota(jnp.int32, (rows_per_grp, t), 1)
    mask_n = c_n <= r_n

    for bi in range(_SAMPLE_BT):
        nk_ref[bi, 0:w - t, :] = ck_ref[bi, t:w, :]
        nk_ref[bi, w - t:w, :] = k_ref[bi]
        nv_ref[bi, 0:w - t, :] = cv_ref[bi, t:w, :]
        nv_ref[bi, w - t:w, :] = v_ref[bi]

    chains = [(bi, grp) for bi in range(_SAMPLE_BT) for grp in range(ATT_KV // LANES)]
    gsl = lambda grp: slice(grp * LANES, (grp + 1) * LANES)
    kc = [_bf(ck_ref[bi, :, gsl(grp)]) for bi, grp in chains]
    vc = [_bf(cv_ref[bi, :, gsl(grp)]) for bi, grp in chains]
    kn = [_bf(k_ref[bi, :, gsl(grp)]) for bi, grp in chains]
    vn = [_bf(v_ref[bi, :, gsl(grp)]) for bi, grp in chains]
    sink_rows = [[jnp.full((t, 1), sink_ref[(grp * 2 + jh) * Q_PER_KV + gq], F32)
                  for jh in range(2) for gq in range(Q_PER_KV)] for grp in range(ATT_KV // LANES)]
    sink_g = [jnp.concatenate(r_, axis=0) for r_ in sink_rows]
    qs = []
    for bi, grp in chains:
        rows = []
        for jh in range(2):
            keep = first_half if jh == 0 else jnp.logical_not(first_half)
            for gq in range(Q_PER_KV):
                hq = (grp * 2 + jh) * Q_PER_KV + gq
                qg = q_ref[bi, :, (hq // 2) * LANES:(hq // 2 + 1) * LANES]
                if hq % 2 != jh:
                    qg = pltpu.roll(qg, HEAD_DIM, axis=1)
                rows.append(jnp.where(keep, qg * ATTN_SCALE, 0.0))
        qs.append(_bf(jnp.concatenate(rows, axis=0)))
    sink = [sink_g[grp] for _, grp in chains]
    s_c = [jnp.where(mask_c, _nt(q_, k_), -jnp.inf) for q_, k_ in zip(qs, kc)]
    s_n = [jnp.where(mask_n, _nt(q_, k_), -jnp.inf) for q_, k_ in zip(qs, kn)]
    m = [jnp.maximum(jnp.maximum(jnp.max(a, axis=-1, keepdims=True), jnp.max(b_, axis=-1, keepdims=True)), sk)
         for a, b_, sk in zip(s_c, s_n, sink)]
    e_c = [jnp.exp(a - m_) for a, m_ in zip(s_c, m)]
    e_n = [jnp.exp(a - m_) for a, m_ in zip(s_n, m)]
    denom = [jnp.sum(a, axis=-1, keepdims=True) + jnp.sum(b_, axis=-1, keepdims=True) + jnp.exp(sk - m_)
             for a, b_, sk, m_ in zip(e_c, e_n, sink, m)]
    pv = [_nn(_bf(a / d_), va) + _nn(_bf(b_ / d_), vb)
          for a, b_, d_, va, vb in zip(e_c, e_n, denom, vc, vn)]
    for (bi, grp), pv_ in zip(chains, pv):
        for jh in range(2):
            for go in range(Q_PER_KV // 2):
                r0 = (jh * Q_PER_KV + 2 * go) * t
                pa, pb = pv_[r0:r0 + t], pv_[r0 + t:r0 + 2 * t]
                if jh == 0:
                    pb = pltpu.roll(pb, HEAD_DIM, axis=1)
                else:
                    pa = pltpu.roll(pa, HEAD_DIM, axis=1)
                og = (grp * 2 + jh) * (Q_PER_KV // 2) + go
                o_ref[bi, :, og * LANES:(og + 1) * LANES] = jnp.where(first_half, pa, pb)


def _attn_sample(q, k, v, sinks, cache_k, cache_v):
    b, t, _ = q.shape
    bt = _SAMPLE_BT
    blk = lambda i: (i, 0, 0)
    return pl.pallas_call(
        functools.partial(_attn_sample_kernel, t=t),
        grid=(b // bt,),
        in_specs=[
            pl.BlockSpec(memory_space=pltpu.SMEM),
            pl.BlockSpec((bt, t, ATT_Q), blk),
            pl.BlockSpec((bt, t, ATT_KV), blk),
            pl.BlockSpec((bt, t, ATT_KV), blk),
            pl.BlockSpec((bt, WINDOW, ATT_KV), blk),
            pl.BlockSpec((bt, WINDOW, ATT_KV), blk),
        ],
        out_specs=[
            pl.BlockSpec((bt, t, ATT_Q), blk),
            pl.BlockSpec((bt, WINDOW, ATT_KV), blk),
            pl.BlockSpec((bt, WINDOW, ATT_KV), blk),
        ],
        out_shape=[
            jax.ShapeDtypeStruct((b, t, ATT_Q), F32),
            jax.ShapeDtypeStruct((b, WINDOW, ATT_KV), F32),
            jax.ShapeDtypeStruct((b, WINDOW, ATT_KV), F32),
        ],
        compiler_params=_params(1),
    )(sinks, q, k, v, cache_k, cache_v)


def _rwkv_kernel(z_ref, s0_ref, shift_ref, mu_ref, w0_ref, wa_up_ref, a0_ref, g_up_ref,
                 kk_ref, ka_ref, rk_ref, lng_ref, lnb_ref, o_ref, s_ref, prev_ref, *, c, nseg):
    tseg = c // nseg
    step = pl.program_id(1)
    pairs = range(RW_PAIRS)
    sls = [slice(p * LANES, (p + 1) * LANES) for p in pairs]

    @pl.when(step == 0)
    def _():
        s_ref[...] = s0_ref[...]
        if nseg == 1:
            prev_ref[...] = shift_ref[...]

    z = z_ref[...]
    row1 = lax.broadcasted_iota(jnp.int32, (c, 1), 0)
    if nseg == 1:
        zprev = jnp.where(row1 == 0, prev_ref[...], pltpu.roll(z, 1, axis=0))
        prev_ref[...] = z[c - 1:c, :]
    else:
        zprev = jnp.where(row1 % tseg == 0, shift_ref[...], pltpu.roll(z, 1, axis=0))
    zs = z + (zprev - z) * mu_ref[...]

    w3 = 3 * RW_WIDTH
    r, k, v = zs[:, 0:RW_WIDTH], zs[:, RW_WIDTH:2 * RW_WIDTH], zs[:, 2 * RW_WIDTH:w3]
    xwa = zs[:, w3:w3 + LANES]
    xg = zs[:, w3 + LANES:w3 + 2 * LANES]
    lane = lax.broadcasted_iota(jnp.int32, (c, LANES), 1)
    head0 = lane < RW_HEAD
    lora = _nn(_bf(jnp.where(head0, jnp.tanh(xwa), xwa)), wa_up_ref[...])
    lw = -DECAY_SCALE * _sigmoid(w0_ref[...] + lora[:, 0:RW_WIDTH])
    a_sig = _sigmoid(a0_ref[...] + lora[:, RW_WIDTH:2 * RW_WIDTH])
    g = _nn(_bf(_sigmoid(xg)), g_up_ref[...])
    kk = k * kk_ref[...]
    k = k * (1.0 + (a_sig - 1.0) * ka_ref[...])
    rkr = r * k * rk_ref[...]

    ti = lax.broadcasted_iota(jnp.int32, (c, c), 0)
    tj = lax.broadcasted_iota(jnp.int32, (c, c), 1)
    same_seq = (ti // tseg) == (tj // tseg)
    tri = jnp.where((tj <= ti) & same_seq, 1.0, 0.0).astype(BF16)
    lw3 = _split3(lw)
    cum = _nn(tri, lw3[0]) + _nn(tri, lw3[1]) + _nn(tri, lw3[2])
    if nseg == 1:
        c_end = cum[c - 1:c, :]
    else:
        seq1 = jnp.where(same_seq, 1.0, 0.0).astype(BF16)
        c_end = _nn(seq1, lw3[0]) + _nn(seq1, lw3[1]) + _nn(seq1, lw3[2])
    e_c, e_ci, e_cm = jnp.exp(cum), jnp.exp(-cum), jnp.exp(cum - lw)
    e_ce, w_end = jnp.exp(c_end - cum), jnp.exp(c_end)

    gi = lax.broadcasted_iota(jnp.int32, (2 * LANES, LANES), 0) % LANES
    gj = lax.broadcasted_iota(jnp.int32, (2 * LANES, LANES), 1)
    ones2 = jnp.where((gi // RW_HEAD) == (gj // RW_HEAD), 1.0, 0.0).astype(BF16)
    bi_ = lax.broadcasted_iota(jnp.int32, (LANES, LANES), 0)
    bj_ = lax.broadcasted_iota(jnp.int32, (LANES, LANES), 1)
    same_head = (bi_ // RW_HEAD) == (bj_ // RW_HEAD)
    ci = lax.broadcasted_iota(jnp.int32, (c, 2 * c), 0)
    cj = lax.broadcasted_iota(jnp.int32, (c, 2 * c), 1)
    cjm = cj % c
    seq_ok = (ci // tseg) == (cjm // tseg)
    strict = (cjm < ci) & seq_ok
    incl = (cjm <= ci) & seq_ok
    eye_cat = jnp.where(cjm == ci, 1.0, 0.0)
    left = cj < c

    def seg_sum(x):
        hi, lo = _split2(x)
        return _nn(jnp.concatenate([hi, lo], axis=1), ones2)

    def rows2(x):
        return jnp.concatenate([jnp.where(head0, x, 0.0), jnp.where(head0, 0.0, x)], axis=0)

    def bd(cat):
        return _bf(jnp.concatenate([jnp.where(left, cat, 0.0), jnp.where(left, 0.0, cat)], axis=0))

    def pair_mm(cat, x):
        return _nn(_bf(cat), _bf(rows2(x)))

    def seg_sums(xs):
        tot = seg_sum(jnp.concatenate(xs, axis=0))
        return [tot[i * c:(i + 1) * c] for i in range(len(xs))]

    ss = seg_sums([kk[:, s] * kk[:, s] for s in sls])
    kkn = [kk[:, s] / jnp.maximum(jnp.sqrt(q), 1e-12) for s, q in zip(sls, ss)]
    bv = [n_ * a_sig[:, s] for s, n_ in zip(sls, kkn)]
    rt = [r[:, s] * e_c[:, s] for s in sls]
    kt = [k[:, s] * e_ci[:, s] for s in sls]
    at = [-n_ * e_cm[:, s] for s, n_ in zip(sls, kkn)]
    bt = [b_ * e_ci[:, s] for s, b_ in zip(sls, bv)]
    bh = [b_ * e_ce[:, s] for s, b_ in zip(sls, bv)]
    kh = [k[:, s] * e_ce[:, s] for s in sls]
    vv = [v[:, s] for s in sls]

    ar = [_bf(jnp.concatenate([a_, r_], axis=0)) for a_, r_ in zip(at, rt)]
    xbk = [_nt(x, _bf(jnp.concatenate([rows2(b_), rows2(k_)], axis=0)))
           for x, b_, k_ in zip(ar, bt, kt)]
    l_ab = [jnp.where(strict, x[0:c, 0:2 * c], 0.0) for x in xbk]
    l_ak = [jnp.where(strict, x[0:c, 2 * c:4 * c], 0.0) for x in xbk]
    m_rb = [jnp.where(incl, x[c:2 * c, 0:2 * c], 0.0) for x in xbk]
    m_rk = [jnp.where(incl, x[c:2 * c, 2 * c:4 * c], 0.0) for x in xbk]

    t_inv = [l + eye_cat for l in l_ab]
    pw = l_ab
    pw_bd = [bd(x) for x in pw]
    for _ in range(int(math.log2(tseg)) - 1):
        pw = [_nn(_bf(x), xb_) for x, xb_ in zip(pw, pw_bd)]
        pw_bd = [bd(x) for x in pw]
        t_inv = [t + _nn(_bf(t), xb_) for t, xb_ in zip(t_inv, pw_bd)]

    lvy = [pair_mm(jnp.concatenate([l, m], axis=0), v_) for l, m, v_ in zip(l_ak, m_rk, vv)]
    y1 = [x[c:2 * c] for x in lvy]
    au = [_nn(_bf(t), _bf(jnp.concatenate([rows2(a_), rows2(x[0:c])], axis=1)))
          for t, a_, x in zip(t_inv, at, lvy)]
    a_hat = [x[:, 0:LANES] for x in au]
    u0 = [x[:, LANES:2 * LANES] for x in au]

    segs = range(nseg)
    rs = [slice(q * tseg, (q + 1) * tseg) for q in segs]
    s_old = [[s_ref[q, p] for q in segs] for p in pairs]
    pp = [[_nt(_bf(jnp.concatenate([a_hat[p][rs[q]], rt[p][rs[q]]], axis=0)), _bf(s_old[p][q]))
           for q in segs] for p in pairs]
    u = [jnp.concatenate([pp[p][q][0:tseg] for q in segs], axis=0) + u0[p] for p in pairs]
    y0 = [jnp.concatenate([pp[p][q][tseg:2 * tseg] for q in segs], axis=0) for p in pairs]
    y = [y0[p] + pair_mm(m_rb[p], u[p]) + y1[p] for p in pairs]
    for p in pairs:
        for q in segs:
            upd = _tn(_bf(jnp.concatenate([u[p][rs[q]], vv[p][rs[q]]], axis=0)),
                      _bf(jnp.concatenate([bh[p][rs[q]], kh[p][rs[q]]], axis=0)))
            w_q = w_end[q * tseg:q * tseg + 1, sls[p]]
            s_ref[q, p] = s_old[p][q] * w_q + jnp.where(same_head, upd, 0.0)

    mean = [x * (1.0 / RW_HEAD) for x in seg_sums(y)]
    d = [x - m for x, m in zip(y, mean)]
    var = [x * (1.0 / RW_HEAD) for x in seg_sums([x * x for x in d])]
    bonus = [x * v_ for x, v_ in zip(seg_sums([rkr[:, s] for s in sls]), vv)]
    for p in pairs:
        s = sls[p]
        yn = d[p] * lax.rsqrt(var[p] + LNX_EPS) * lng_ref[:, s] + lnb_ref[:, s]
        o_ref[:, s] = ((yn + bonus[p]) * g[:, s]).astype(o_ref.dtype)


def _rwkv(zr, s0_bd, shift0, lw, c, nseg):
    b, t, _ = zr.shape
    if nseg == 1:
        ngrp, nchunk = b, t // c
        z3 = zr
        shift = shift0.reshape(b, 1, RW_COLS)
        shift_spec = pl.BlockSpec((None, 1, RW_COLS), lambda bi, i: (bi, 0, 0))
    else:
        assert c == nseg * t
        ngrp, nchunk = b // nseg, 1
        z3 = zr.reshape(ngrp, c, RW_COLS)
        shift = jnp.pad(shift0[:, None, :], ((0, 0), (0, t - 1), (0, 0))).reshape(ngrp, c, RW_COLS)
        shift_spec = pl.BlockSpec((None, c, RW_COLS), lambda bi, i: (bi, 0, 0))
    vec = lambda name: lw[name].reshape(1, -1).astype(F32)
    cst = lambda bi, i: (0, 0)
    vspec = lambda wd: pl.BlockSpec((1, wd), cst)
    state_spec = pl.BlockSpec((nseg, RW_PAIRS, LANES, LANES), lambda bi, i: (bi, 0, 0, 0))
    o, s_bd = pl.pallas_call(
        functools.partial(_rwkv_kernel, c=c, nseg=nseg),
        grid=(ngrp, nchunk),
        in_specs=[
            pl.BlockSpec((None, c, RW_COLS), lambda bi, i: (bi, i, 0)),
            state_spec,
            shift_spec,
            vspec(RW_COLS), vspec(RW_WIDTH),
            pl.BlockSpec((LANES, 2 * RW_WIDTH), cst),
            vspec(RW_WIDTH),
            pl.BlockSpec((G_LORA, RW_WIDTH), cst),
            vspec(RW_WIDTH), vspec(RW_WIDTH), vspec(RW_WIDTH), vspec(RW_WIDTH), vspec(RW_WIDTH),
        ],
        out_specs=[
            pl.BlockSpec((None, c, RW_WIDTH), lambda bi, i: (bi, i, 0)),
            state_spec,
        ],
        out_shape=[
            jax.ShapeDtypeStruct((ngrp, nchunk * c, RW_WIDTH), BF16),
            jax.ShapeDtypeStruct((b, RW_PAIRS, LANES, LANES), F32),
        ],
        scratch_shapes=[pltpu.VMEM((1, RW_COLS), F32)],
        compiler_params=_params(2),
    )(z3, s0_bd, shift, vec("rw_mu"), vec("rw_w0"), lw["wa_up"], vec("rw_a0"),
      lw["g_up"], vec("rw_k_k"), vec("rw_k_a"), vec("rw_r_k"), vec("rw_lnx_g"), vec("rw_lnx_b"))
    return o.reshape(b, t, RW_WIDTH), s_bd


def _state_to_bd(s):
    b = s.shape[0]
    s5 = s.reshape(b, RW_PAIRS, 2, RW_HEAD, RW_HEAD)
    eye = jnp.eye(2, dtype=s.dtype)
    bd = s5[:, :, :, :, None, :] * eye[None, None, :, None, :, None]
    return bd.reshape(b, RW_PAIRS, LANES, LANES)


def _state_from_bd(bd):
    b = bd.shape[0]
    s6 = bd.reshape(b, RW_PAIRS, 2, RW_HEAD, 2, RW_HEAD)
    d = jnp.stack([s6[:, :, 0, :, 0, :], s6[:, :, 1, :, 1, :]], axis=2)
    return d.reshape(b, RW_HEADS, RW_HEAD, RW_HEAD)


def _mix_kernel(x_ref, oa_ref, or_ref, zg_ref, wba_ref, wbr_ref, wo_ref, g_ref, wr_ref, br_ref,
                x1_ref, hn_ref, comb_ref, cnt_ref):
    ya = _nn(_bf(oa_ref[...]), wba_ref[...])
    yr = _nn(_bf(or_ref[...]), wbr_ref[...])
    merged = _sigmoid(zg_ref[:, 0:D_MODEL]) * ya + _sigmoid(zg_ref[:, D_MODEL:2 * D_MODEL]) * yr
    x1 = x_ref[...] + _nn(_bf(merged), wo_ref[...])
    x1_ref[...] = x1
    ms = jnp.mean(x1 * x1, axis=-1, keepdims=True)
    hn = x1 * lax.rsqrt(ms + RMS_EPS) * g_ref[...]
    hn_ref[...] = _bf(hn)
    logits = _nn(_bf(hn), wr_ref[...]) + br_ref[...]
    lane = lax.broadcasted_iota(jnp.int32, logits.shape, 1).astype(F32)
    work = logits
    top = None
    for _ in range(TOP_K):
        m = jnp.max(work, axis=-1, keepdims=True)
        if top is None:
            top = m
        idx = jnp.min(jnp.where(work == m, lane, float(LANES)), axis=-1, keepdims=True)
        work = jnp.where(lane == idx, -jnp.inf, work)
    e = jnp.where(work != logits, jnp.exp(logits - top), 0.0)
    comb = e / jnp.sum(e, axis=-1, keepdims=True)
    comb_ref[...] = comb
    cnt_ref[...] = jnp.sum(jnp.where(comb > 0.0, 1.0, 0.0), axis=0, keepdims=True)


def _mix(x2d, oa, orw, zg, lw, tm):
    n = x2d.shape[0]
    row = lambda i: (i, 0)
    cst = lambda i: (0, 0)
    wspec = pl.BlockSpec((D_MODEL, D_MODEL), cst)
    return pl.pallas_call(
        _mix_kernel,
        grid=(n // tm,),
        in_specs=[
            pl.BlockSpec((tm, D_MODEL), row), pl.BlockSpec((tm, D_MODEL), row), pl.BlockSpec((tm, D_MODEL), row),
            pl.BlockSpec((tm, GATE_COLS), row),
            wspec, wspec, wspec,
            pl.BlockSpec((1, D_MODEL), cst),
            pl.BlockSpec((D_MODEL, LANES), cst),
            pl.BlockSpec((1, LANES), cst),
        ],
        out_specs=[pl.BlockSpec((tm, D_MODEL), row), pl.BlockSpec((tm, D_MODEL), row),
                   pl.BlockSpec((tm, LANES), row), pl.BlockSpec((None, 1, LANES), lambda i: (i, 0, 0))],
        out_shape=[jax.ShapeDtypeStruct((n, D_MODEL), F32), jax.ShapeDtypeStruct((n, D_MODEL), BF16),
                   jax.ShapeDtypeStruct((n, LANES), F32), jax.ShapeDtypeStruct((n // tm, 1, LANES), F32)],
        compiler_params=_params(1),
    )(x2d, oa, orw, zg, lw["w_ba"], lw["w_br"], lw["w_o"], lw["norm_ffn_g"], lw["w_r"], lw["b_r"])


_GU_GROUP = 2 * LANES


def _gu_regroup_kernel(w_ref, o_ref):
    src = lax.broadcasted_iota(jnp.int32, (_GU_GROUP, _GU_GROUP), 0)
    dst = lax.broadcasted_iota(jnp.int32, (_GU_GROUP, _GU_GROUP), 1)
    want = jnp.where(dst < LANES, 2 * dst, 2 * (dst - LANES) + 1)
    perm = jnp.where(src == want, 1.0, 0.0).astype(BF16)
    for j in range(0, 2 * D_EXPERT, _GU_GROUP):
        o_ref[:, j:j + _GU_GROUP] = _nn(_bf(w_ref[:, j:j + _GU_GROUP]), perm).astype(o_ref.dtype)


def _gu_regroup(w_gate_up):
    e, d, n = w_gate_up.shape
    tk = 512
    spec = pl.BlockSpec((None, tk, n), lambda i, j: (i, j, 0))
    return pl.pallas_call(
        _gu_regroup_kernel,
        grid=(e, d // tk),
        in_specs=[spec],
        out_specs=spec,
        out_shape=jax.ShapeDtypeStruct((e, d, n), BF16),
        compiler_params=_params(2),
    )(w_gate_up)


_MOE_DOMAIN = 1024
_MOE_ROWS = 160
_RANK_CHUNK = 256


def _moe_kernel(nsub_ref, hn_ref, comb_ref, wgu_ref, bgu_ref, wd_ref, bd_ref, out_ref,
                rank_ref, rank_t_ref, comb_t_ref, act_ref):
    blk, e = pl.program_id(0), pl.program_id(1)
    tb = hn_ref.shape[0]
    dom, rows = min(_MOE_DOMAIN, tb), _MOE_ROWS
    ndom = tb // dom

    @pl.when(e == 0)
    def _():
        out_ref[...] = jnp.zeros_like(out_ref)
        ci = lax.broadcasted_iota(jnp.int32, (_RANK_CHUNK, _RANK_CHUNK), 0)
        cj = lax.broadcasted_iota(jnp.int32, (_RANK_CHUNK, _RANK_CHUNK), 1)
        before = jnp.where(cj < ci, 1.0, 0.0).astype(BF16)
        for d0 in range(0, tb, dom):
            seen = jnp.zeros((1, LANES), F32)
            for c0 in range(d0, d0 + dom, _RANK_CHUNK):
                routed = comb_ref[c0:c0 + _RANK_CHUNK, :] > 0.0
                hot = jnp.where(routed, 1.0, 0.0)
                rank_ref[c0:c0 + _RANK_CHUNK, :] = jnp.where(routed, _nn(before, _bf(hot)) + seen, -1.0)
                seen = seen + jnp.sum(hot, axis=0, keepdims=True)
        rank_t_ref[...] = rank_ref[...].T
        comb_t_ref[...] = comb_ref[...].T

    slot_r = lax.broadcasted_iota(jnp.int32, (rows, dom), 0).astype(F32)

    for d in range(ndom):
        ds_ = slice(d * dom, (d + 1) * dom)
        rank_row = rank_t_ref[pl.ds(e, 1), ds_]
        w_row = comb_t_ref[pl.ds(e, 1), ds_]

        def sub_tile(s, carry, ds_=ds_, rank_row=rank_row, w_row=w_row):
            base = (s * rows).astype(F32)
            hit = rank_row - base == slot_r
            gather = jnp.where(hit, 1.0, 0.0).astype(BF16)
            x = _bf(_nn(gather, hn_ref[ds_, :]))
            w_rows = jnp.sum(jnp.where(hit, w_row, 0.0), axis=1, keepdims=True)
            for j in range(D_EXPERT // LANES):
                gs = slice(j * _GU_GROUP, (j + 1) * _GU_GROUP)
                gu = _nn(x, wgu_ref[:, gs]) + bgu_ref[:, gs]
                glu = jnp.minimum(gu[:, 0:LANES], SWIGLU_LIMIT)
                lin = jnp.clip(gu[:, LANES:_GU_GROUP], -SWIGLU_LIMIT, SWIGLU_LIMIT)
                act_ref[:, j * LANES:(j + 1) * LANES] = _bf(glu * _sigmoid(SWIGLU_ALPHA * glu) * (lin + 1.0))
            y = (_nn(act_ref[...], wd_ref[...]) + bd_ref[...]) * w_rows
            out_ref[ds_, :] += _tn(gather, _bf(y))
            return carry

        lax.fori_loop(0, nsub_ref[(blk * ndom + d) * N_EXPERTS + e], sub_tile, 0)


def _moe(hn, comb, tile_counts, lw, tb):
    n = hn.shape[0]
    nblk = n // tb
    ndomains = n // min(_MOE_DOMAIN, tb)
    counts = tile_counts.reshape(ndomains, -1, LANES).sum(axis=1)[:, :N_EXPERTS].astype(jnp.int32)
    nsub = ((counts + _MOE_ROWS - 1) // _MOE_ROWS).reshape(-1)
    row = lambda i, e, ns: (i, 0)
    ex = lambda i, e, ns: (e, 0, 0)
    return pl.pallas_call(
        _moe_kernel,
        grid_spec=pltpu.PrefetchScalarGridSpec(
            num_scalar_prefetch=1,
            grid=(nblk, N_EXPERTS),
            in_specs=[
                pl.BlockSpec((tb, D_MODEL), row), pl.BlockSpec((tb, LANES), row),
                pl.BlockSpec((None, D_MODEL, 2 * D_EXPERT), ex), pl.BlockSpec((None, 1, 2 * D_EXPERT), ex),
                pl.BlockSpec((None, D_EXPERT, D_MODEL), ex), pl.BlockSpec((None, 1, D_MODEL), ex),
            ],
            out_specs=pl.BlockSpec((tb, D_MODEL), row),
            scratch_shapes=[pltpu.VMEM((tb, LANES), F32), pltpu.VMEM((LANES, tb), F32),
                            pltpu.VMEM((LANES, tb), F32), pltpu.VMEM((_MOE_ROWS, D_EXPERT), BF16)],
        ),
        out_shape=jax.ShapeDtypeStruct((n, D_MODEL), F32),
        compiler_params=_params(2),
    )(nsub, hn, comb, lw["w_gu"], lw["b_gu"], lw["w_down"], lw["b_down"])


def _final_kernel(x1_ref, moe_ref, g_ref, y_ref):
    xo = x1_ref[...] + moe_ref[...]
    ms = jnp.mean(xo * xo, axis=-1, keepdims=True)
    y_ref[...] = xo * lax.rsqrt(ms + RMS_EPS) * g_ref[...]


def _final(x1, moe, gf, tm):
    n = x1.shape[0]
    row = pl.BlockSpec((tm, D_MODEL), lambda i: (i, 0))
    return pl.pallas_call(
        _final_kernel,
        grid=(n // tm,),
        in_specs=[row, row, pl.BlockSpec((1, D_MODEL), lambda i: (0, 0))],
        out_specs=row,
        out_shape=jax.ShapeDtypeStruct((n, D_MODEL), F32),
        compiler_params=_params(1),
    )(x1, moe, gf)


def _prep_layer(l, norm_mix_g, w_in, attn_sinks, rw_mu, rw_w0, rw_w_up, rw_a0, rw_a_up, rw_g_up,
                rw_k_k, rw_k_a, rw_r_k, rw_lnx_g, rw_lnx_b, w_branch_attn, w_branch_rwkv, w_out,
                norm_ffn_g, w_router, b_router, w_gate_up, b_gate_up, w_down, b_down):
    zeros = jnp.zeros((W_LORA, RW_WIDTH), F32)
    wa_up = jnp.concatenate([jnp.concatenate([rw_w_up[l], zeros], axis=1),
                             jnp.concatenate([zeros, rw_a_up[l]], axis=1)], axis=0)
    pad = LANES - N_EXPERTS
    return {
        "norm_mix_g": norm_mix_g[l].reshape(1, D_MODEL),
        "w_in": _bf(w_in[l]),
        "attn_sinks": attn_sinks[l].astype(F32),
        "rw_mu": rw_mu[l], "rw_w0": rw_w0[l], "rw_a0": rw_a0[l], "rw_k_k": rw_k_k[l], "rw_k_a": rw_k_a[l],
        "rw_r_k": rw_r_k[l], "rw_lnx_g": rw_lnx_g[l], "rw_lnx_b": rw_lnx_b[l],
        "wa_up": _bf(wa_up), "g_up": _bf(rw_g_up[l]),
        "w_ba": _bf(w_branch_attn[l]), "w_br": _bf(w_branch_rwkv[l]), "w_o": _bf(w_out[l]),
        "norm_ffn_g": norm_ffn_g[l].reshape(1, D_MODEL),
        "w_r": _bf(jnp.pad(w_router[l], ((0, 0), (0, pad)))),
        "b_r": jnp.pad(b_router[l], (0, pad), constant_values=-jnp.inf).reshape(1, LANES),
        "w_gu": _gu_regroup(w_gate_up[l]),
        "b_gu": b_gate_up[l].reshape(N_EXPERTS, D_EXPERT // LANES, LANES, 2).transpose(0, 1, 3, 2)
                .reshape(N_EXPERTS, 1, 2 * D_EXPERT),
        "w_down": _bf(w_down[l]), "b_down": b_down[l][:, None, :],
    }


def _tile(n, pref):
    tm = pref
    while n % tm:
        tm //= 2
    return tm


def _layer(x, tables, cache_k, cache_v, s0, shift0, lw, gf, chunk, nseg):
    b, t, _ = x.shape
    n = b * t
    x2d = x.reshape(n, D_MODEL)
    tm = _tile(n, 256)
    q, k, v, zr, zg = _inproj(x2d, lw["norm_mix_g"], lw["w_in"], tables, tm)
    q3, k3, v3 = q.reshape(b, t, ATT_Q), k.reshape(b, t, ATT_KV), v.reshape(b, t, ATT_KV)
    if cache_k is None:
        o_att, k_win, v_win = _attn_prompt(q3, k3, v3, lw["attn_sinks"])
    else:
        o_att, k_win, v_win = _attn_sample(q3, k3, v3, lw["attn_sinks"],
                                           cache_k.reshape(b, WINDOW, ATT_KV), cache_v.reshape(b, WINDOW, ATT_KV))
    zr3 = zr.reshape(b, t, RW_COLS)
    o_rw, s_bd = _rwkv(zr3, _state_to_bd(s0), shift0, lw, chunk, nseg)
    x1, hn, comb, tile_counts = _mix(x2d, o_att.reshape(n, ATT_Q), o_rw.reshape(n, RW_WIDTH), zg, lw, tm)
    moe = _moe(hn, comb, tile_counts, lw, _tile(n, MOE_BLOCK))
    y = _final(x1, moe, gf, tm)
    kv_shape = (b, WINDOW, N_KV_HEADS, HEAD_DIM)
    return (y.reshape(b, t, D_MODEL), k_win.reshape(kv_shape), v_win.reshape(kv_shape),
            _state_from_bd(s_bd), zr3[:, t - 1, :])


def kernel(x_prompt, x_sample, cache_k, cache_v, state_wkv, state_shift, norm_mix_g, w_in, attn_sinks, rw_mu, rw_w0, rw_w_up, rw_a0, rw_a_up, rw_g_up, rw_k_k, rw_k_a, rw_r_k, rw_lnx_g, rw_lnx_b, w_branch_attn, w_branch_rwkv, w_out, norm_ffn_g, w_router, b_router, w_gate_up, b_gate_up, w_down, b_down, norm_final_g):
    assert w_in.shape[0] == 1, "single-layer trunk"
    bp, tp, _ = x_prompt.shape
    bs, ts, _ = x_sample.shape
    lw = _prep_layer(0, norm_mix_g, w_in, attn_sinks, rw_mu, rw_w0, rw_w_up, rw_a0, rw_a_up, rw_g_up,
                     rw_k_k, rw_k_a, rw_r_k, rw_lnx_g, rw_lnx_b, w_branch_attn, w_branch_rwkv, w_out,
                     norm_ffn_g, w_router, b_router, w_gate_up, b_gate_up, w_down, b_down)
    gf = norm_final_g.reshape(1, D_MODEL)

    tm_p = _tile(bp * tp, 256)
    tab_p = _rope_tables(jnp.arange(tp, dtype=jnp.int32))
    tm_s = _tile(bs * ts, 256)
    pos_s = PAST_LEN + jnp.arange(ts, dtype=jnp.int32)
    tab_s = tuple(jnp.tile(u, (tm_s // ts, 1)) for u in _rope_tables(pos_s))
    del tm_p

    s0p = jnp.zeros((bp, RW_HEADS, RW_HEAD, RW_HEAD), state_wkv.dtype)
    sh0p = jnp.zeros((bp, RW_COLS), state_shift.dtype)
    yp, kp, vp, sp, shp = _layer(x_prompt, tab_p, None, None, s0p, sh0p, lw, gf, RW_CHUNK, 1)
    ys, ks, vs, ss, shs = _layer(x_sample, tab_s, cache_k[0], cache_v[0], state_wkv[0], state_shift[0],
                                 lw, gf, RW_CHUNK, RW_CHUNK // ts)
    ex = lambda u: u[None]
    return (yp, ys, ex(kp), ex(vp), ex(sp), ex(shp), ex(ks), ex(vs), ex(ss), ex(shs))
```

```python
import functools
import math

import jax
import jax.numpy as jnp
from jax import lax
from jax.experimental import pallas as pl
from jax.experimental.pallas import tpu as pltpu

F32 = jnp.float32
BF16 = jnp.bfloat16

LANES = 128
SUBLANES = 8
VMEM_LIMIT_BYTES = 56 * 1024 * 1024

D_MODEL = 1024
HEAD_DIM = 64
N_Q_HEADS = 16
N_KV_HEADS = 4
Q_PER_KV = 4
WINDOW = 128
ROPE_THETA = 500000.0
ROPE_DIM = 16
ROPE_HALF = 8
ATTN_SCALE = HEAD_DIM ** -0.5
PAST_LEN = 16384
RW_HEAD = 64
RW_HEADS = 16
RW_PAIRS = RW_HEADS // 2
W_LORA = 64
A_LORA = 64
G_LORA = 128
LNX_EPS = 64e-5
N_EXPERTS = 32
TOP_K = 4
D_EXPERT = 1024
SWIGLU_LIMIT = 7.0
SWIGLU_ALPHA = 1.702
RMS_EPS = 1e-5
ATT_Q = N_Q_HEADS * HEAD_DIM
ATT_KV = N_KV_HEADS * HEAD_DIM
RW_WIDTH = RW_HEADS * RW_HEAD
RW_COLS = 3 * RW_WIDTH + W_LORA + A_LORA + G_LORA
GATE_COLS = 2 * D_MODEL
IN_COLS = ATT_Q + 2 * ATT_KV + RW_COLS + GATE_COLS
DECAY_SCALE = math.exp(-0.5)
RW_CHUNK = 64
MOE_BLOCK = 2048
RW_SEQS_PER_STEP = 4


def _nn(a, b):
    return jnp.dot(a, b, preferred_element_type=F32)


def _nt(a, b):
    return lax.dot_general(a, b, (((1,), (1,)), ((), ())), preferred_element_type=F32)


def _tn(a, b):
    return lax.dot_general(a, b, (((0,), (0,)), ((), ())), preferred_element_type=F32)


def _bf(x):
    return x.astype(BF16)


def _sigmoid(x):
    return 0.5 * jnp.tanh(0.5 * x) + 0.5


def _split2(x):
    hi = x.astype(BF16)
    lo = (x - hi.astype(F32)).astype(BF16)
    return hi, lo


def _split3(x):
    hi = x.astype(BF16)
    r1 = x - hi.astype(F32)
    mid = r1.astype(BF16)
    lo = (r1 - mid.astype(F32)).astype(BF16)
    return hi, mid, lo


def _params(n_axes):
    return pltpu.CompilerParams(
        dimension_semantics=("arbitrary",) * n_axes, vmem_limit_bytes=VMEM_LIMIT_BYTES)


_IN_CHUNK = 512


def _inproj_kernel(x_ref, g_ref, w_ref, cos_ref, sa_ref, sb_ref,
                   q_ref, k_ref, v_ref, zr_ref, zg_ref):
    x = x_ref[...]
    ms = jnp.mean(x * x, axis=-1, keepdims=True)
    h = _bf(x * lax.rsqrt(ms + RMS_EPS) * g_ref[...])
    cos, sa, sb = cos_ref[...], sa_ref[...], sb_ref[...]

    def rope(y):
        return (y * cos + pltpu.roll(y, LANES - ROPE_HALF, axis=1) * sa
                + pltpu.roll(y, ROPE_HALF, axis=1) * sb)

    def project(out_ref, col0, width, with_rope):
        for c in range(0, width, _IN_CHUNK):
            cw = min(_IN_CHUNK, width - c)
            acc = _nn(h, w_ref[:, col0 + c:col0 + c + cw])
            if with_rope:
                for j in range(0, cw, LANES):
                    out_ref[:, c + j:c + j + LANES] = rope(acc[:, j:j + LANES])
            else:
                out_ref[:, c:c + cw] = acc

    project(q_ref, 0, ATT_Q, True)
    project(k_ref, ATT_Q, ATT_KV, True)
    project(v_ref, ATT_Q + ATT_KV, ATT_KV, False)
    project(zr_ref, ATT_Q + 2 * ATT_KV, RW_COLS, False)
    project(zg_ref, ATT_Q + 2 * ATT_KV + RW_COLS, GATE_COLS, False)


def _rope_tables(positions):
    inv_freq = ROPE_THETA ** (-jnp.arange(ROPE_HALF, dtype=F32) / ROPE_HALF)
    ang = positions.astype(F32)[:, None] * inv_freq[None, :]
    cos, sin = jnp.cos(ang), jnp.sin(ang)
    t = positions.shape[0]
    one = jnp.ones((t, HEAD_DIM - ROPE_DIM), F32)
    zero = jnp.zeros((t, HEAD_DIM - ROPE_DIM), F32)
    z8 = jnp.zeros((t, ROPE_HALF), F32)
    cos_h = jnp.concatenate([cos, cos, one], axis=1)
    sa_h = jnp.concatenate([-sin, z8, zero], axis=1)
    sb_h = jnp.concatenate([z8, sin, zero], axis=1)
    two = lambda u: jnp.concatenate([u, u], axis=1)
    return two(cos_h), two(sa_h), two(sb_h)


def _inproj(x2d, g, w_bf, tables, tm):
    n = x2d.shape[0]
    cos, sa, sb = tables
    nper = cos.shape[0] // tm
    row = lambda i: (i, 0)
    tab = lambda i: (i % nper, 0)
    const = lambda i: (0, 0)
    out_shapes = [jax.ShapeDtypeStruct((n, w), F32) for w in (ATT_Q, ATT_KV, ATT_KV, RW_COLS, GATE_COLS)]
    return pl.pallas_call(
        _inproj_kernel,
        grid=(n // tm,),
        in_specs=[
            pl.BlockSpec((tm, D_MODEL), row),
            pl.BlockSpec((1, D_MODEL), const),
            pl.BlockSpec((D_MODEL, IN_COLS), const, pipeline_mode=pl.Buffered(1)),
            pl.BlockSpec((tm, LANES), tab),
            pl.BlockSpec((tm, LANES), tab),
            pl.BlockSpec((tm, LANES), tab),
        ],
        out_specs=[pl.BlockSpec((tm, w), row) for w in (ATT_Q, ATT_KV, ATT_KV, RW_COLS, GATE_COLS)],
        out_shape=out_shapes,
        compiler_params=_params(1),
    )(x2d, g, w_bf, cos, sa, sb)


def _attn_prompt_kernel(sink_ref, q_ref, kp_ref, kc_ref, vp_ref, vc_ref, o_ref, kw_ref, vw_ref):
    n = pl.program_id(1)
    w = WINDOW
    lane = lax.broadcasted_iota(jnp.int32, (w, LANES), 1)
    first_half = lane < HEAD_DIM
    qi = lax.broadcasted_iota(jnp.int32, (Q_PER_KV * w, 2 * w), 0) % w
    kj = lax.broadcasted_iota(jnp.int32, (Q_PER_KV * w, 2 * w), 1)
    first_key = jnp.where(n > 0, 0, w)
    band4 = (kj > qi) & (kj <= qi + w) & (kj >= first_key)

    kw_ref[...] = kc_ref[...]
    vw_ref[...] = vc_ref[...]

    lane2 = lax.broadcasted_iota(jnp.int32, (2 * w, LANES), 1)
    kvs = range(N_KV_HEADS)
    k2, v2 = [], []
    for kv in kvs:
        gs = slice((kv // 2) * LANES, (kv // 2 + 1) * LANES)
        mine = (lane2 < HEAD_DIM) if kv % 2 == 0 else (lane2 >= HEAD_DIM)
        k2.append(_bf(jnp.concatenate([kp_ref[:, gs], kc_ref[:, gs]], axis=0)))
        v2.append(_bf(jnp.where(mine, jnp.concatenate([vp_ref[:, gs], vc_ref[:, gs]], axis=0), 1.0)))
    qs, sink = [], []
    for kv in kvs:
        jh = kv % 2
        keep = first_half if jh == 0 else jnp.logical_not(first_half)
        rows, sinks = [], []
        for gq in range(Q_PER_KV):
            hq = kv * Q_PER_KV + gq
            qg = q_ref[:, (hq // 2) * LANES:(hq // 2 + 1) * LANES]
            if hq % 2 != jh:
                qg = pltpu.roll(qg, HEAD_DIM, axis=1)
            rows.append(jnp.where(keep, qg * ATTN_SCALE, 0.0))
            sinks.append(jnp.full((w, 1), sink_ref[hq], F32))
        qs.append(_bf(jnp.concatenate(rows, axis=0)))
        sink.append(jnp.concatenate(sinks, axis=0))
    s = [jnp.where(band4, _nt(q_, k_), -jnp.inf) for q_, k_ in zip(qs, k2)]
    m = [jnp.maximum(jnp.max(s_, axis=-1, keepdims=True), sk) for s_, sk in zip(s, sink)]
    e = [_bf(jnp.exp(s_ - m_)) for s_, m_ in zip(s, m)]
    e_sink = [jnp.exp(sk - m_) for sk, m_ in zip(sink, m)]
    pv = [_nn(e_, v_) for e_, v_ in zip(e, v2)]
    for kv in kvs:
        jh = kv % 2
        for go in range(Q_PER_KV // 2):
            ra, rb = slice(2 * go * w, (2 * go + 1) * w), slice((2 * go + 1) * w, (2 * go + 2) * w)
            pa, pb = pv[kv][ra], pv[kv][rb]
            pa_r, pb_r = pltpu.roll(pa, HEAD_DIM, axis=1), pltpu.roll(pb, HEAD_DIM, axis=1)
            if jh == 0:
                oa, ob = pa / (pa_r + e_sink[kv][ra]), pb_r / (pb + e_sink[kv][rb])
            else:
                oa, ob = pa_r / (pa + e_sink[kv][ra]), pb / (pb_r + e_sink[kv][rb])
            og = kv * (Q_PER_KV // 2) + go
            o_ref[:, og * LANES:(og + 1) * LANES] = jnp.where(first_half, oa, ob).astype(o_ref.dtype)


def _attn_prompt(q, k, v, sinks):
    b, t, _ = q.shape
    nb = t // WINDOW
    cur = lambda bi, n: (bi, n, 0)
    prev = lambda bi, n: (bi, jnp.maximum(n - 1, 0), 0)
    win = lambda bi, n: (bi, 0, 0)
    return pl.pallas_call(
        _attn_prompt_kernel,
        grid=(b, nb),
        in_specs=[
            pl.BlockSpec(memory_space=pltpu.SMEM),
            pl.BlockSpec((None, WINDOW, ATT_Q), cur),
            pl.BlockSpec((None, WINDOW, ATT_KV), prev),
            pl.BlockSpec((None, WINDOW, ATT_KV), cur),
            pl.BlockSpec((None, WINDOW, ATT_KV), prev),
            pl.BlockSpec((None, WINDOW, ATT_KV), cur),
        ],
        out_specs=[
            pl.BlockSpec((None, WINDOW, ATT_Q), cur),
            pl.BlockSpec((None, WINDOW, ATT_KV), win),
            pl.BlockSpec((None, WINDOW, ATT_KV), win),
        ],
        out_shape=[
            jax.ShapeDtypeStruct((b, t, ATT_Q), BF16),
            jax.ShapeDtypeStruct((b, WINDOW, ATT_KV), F32),
            jax.ShapeDtypeStruct((b, WINDOW, ATT_KV), F32),
        ],
        compiler_params=_params(2),
    )(sinks, q, k, k, v, v)


_SAMPLE_BT = 8


def _attn_sample_kernel(sink_ref, q_ref, k_ref, v_ref, ck_ref, cv_ref, o_ref, nk_ref, nv_ref, *, t):
    w = WINDOW
    rows_per_grp = 2 * Q_PER_KV * t
    lane = lax.broadcasted_iota(jnp.int32, (t, LANES), 1)
    first_half = lane < HEAD_DIM
    r_c = lax.broadcasted_iota(jnp.int32, (rows_per_grp, w), 0) % t
    c_c = lax.broadcasted_iota(jnp.int32, (rows_per_grp, w), 1)
    mask_c = c_c > r_c
    r_n = lax.broadcasted_iota(jnp.int32, (rows_per_grp, t), 0) % t
    c_n = lax.broadcasted_iota(jnp.int32, (rows_per_grp, t), 1)
    mask_n = c_n <= r_n

    for bi in range(_SAMPLE_BT):
        nk_ref[bi, 0:w - t, :] = ck_ref[bi, t:w, :]
        nk_ref[bi, w - t:w, :] = k_ref[bi]
        nv_ref[bi, 0:w - t, :] = cv_ref[bi, t:w, :]
        nv_ref[bi, w - t:w, :] = v_ref[bi]

    chains = [(bi, grp) for bi in range(_SAMPLE_BT) for grp in range(ATT_KV // LANES)]
    gsl = lambda grp: slice(grp * LANES, (grp + 1) * LANES)
    kc = [_bf(ck_ref[bi, :, gsl(grp)]) for bi, grp in chains]
    vc = [_bf(cv_ref[bi, :, gsl(grp)]) for bi, grp in chains]
    kn = [_bf(k_ref[bi, :, gsl(grp)]) for bi, grp in chains]
    vn = [_bf(v_ref[bi, :, gsl(grp)]) for bi, grp in chains]
    sink_rows = [[jnp.full((t, 1), sink_ref[(grp * 2 + jh) * Q_PER_KV + gq], F32)
                  for jh in range(2) for gq in range(Q_PER_KV)] for grp in range(ATT_KV // LANES)]
    sink_g = [jnp.concatenate(r_, axis=0) for r_ in sink_rows]
    qs = []
    for bi, grp in chains:
        rows = []
        for jh in range(2):
            keep = first_half if jh == 0 else jnp.logical_not(first_half)
            for gq in range(Q_PER_KV):
                hq = (grp * 2 + jh) * Q_PER_KV + gq
                qg = q_ref[bi, :, (hq // 2) * LANES:(hq // 2 + 1) * LANES]
                if hq % 2 != jh:
                    qg = pltpu.roll(qg, HEAD_DIM, axis=1)
                rows.append(jnp.where(keep, qg * ATTN_SCALE, 0.0))
        qs.append(_bf(jnp.concatenate(rows, axis=0)))
    sink = [sink_g[grp] for _, grp in chains]
    s_c = [jnp.where(mask_c, _nt(q_, k_), -jnp.inf) for q_, k_ in zip(qs, kc)]
    s_n = [jnp.where(mask_n, _nt(q_, k_), -jnp.inf) for q_, k_ in zip(qs, kn)]
    m = [jnp.maximum(jnp.maximum(jnp.max(a, axis=-1, keepdims=True), jnp.max(b_, axis=-1, keepdims=True)), sk)
         for a, b_, sk in zip(s_c, s_n, sink)]
    e_c = [jnp.exp(a - m_) for a, m_ in zip(s_c, m)]
    e_n = [jnp.exp(a - m_) for a, m_ in zip(s_n, m)]
    denom = [jnp.sum(a, axis=-1, keepdims=True) + jnp.sum(b_, axis=-1, keepdims=True) + jnp.exp(sk - m_)
             for a, b_, sk, m_ in zip(e_c, e_n, sink, m)]
    pv = [_nn(_bf(a / d_), va) + _nn(_bf(b_ / d_), vb)
          for a, b_, d_, va, vb in zip(e_c, e_n, denom, vc, vn)]
    for (bi, grp), pv_ in zip(chains, pv):
        for jh in range(2):
            for go in range(Q_PER_KV // 2):
                r0 = (jh * Q_PER_KV + 2 * go) * t
                pa, pb = pv_[r0:r0 + t], pv_[r0 + t:r0 + 2 * t]
                if jh == 0:
                    pb = pltpu.roll(pb, HEAD_DIM, axis=1)
                else:
                    pa = pltpu.roll(pa, HEAD_DIM, axis=1)
                og = (grp * 2 + jh) * (Q_PER_KV // 2) + go
                o_ref[bi, :, og * LANES:(og + 1) * LANES] = jnp.where(first_half, pa, pb)


def _attn_sample(q, k, v, sinks, cache_k, cache_v):
    b, t, _ = q.shape
    bt = _SAMPLE_BT
    blk = lambda i: (i, 0, 0)
    return pl.pallas_call(
        functools.partial(_attn_sample_kernel, t=t),
        grid=(b // bt,),
        in_specs=[
            pl.BlockSpec(memory_space=pltpu.SMEM),
            pl.BlockSpec((bt, t, ATT_Q), blk),
            pl.BlockSpec((bt, t, ATT_KV), blk),
            pl.BlockSpec((bt, t, ATT_KV), blk),
            pl.BlockSpec((bt, WINDOW, ATT_KV), blk),
            pl.BlockSpec((bt, WINDOW, ATT_KV), blk),
        ],
        out_specs=[
            pl.BlockSpec((bt, t, ATT_Q), blk),
            pl.BlockSpec((bt, WINDOW, ATT_KV), blk),
            pl.BlockSpec((bt, WINDOW, ATT_KV), blk),
        ],
        out_shape=[
            jax.ShapeDtypeStruct((b, t, ATT_Q), F32),
            jax.ShapeDtypeStruct((b, WINDOW, ATT_KV), F32),
            jax.ShapeDtypeStruct((b, WINDOW, ATT_KV), F32),
        ],
        compiler_params=_params(1),
    )(sinks, q, k, v, cache_k, cache_v)


def _rwkv_kernel(z_ref, s0_ref, shift_ref, mu_ref, w0_ref, wa_up_ref, a0_ref, g_up_ref,
                 kk_ref, ka_ref, rk_ref, lng_ref, lnb_ref, o_ref, sout_ref, s_ref, prev_ref, *, nb, c, nseg):
    tseg = c // nseg
    rows_all = nb * c
    step = pl.program_id(1)
    pairs = range(RW_PAIRS)
    sls = [slice(p * LANES, (p + 1) * LANES) for p in pairs]
    rsl = [slice(j * c, (j + 1) * c) for j in range(nb)]

    @pl.when(step == 0)
    def _():
        zero = jnp.zeros((RW_HEAD, RW_HEAD), F32)
        for q in range(nb * nseg):
            for p in pairs:
                top = jnp.concatenate([s0_ref[q, 2 * p], zero], axis=1)
                bot = jnp.concatenate([zero, s0_ref[q, 2 * p + 1]], axis=1)
                s_ref[q, p] = jnp.concatenate([top, bot], axis=0)
        if nseg == 1:
            prev_ref[...] = shift_ref[...].reshape(nb, RW_COLS)

    z = z_ref[...].reshape(rows_all, RW_COLS)
    row1 = lax.broadcasted_iota(jnp.int32, (rows_all, 1), 0)
    zprev = pltpu.roll(z, 1, axis=0)
    if nseg == 1:
        for j in range(nb):
            zprev = jnp.where(row1 == j * c, prev_ref[j:j + 1, :], zprev)
            prev_ref[j:j + 1, :] = z[(j + 1) * c - 1:(j + 1) * c, :]
    else:
        zprev = jnp.where(row1 % tseg == 0, shift_ref[...].reshape(rows_all, RW_COLS), zprev)
    zs = z + (zprev - z) * mu_ref[...]

    w3 = 3 * RW_WIDTH
    r, k, v = zs[:, 0:RW_WIDTH], zs[:, RW_WIDTH:2 * RW_WIDTH], zs[:, 2 * RW_WIDTH:w3]
    xwa = zs[:, w3:w3 + LANES]
    xg = zs[:, w3 + LANES:w3 + 2 * LANES]
    lane_all = lax.broadcasted_iota(jnp.int32, (rows_all, LANES), 1)
    lora = _nn(_bf(jnp.where(lane_all < RW_HEAD, jnp.tanh(xwa), xwa)), wa_up_ref[...])
    lane = lax.broadcasted_iota(jnp.int32, (c, LANES), 1)
    head0 = lane < RW_HEAD
    lw = -DECAY_SCALE * _sigmoid(w0_ref[...] + lora[:, 0:RW_WIDTH])
    a_sig = _sigmoid(a0_ref[...] + lora[:, RW_WIDTH:2 * RW_WIDTH])
    g = _nn(_bf(_sigmoid(xg)), g_up_ref[...])
    kk = k * kk_ref[...]
    k = k * (1.0 + (a_sig - 1.0) * ka_ref[...])
    rkr = r * k * rk_ref[...]

    ti = lax.broadcasted_iota(jnp.int32, (rows_all, rows_all), 0)
    tj = lax.broadcasted_iota(jnp.int32, (rows_all, rows_all), 1)
    same_seq = (ti // tseg) == (tj // tseg)
    tri = jnp.where((tj <= ti) & same_seq, 1.0, 0.0).astype(BF16)
    lw3 = _split3(lw)
    cum = _nn(tri, lw3[0]) + _nn(tri, lw3[1]) + _nn(tri, lw3[2])
    if nseg == 1:
        ends = [jnp.broadcast_to(cum[(j + 1) * c - 1:(j + 1) * c, :], (c, RW_WIDTH)) for j in range(nb)]
        c_end = ends[0] if nb == 1 else jnp.concatenate(ends, axis=0)
    else:
        seq1 = jnp.where(same_seq, 1.0, 0.0).astype(BF16)
        c_end = _nn(seq1, lw3[0]) + _nn(seq1, lw3[1]) + _nn(seq1, lw3[2])
    e_c, e_ci, e_cm = jnp.exp(cum), jnp.exp(-cum), jnp.exp(cum - lw)
    e_ce, w_end = jnp.exp(c_end - cum), jnp.exp(c_end)

    gi = lax.broadcasted_iota(jnp.int32, (2 * LANES, LANES), 0) % LANES
    gj = lax.broadcasted_iota(jnp.int32, (2 * LANES, LANES), 1)
    ones2 = jnp.where((gi // RW_HEAD) == (gj // RW_HEAD), 1.0, 0.0).astype(BF16)
    bi_ = lax.broadcasted_iota(jnp.int32, (LANES, LANES), 0)
    bj_ = lax.broadcasted_iota(jnp.int32, (LANES, LANES), 1)
    same_head = (bi_ // RW_HEAD) == (bj_ // RW_HEAD)
    ci = lax.broadcasted_iota(jnp.int32, (c, 2 * c), 0)
    cj = lax.broadcasted_iota(jnp.int32, (c, 2 * c), 1)
    cjm = cj % c
    seq_ok = (ci // tseg) == (cjm // tseg)
    strict = (cjm < ci) & seq_ok
    incl = (cjm <= ci) & seq_ok
    eye_cat = jnp.where(cjm == ci, 1.0, 0.0)
    left = cj < c

    def seg_sum(x):
        hi, lo = _split2(x)
        return _nn(jnp.concatenate([hi, lo], axis=1), ones2)

    def rows2(x):
        return jnp.concatenate([jnp.where(head0, x, 0.0), jnp.where(head0, 0.0, x)], axis=0)

    def bd(cat):
        return _bf(jnp.concatenate([jnp.where(left, cat, 0.0), jnp.where(left, 0.0, cat)], axis=0))

    def pair_mm(cat, x):
        return _nn(_bf(cat), _bf(rows2(x)))

    def seg_sums(xs):
        tot = seg_sum(jnp.concatenate(xs, axis=0))
        return [tot[i * c:(i + 1) * c] for i in range(len(xs))]

    segs = range(nseg)
    rs = [slice(q * tseg, (q + 1) * tseg) for q in segs]

    def run(chains):
        gs = [(rsl[j], sls[p]) for j, p in chains]
        idx = range(len(chains))
        ss = seg_sums([kk[s] * kk[s] for s in gs])
        kkn = [kk[s] * lax.rsqrt(jnp.maximum(q, 1e-24)) for s, q in zip(gs, ss)]
        bv = [n_ * a_sig[s] for s, n_ in zip(gs, kkn)]
        rt = [r[s] * e_c[s] for s in gs]
        kt = [k[s] * e_ci[s] for s in gs]
        at = [-n_ * e_cm[s] for s, n_ in zip(gs, kkn)]
        bt = [b_ * e_ci[s] for s, b_ in zip(gs, bv)]
        bh = [b_ * e_ce[s] for s, b_ in zip(gs, bv)]
        kh = [k[s] * e_ce[s] for s in gs]
        vv = [v[s] for s in gs]

        ar = [_bf(jnp.concatenate([a_, r_], axis=0)) for a_, r_ in zip(at, rt)]
        xbk = [_nt(x, _bf(jnp.concatenate([rows2(b_), rows2(k_)], axis=0)))
               for x, b_, k_ in zip(ar, bt, kt)]
        l_ab = [jnp.where(strict, x[0:c, 0:2 * c], 0.0) for x in xbk]
        l_ak = [jnp.where(strict, x[0:c, 2 * c:4 * c], 0.0) for x in xbk]
        m_rb = [jnp.where(incl, x[c:2 * c, 0:2 * c], 0.0) for x in xbk]
        m_rk = [jnp.where(incl, x[c:2 * c, 2 * c:4 * c], 0.0) for x in xbk]

        t_inv = [l + eye_cat for l in l_ab]
        pw = l_ab
        pw_bd = [bd(x) for x in pw]
        for _ in range(int(math.log2(tseg)) - 1):
            pw = [_nn(_bf(x), xb_) for x, xb_ in zip(pw, pw_bd)]
            pw_bd = [bd(x) for x in pw]
            t_inv = [t + _nn(_bf(t), xb_) for t, xb_ in zip(t_inv, pw_bd)]

        lvy = [pair_mm(jnp.concatenate([l, m], axis=0), v_) for l, m, v_ in zip(l_ak, m_rk, vv)]
        y1 = [x[c:2 * c] for x in lvy]
        au = [_nn(_bf(t), _bf(jnp.concatenate([rows2(a_), rows2(x[0:c])], axis=1)))
              for t, a_, x in zip(t_inv, at, lvy)]
        a_hat = [x[:, 0:LANES] for x in au]
        u0 = [x[:, LANES:2 * LANES] for x in au]

        s_old = [[s_ref[j * nseg + q, p] for q in segs] for j, p in chains]
        pp = [[_nt(_bf(jnp.concatenate([a_hat[i][rs[q]], rt[i][rs[q]]], axis=0)), _bf(s_old[i][q]))
               for q in segs] for i in idx]
        u = [jnp.concatenate([pp[i][q][0:tseg] for q in segs], axis=0) + u0[i] for i in idx]
        y0 = [jnp.concatenate([pp[i][q][tseg:2 * tseg] for q in segs], axis=0) for i in idx]
        y = [y0[i] + pair_mm(m_rb[i], u[i]) + y1[i] for i in idx]
        for i in idx:
            for q in segs:
                upd = _tn(_bf(jnp.concatenate([u[i][rs[q]], vv[i][rs[q]]], axis=0)),
                          _bf(jnp.concatenate([bh[i][rs[q]], kh[i][rs[q]]], axis=0)))
                j, p = chains[i]
                row0 = (j * c + q * tseg) % w_end.shape[0]
                w_q = w_end[row0:row0 + 1, sls[p]]
                s_ref[j * nseg + q, p] = s_old[i][q] * w_q + jnp.where(same_head, upd, 0.0)

        mean = [x * (1.0 / RW_HEAD) for x in seg_sums(y)]
        d = [x - m for x, m in zip(y, mean)]
        var = [x * (1.0 / RW_HEAD) for x in seg_sums([x * x for x in d])]
        bonus = [x * v_ for x, v_ in zip(seg_sums([rkr[s] for s in gs]), vv)]
        for i in idx:
            j, p = chains[i]
            yn = d[i] * lax.rsqrt(var[i] + LNX_EPS) * lng_ref[:, sls[p]] + lnb_ref[:, sls[p]]
            o_ref[j, :, sls[p]] = ((yn + bonus[i]) * g[gs[i]]).astype(o_ref.dtype)

    run([(j, p) for j in range(nb) for p in pairs])

    @pl.when(step == pl.num_programs(1) - 1)
    def _():
        for q in range(nb * nseg):
            for p in pairs:
                tile = s_ref[q, p]
                sout_ref[q, 2 * p] = tile[0:RW_HEAD, 0:RW_HEAD]
                sout_ref[q, 2 * p + 1] = tile[RW_HEAD:2 * RW_HEAD, RW_HEAD:2 * RW_HEAD]


def _rwkv(zr, s0, shift0, lw, nb, c, nseg):
    b, t, _ = zr.shape
    if nseg == 1:
        ngrp, nchunk = b // nb, t // c
        z3 = zr
        shift = shift0.reshape(b, 1, RW_COLS)
        shift_spec = pl.BlockSpec((nb, 1, RW_COLS), lambda bi, i: (bi, 0, 0))
    else:
        assert c == nseg * t and nb == 1
        ngrp, nchunk = b // nseg, 1
        z3 = zr.reshape(ngrp, c, RW_COLS)
        shift = jnp.pad(shift0[:, None, :], ((0, 0), (0, t - 1), (0, 0))).reshape(ngrp, c, RW_COLS)
        shift_spec = pl.BlockSpec((nb, c, RW_COLS), lambda bi, i: (bi, 0, 0))
    vec = lambda name: lw[name].reshape(1, -1).astype(F32)
    cst = lambda bi, i: (0, 0)
    vspec = lambda wd: pl.BlockSpec((1, wd), cst)
    state_spec = pl.BlockSpec((nb * nseg, RW_HEADS, RW_HEAD, RW_HEAD), lambda bi, i: (bi, 0, 0, 0))
    o, s_new = pl.pallas_call(
        functools.partial(_rwkv_kernel, nb=nb, c=c, nseg=nseg),
        grid=(ngrp, nchunk),
        in_specs=[
            pl.BlockSpec((nb, c, RW_COLS), lambda bi, i: (bi, i, 0)),
            state_spec,
            shift_spec,
            vspec(RW_COLS), vspec(RW_WIDTH),
            pl.BlockSpec((LANES, 2 * RW_WIDTH), cst),
            vspec(RW_WIDTH),
            pl.BlockSpec((G_LORA, RW_WIDTH), cst),
            vspec(RW_WIDTH), vspec(RW_WIDTH), vspec(RW_WIDTH), vspec(RW_WIDTH), vspec(RW_WIDTH),
        ],
        out_specs=[
            pl.BlockSpec((nb, c, RW_WIDTH), lambda bi, i: (bi, i, 0)),
            state_spec,
        ],
        out_shape=[
            jax.ShapeDtypeStruct((ngrp * nb, nchunk * c, RW_WIDTH), BF16),
            jax.ShapeDtypeStruct((b, RW_HEADS, RW_HEAD, RW_HEAD), F32),
        ],
        scratch_shapes=[pltpu.VMEM((nb * nseg, RW_PAIRS, LANES, LANES), F32), pltpu.VMEM((nb, RW_COLS), F32)],
        compiler_params=_params(2),
    )(z3, s0, shift, vec("rw_mu"), vec("rw_w0"), lw["wa_up"], vec("rw_a0"),
      lw["g_up"], vec("rw_k_k"), vec("rw_k_a"), vec("rw_r_k"), vec("rw_lnx_g"), vec("rw_lnx_b"))
    return o.reshape(b, t, RW_WIDTH), s_new


def _mix_kernel(x_ref, oa_ref, or_ref, zg_ref, wba_ref, wbr_ref, wo_ref, g_ref, wr_ref, br_ref,
                x1_ref, hn_ref, comb_ref, cnt_ref):
    ya = _nn(_bf(oa_ref[...]), wba_ref[...])
    yr = _nn(_bf(or_ref[...]), wbr_ref[...])
    merged = _sigmoid(zg_ref[:, 0:D_MODEL]) * ya + _sigmoid(zg_ref[:, D_MODEL:2 * D_MODEL]) * yr
    x1 = x_ref[...] + _nn(_bf(merged), wo_ref[...])
    x1_ref[...] = x1
    ms = jnp.mean(x1 * x1, axis=-1, keepdims=True)
    hn = x1 * lax.rsqrt(ms + RMS_EPS) * g_ref[...]
    hn_ref[...] = _bf(hn)
    logits = _nn(_bf(hn), wr_ref[...]) + br_ref[...]
    lane = lax.broadcasted_iota(jnp.int32, logits.shape, 1).astype(F32)
    work = logits
    top = None
    for _ in range(TOP_K):
        m = jnp.max(work, axis=-1, keepdims=True)
        if top is None:
            top = m
        idx = jnp.min(jnp.where(work == m, lane, float(LANES)), axis=-1, keepdims=True)
        work = jnp.where(lane == idx, -jnp.inf, work)
    e = jnp.where(work != logits, jnp.exp(logits - top), 0.0)
    comb = e / jnp.sum(e, axis=-1, keepdims=True)
    comb_ref[...] = comb
    cnt_ref[...] = jnp.sum(jnp.where(comb > 0.0, 1.0, 0.0), axis=0, keepdims=True)


def _mix(x2d, oa, orw, zg, lw, tm):
    n = x2d.shape[0]
    row = lambda i: (i, 0)
    cst = lambda i: (0, 0)
    wspec = pl.BlockSpec((D_MODEL, D_MODEL), cst)
    return pl.pallas_call(
        _mix_kernel,
        grid=(n // tm,),
        in_specs=[
            pl.BlockSpec((tm, D_MODEL), row), pl.BlockSpec((tm, D_MODEL), row), pl.BlockSpec((tm, D_MODEL), row),
            pl.BlockSpec((tm, GATE_COLS), row),
            wspec, wspec, wspec,
            pl.BlockSpec((1, D_MODEL), cst),
            pl.BlockSpec((D_MODEL, LANES), cst),
            pl.BlockSpec((1, LANES), cst),
        ],
        out_specs=[pl.BlockSpec((tm, D_MODEL), row), pl.BlockSpec((tm, D_MODEL), row),
                   pl.BlockSpec((tm, LANES), row), pl.BlockSpec((None, 1, LANES), lambda i: (i, 0, 0))],
        out_shape=[jax.ShapeDtypeStruct((n, D_MODEL), F32), jax.ShapeDtypeStruct((n, D_MODEL), BF16),
                   jax.ShapeDtypeStruct((n, LANES), F32), jax.ShapeDtypeStruct((n // tm, 1, LANES), F32)],
        compiler_params=_params(1),
    )(x2d, oa, orw, zg, lw["w_ba"], lw["w_br"], lw["w_o"], lw["norm_ffn_g"], lw["w_r"], lw["b_r"])


_GU_GROUP = 2 * LANES


def _gu_regroup_kernel(w_ref, o_ref):
    src = lax.broadcasted_iota(jnp.int32, (_GU_GROUP, _GU_GROUP), 0)
    dst = lax.broadcasted_iota(jnp.int32, (_GU_GROUP, _GU_GROUP), 1)
    want = jnp.where(dst < LANES, 2 * dst, 2 * (dst - LANES) + 1)
    perm = jnp.where(src == want, 1.0, 0.0).astype(BF16)
    for j in range(0, 2 * D_EXPERT, _GU_GROUP):
        o_ref[:, j:j + _GU_GROUP] = _nn(_bf(w_ref[:, j:j + _GU_GROUP]), perm).astype(o_ref.dtype)


def _gu_regroup(w_gate_up):
    e, d, n = w_gate_up.shape
    tk = 512
    spec = pl.BlockSpec((None, tk, n), lambda i, j: (i, j, 0))
    return pl.pallas_call(
        _gu_regroup_kernel,
        grid=(e, d // tk),
        in_specs=[spec],
        out_specs=spec,
        out_shape=jax.ShapeDtypeStruct((e, d, n), BF16),
        compiler_params=_params(2),
    )(w_gate_up)


_MOE_DOMAIN = 1024
_MOE_ROWS = 160
_RANK_CHUNK = 256


def _moe_kernel(nsub_ref, hn_ref, comb_ref, wgu_ref, bgu_ref, wd_ref, bd_ref, out_ref,
                rank_ref, rank_t_ref, comb_t_ref, act_ref):
    blk, e = pl.program_id(0), pl.program_id(1)
    tb = hn_ref.shape[0]
    dom, rows = min(_MOE_DOMAIN, tb), _MOE_ROWS
    ndom = tb // dom

    @pl.when(e == 0)
    def _():
        out_ref[...] = jnp.zeros_like(out_ref)
        ci = lax.broadcasted_iota(jnp.int32, (_RANK_CHUNK, _RANK_CHUNK), 0)
        cj = lax.broadcasted_iota(jnp.int32, (_RANK_CHUNK, _RANK_CHUNK), 1)
        before = jnp.where(cj < ci, 1.0, 0.0).astype(BF16)
        for d0 in range(0, tb, dom):
            seen = jnp.zeros((1, LANES), F32)
            for c0 in range(d0, d0 + dom, _RANK_CHUNK):
                routed = comb_ref[c0:c0 + _RANK_CHUNK, :] > 0.0
                hot = jnp.where(routed, 1.0, 0.0)
                rank_ref[c0:c0 + _RANK_CHUNK, :] = jnp.where(routed, _nn(before, _bf(hot)) + seen, -1.0)
                seen = seen + jnp.sum(hot, axis=0, keepdims=True)
        rank_t_ref[...] = rank_ref[...].T
        comb_t_ref[...] = comb_ref[...].T

    slot_r = lax.broadcasted_iota(jnp.int32, (rows, dom), 0).astype(F32)

    for d in range(ndom):
        ds_ = slice(d * dom, (d + 1) * dom)
        rank_row = rank_t_ref[pl.ds(e, 1), ds_]
        w_row = comb_t_ref[pl.ds(e, 1), ds_]

        def sub_tile(s, carry, ds_=ds_, rank_row=rank_row, w_row=w_row):
            base = (s * rows).astype(F32)
            hit = rank_row - base == slot_r
            gather = jnp.where(hit, 1.0, 0.0).astype(BF16)
            x = _bf(_nn(gather, hn_ref[ds_, :]))
            w_rows = jnp.sum(jnp.where(hit, w_row, 0.0), axis=1, keepdims=True)
            for j in range(D_EXPERT // LANES):
                gs = slice(j * _GU_GROUP, (j + 1) * _GU_GROUP)
                gu = _nn(x, wgu_ref[:, gs]) + bgu_ref[:, gs]
                glu = jnp.minimum(gu[:, 0:LANES], SWIGLU_LIMIT)
                lin = jnp.clip(gu[:, LANES:_GU_GROUP], -SWIGLU_LIMIT, SWIGLU_LIMIT)
                act_ref[:, j * LANES:(j + 1) * LANES] = _bf(glu * _sigmoid(SWIGLU_ALPHA * glu) * (lin + 1.0))
            y = (_nn(act_ref[...], wd_ref[...]) + bd_ref[...]) * w_rows
            out_ref[ds_, :] += _tn(gather, _bf(y))
            return carry

        lax.fori_loop(0, nsub_ref[(blk * ndom + d) * N_EXPERTS + e], sub_tile, 0)


def _moe(hn, comb, tile_counts, lw, tb):
    n = hn.shape[0]
    nblk = n // tb
    ndomains = n // min(_MOE_DOMAIN, tb)
    counts = tile_counts.reshape(ndomains, -1, LANES).sum(axis=1)[:, :N_EXPERTS].astype(jnp.int32)
    nsub = ((counts + _MOE_ROWS - 1) // _MOE_ROWS).reshape(-1)
    row = lambda i, e, ns: (i, 0)
    ex = lambda i, e, ns: (e, 0, 0)
    return pl.pallas_call(
        _moe_kernel,
        grid_spec=pltpu.PrefetchScalarGridSpec(
            num_scalar_prefetch=1,
            grid=(nblk, N_EXPERTS),
            in_specs=[
                pl.BlockSpec((tb, D_MODEL), row), pl.BlockSpec((tb, LANES), row),
                pl.BlockSpec((None, D_MODEL, 2 * D_EXPERT), ex), pl.BlockSpec((None, 1, 2 * D_EXPERT), ex),
                pl.BlockSpec((None, D_EXPERT, D_MODEL), ex), pl.BlockSpec((None, 1, D_MODEL), ex),
            ],
            out_specs=pl.BlockSpec((tb, D_MODEL), row),
            scratch_shapes=[pltpu.VMEM((tb, LANES), F32), pltpu.VMEM((LANES, tb), F32),
                            pltpu.VMEM((LANES, tb), F32), pltpu.VMEM((_MOE_ROWS, D_EXPERT), BF16)],
        ),
        out_shape=jax.ShapeDtypeStruct((n, D_MODEL), F32),
        compiler_params=_params(2),
    )(nsub, hn, comb, lw["w_gu"], lw["b_gu"], lw["w_down"], lw["b_down"])


def _final_kernel(x1_ref, moe_ref, g_ref, y_ref):
    xo = x1_ref[...] + moe_ref[...]
    ms = jnp.mean(xo * xo, axis=-1, keepdims=True)
    y_ref[...] = xo * lax.rsqrt(ms + RMS_EPS) * g_ref[...]


def _final(x1, moe, gf, tm):
    n = x1.shape[0]
    row = pl.BlockSpec((tm, D_MODEL), lambda i: (i, 0))
    return pl.pallas_call(
        _final_kernel,
        grid=(n // tm,),
        in_specs=[row, row, pl.BlockSpec((1, D_MODEL), lambda i: (0, 0))],
        out_specs=row,
        out_shape=jax.ShapeDtypeStruct((n, D_MODEL), F32),
        compiler_params=_params(1),
    )(x1, moe, gf)


def _prep_layer(l, norm_mix_g, w_in, attn_sinks, rw_mu, rw_w0, rw_w_up, rw_a0, rw_a_up, rw_g_up,
                rw_k_k, rw_k_a, rw_r_k, rw_lnx_g, rw_lnx_b, w_branch_attn, w_branch_rwkv, w_out,
                norm_ffn_g, w_router, b_router, w_gate_up, b_gate_up, w_down, b_down):
    zeros = jnp.zeros((W_LORA, RW_WIDTH), F32)
    wa_up = jnp.concatenate([jnp.concatenate([rw_w_up[l], zeros], axis=1),
                             jnp.concatenate([zeros, rw_a_up[l]], axis=1)], axis=0)
    pad = LANES - N_EXPERTS
    return {
        "norm_mix_g": norm_mix_g[l].reshape(1, D_MODEL),
        "w_in": _bf(w_in[l]),
        "attn_sinks": attn_sinks[l].astype(F32),
        "rw_mu": rw_mu[l], "rw_w0": rw_w0[l], "rw_a0": rw_a0[l], "rw_k_k": rw_k_k[l], "rw_k_a": rw_k_a[l],
        "rw_r_k": rw_r_k[l], "rw_lnx_g": rw_lnx_g[l], "rw_lnx_b": rw_lnx_b[l],
        "wa_up": _bf(wa_up), "g_up": _bf(rw_g_up[l]),
        "w_ba": _bf(w_branch_attn[l]), "w_br": _bf(w_branch_rwkv[l]), "w_o": _bf(w_out[l]),
        "norm_ffn_g": norm_ffn_g[l].reshape(1, D_MODEL),
        "w_r": _bf(jnp.pad(w_router[l], ((0, 0), (0, pad)))),
        "b_r": jnp.pad(b_router[l], (0, pad), constant_values=-jnp.inf).reshape(1, LANES),
        "w_gu": _gu_regroup(w_gate_up[l]),
        "b_gu": b_gate_up[l].reshape(N_EXPERTS, D_EXPERT // LANES, LANES, 2).transpose(0, 1, 3, 2)
                .reshape(N_EXPERTS, 1, 2 * D_EXPERT),
        "w_down": _bf(w_down[l]), "b_down": b_down[l][:, None, :],
    }


def _tile(n, pref):
    tm = pref
    while n % tm:
        tm //= 2
    return tm


def _layer(x, tables, cache_k, cache_v, s0, shift0, lw, gf, rw_sets, rw_segs):
    b, t, _ = x.shape
    n = b * t
    x2d = x.reshape(n, D_MODEL)
    tm = _tile(n, 256)
    q, k, v, zr, zg = _inproj(x2d, lw["norm_mix_g"], lw["w_in"], tables, tm)
    q3, k3, v3 = q.reshape(b, t, ATT_Q), k.reshape(b, t, ATT_KV), v.reshape(b, t, ATT_KV)
    if cache_k is None:
        o_att, k_win, v_win = _attn_prompt(q3, k3, v3, lw["attn_sinks"])
    else:
        o_att, k_win, v_win = _attn_sample(q3, k3, v3, lw["attn_sinks"],
                                           cache_k.reshape(b, WINDOW, ATT_KV), cache_v.reshape(b, WINDOW, ATT_KV))
    zr3 = zr.reshape(b, t, RW_COLS)
    o_rw, s_new = _rwkv(zr3, s0, shift0, lw, rw_sets, RW_CHUNK, rw_segs)
    tm2 = _tile(n, 512)
    x1, hn, comb, tile_counts = _mix(x2d, o_att.reshape(n, ATT_Q), o_rw.reshape(n, RW_WIDTH), zg, lw, tm2)
    moe = _moe(hn, comb, tile_counts, lw, _tile(n, MOE_BLOCK))
    y = _final(x1, moe, gf, tm2)
    kv_shape = (b, WINDOW, N_KV_HEADS, HEAD_DIM)
    return (y.reshape(b, t, D_MODEL), k_win.reshape(kv_shape), v_win.reshape(kv_shape),
            s_new, zr3[:, t - 1, :])


def kernel(x_prompt, x_sample, cache_k, cache_v, state_wkv, state_shift, norm_mix_g, w_in, attn_sinks, rw_mu, rw_w0, rw_w_up, rw_a0, rw_a_up, rw_g_up, rw_k_k, rw_k_a, rw_r_k, rw_lnx_g, rw_lnx_b, w_branch_attn, w_branch_rwkv, w_out, norm_ffn_g, w_router, b_router, w_gate_up, b_gate_up, w_down, b_down, norm_final_g):
    assert w_in.shape[0] == 1, "single-layer trunk"
    bp, tp, _ = x_prompt.shape
    bs, ts, _ = x_sample.shape
    lw = _prep_layer(0, norm_mix_g, w_in, attn_sinks, rw_mu, rw_w0, rw_w_up, rw_a0, rw_a_up, rw_g_up,
                     rw_k_k, rw_k_a, rw_r_k, rw_lnx_g, rw_lnx_b, w_branch_attn, w_branch_rwkv, w_out,
                     norm_ffn_g, w_router, b_router, w_gate_up, b_gate_up, w_down, b_down)
    gf = norm_final_g.reshape(1, D_MODEL)

    tm_p = _tile(bp * tp, 256)
    tab_p = _rope_tables(jnp.arange(tp, dtype=jnp.int32))
    tm_s = _tile(bs * ts, 256)
    pos_s = PAST_LEN + jnp.arange(ts, dtype=jnp.int32)
    tab_s = tuple(jnp.tile(u, (tm_s // ts, 1)) for u in _rope_tables(pos_s))
    del tm_p

    s0p = jnp.zeros((bp, RW_HEADS, RW_HEAD, RW_HEAD), state_wkv.dtype)
    sh0p = jnp.zeros((bp, RW_COLS), state_shift.dtype)
    sets_p = RW_SEQS_PER_STEP if bp % RW_SEQS_PER_STEP == 0 else 1
    yp, kp, vp, sp, shp = _layer(x_prompt, tab_p, None, None, s0p, sh0p, lw, gf, sets_p, 1)
    ys, ks, vs, ss, shs = _layer(x_sample, tab_s, cache_k[0], cache_v[0], state_wkv[0], state_shift[0],
                                 lw, gf, 1, RW_CHUNK // ts)
    ex = lambda u: u[None]
    return (yp, ys, ex(kp), ex(vp), ex(sp), ex(shp), ex(ks), ex(vs), ex(ss), ex(shs))
```

```python
import functools
import math

import jax
import jax.numpy as jnp
from jax import lax
from jax.experimental import pallas as pl
from jax.experimental.pallas import tpu as pltpu

F32 = jnp.float32
BF16 = jnp.bfloat16

LANES = 128
SUBLANES = 8
VMEM_LIMIT_BYTES = 56 * 1024 * 1024

D_MODEL = 1024
HEAD_DIM = 64
N_Q_HEADS = 16
N_KV_HEADS = 4
Q_PER_KV = 4
WINDOW = 128
ROPE_THETA = 500000.0
ROPE_DIM = 16
ROPE_HALF = 8
ATTN_SCALE = HEAD_DIM ** -0.5
PAST_LEN = 16384
RW_HEAD = 64
RW_HEADS = 16
RW_PAIRS = RW_HEADS // 2
W_LORA = 64
A_LORA = 64
G_LORA = 128
LNX_EPS = 64e-5
N_EXPERTS = 32
TOP_K = 4
D_EXPERT = 1024
SWIGLU_LIMIT = 7.0
SWIGLU_ALPHA = 1.702
RMS_EPS = 1e-5
ATT_Q = N_Q_HEADS * HEAD_DIM
ATT_KV = N_KV_HEADS * HEAD_DIM
RW_WIDTH = RW_HEADS * RW_HEAD
RW_COLS = 3 * RW_WIDTH + W_LORA + A_LORA + G_LORA
GATE_COLS = 2 * D_MODEL
IN_COLS = ATT_Q + 2 * ATT_KV + RW_COLS + GATE_COLS
DECAY_SCALE = math.exp(-0.5)
RW_CHUNK = 64
IN_TILE = 512
MOE_BLOCK = 2048
RW_SEQS_PER_STEP = 4


def _nn(a, b):
    return jnp.dot(a, b, preferred_element_type=F32)


def _nt(a, b):
    return lax.dot_general(a, b, (((1,), (1,)), ((), ())), preferred_element_type=F32)


def _tn(a, b):
    return lax.dot_general(a, b, (((0,), (0,)), ((), ())), preferred_element_type=F32)


def _bf(x):
    return x.astype(BF16)


def _sigmoid(x):
    return 0.5 * jnp.tanh(0.5 * x) + 0.5


def _split2(x):
    hi = x.astype(BF16)
    lo = (x - hi.astype(F32)).astype(BF16)
    return hi, lo


def _split3(x):
    hi = x.astype(BF16)
    r1 = x - hi.astype(F32)
    mid = r1.astype(BF16)
    lo = (r1 - mid.astype(F32)).astype(BF16)
    return hi, mid, lo


def _params(n_axes):
    return pltpu.CompilerParams(
        dimension_semantics=("arbitrary",) * n_axes, vmem_limit_bytes=VMEM_LIMIT_BYTES)


_IN_CHUNK = 512


def _inproj_kernel(x_ref, g_ref, w_ref, cos_ref, sa_ref, sb_ref,
                   q_ref, k_ref, v_ref, zr_ref, zg_ref):
    x = x_ref[...]
    ms = jnp.mean(x * x, axis=-1, keepdims=True)
    h = _bf(x * lax.rsqrt(ms + RMS_EPS) * g_ref[...])
    cos, sa, sb = cos_ref[...], sa_ref[...], sb_ref[...]

    def rope(y):
        return (y * cos + pltpu.roll(y, LANES - ROPE_HALF, axis=1) * sa
                + pltpu.roll(y, ROPE_HALF, axis=1) * sb)

    def project(out_ref, col0, width, with_rope):
        for c in range(0, width, _IN_CHUNK):
            cw = min(_IN_CHUNK, width - c)
            acc = _nn(h, w_ref[:, col0 + c:col0 + c + cw])
            if with_rope:
                for j in range(0, cw, LANES):
                    out_ref[:, c + j:c + j + LANES] = rope(acc[:, j:j + LANES])
            else:
                out_ref[:, c:c + cw] = acc

    project(q_ref, 0, ATT_Q, True)
    project(k_ref, ATT_Q, ATT_KV, True)
    project(v_ref, ATT_Q + ATT_KV, ATT_KV, False)
    project(zr_ref, ATT_Q + 2 * ATT_KV, RW_COLS, False)
    project(zg_ref, ATT_Q + 2 * ATT_KV + RW_COLS, GATE_COLS, False)


def _rope_tables(positions):
    inv_freq = ROPE_THETA ** (-jnp.arange(ROPE_HALF, dtype=F32) / ROPE_HALF)
    ang = positions.astype(F32)[:, None] * inv_freq[None, :]
    cos, sin = jnp.cos(ang), jnp.sin(ang)
    t = positions.shape[0]
    one = jnp.ones((t, HEAD_DIM - ROPE_DIM), F32)
    zero = jnp.zeros((t, HEAD_DIM - ROPE_DIM), F32)
    z8 = jnp.zeros((t, ROPE_HALF), F32)
    cos_h = jnp.concatenate([cos, cos, one], axis=1)
    sa_h = jnp.concatenate([-sin, z8, zero], axis=1)
    sb_h = jnp.concatenate([z8, sin, zero], axis=1)
    two = lambda u: jnp.concatenate([u, u], axis=1)
    return two(cos_h), two(sa_h), two(sb_h)


def _inproj(x2d, g, w_bf, tables, tm):
    n = x2d.shape[0]
    cos, sa, sb = tables
    nper = cos.shape[0] // tm
    row = lambda i: (i, 0)
    tab = lambda i: (i % nper, 0)
    const = lambda i: (0, 0)
    out_shapes = [jax.ShapeDtypeStruct((n, w), F32) for w in (ATT_Q, ATT_KV, ATT_KV, RW_COLS, GATE_COLS)]
    return pl.pallas_call(
        _inproj_kernel,
        grid=(n // tm,),
        in_specs=[
            pl.BlockSpec((tm, D_MODEL), row),
            pl.BlockSpec((1, D_MODEL), const),
            pl.BlockSpec((D_MODEL, IN_COLS), const, pipeline_mode=pl.Buffered(1)),
            pl.BlockSpec((tm, LANES), tab),
            pl.BlockSpec((tm, LANES), tab),
            pl.BlockSpec((tm, LANES), tab),
        ],
        out_specs=[pl.BlockSpec((tm, w), row) for w in (ATT_Q, ATT_KV, ATT_KV, RW_COLS, GATE_COLS)],
        out_shape=out_shapes,
        compiler_params=_params(1),
    )(x2d, g, w_bf, cos, sa, sb)


def _attn_prompt_kernel(sink_ref, q_ref, kp_ref, kc_ref, vp_ref, vc_ref, o_ref, kw_ref, vw_ref):
    n = pl.program_id(1)
    w = WINDOW
    lane = lax.broadcasted_iota(jnp.int32, (w, LANES), 1)
    first_half = lane < HEAD_DIM
    qi = lax.broadcasted_iota(jnp.int32, (Q_PER_KV * w, 2 * w), 0) % w
    kj = lax.broadcasted_iota(jnp.int32, (Q_PER_KV * w, 2 * w), 1)
    first_key = jnp.where(n > 0, 0, w)
    band4 = (kj > qi) & (kj <= qi + w) & (kj >= first_key)

    kw_ref[...] = kc_ref[...]
    vw_ref[...] = vc_ref[...]

    lane2 = lax.broadcasted_iota(jnp.int32, (2 * w, LANES), 1)
    kvs = range(N_KV_HEADS)
    k2, v2 = [], []
    for kv in kvs:
        gs = slice((kv // 2) * LANES, (kv // 2 + 1) * LANES)
        mine = (lane2 < HEAD_DIM) if kv % 2 == 0 else (lane2 >= HEAD_DIM)
        k2.append(_bf(jnp.concatenate([kp_ref[:, gs], kc_ref[:, gs]], axis=0)))
        v2.append(_bf(jnp.where(mine, jnp.concatenate([vp_ref[:, gs], vc_ref[:, gs]], axis=0), 1.0)))
    qs, sink = [], []
    for kv in kvs:
        jh = kv % 2
        keep = first_half if jh == 0 else jnp.logical_not(first_half)
        rows, sinks = [], []
        for gq in range(Q_PER_KV):
            hq = kv * Q_PER_KV + gq
            qg = q_ref[:, (hq // 2) * LANES:(hq // 2 + 1) * LANES]
            if hq % 2 != jh:
                qg = pltpu.roll(qg, HEAD_DIM, axis=1)
            rows.append(jnp.where(keep, qg * ATTN_SCALE, 0.0))
            sinks.append(jnp.full((w, 1), sink_ref[hq], F32))
        qs.append(_bf(jnp.concatenate(rows, axis=0)))
        sink.append(jnp.concatenate(sinks, axis=0))
    s = [jnp.where(band4, _nt(q_, k_), -jnp.inf) for q_, k_ in zip(qs, k2)]
    m = [jnp.maximum(jnp.max(s_, axis=-1, keepdims=True), sk) for s_, sk in zip(s, sink)]
    e = [_bf(jnp.exp(s_ - m_)) for s_, m_ in zip(s, m)]
    e_sink = [jnp.exp(sk - m_) for sk, m_ in zip(sink, m)]
    pv = [_nn(e_, v_) for e_, v_ in zip(e, v2)]
    for kv in kvs:
        jh = kv % 2
        for go in range(Q_PER_KV // 2):
            ra, rb = slice(2 * go * w, (2 * go + 1) * w), slice((2 * go + 1) * w, (2 * go + 2) * w)
            pa, pb = pv[kv][ra], pv[kv][rb]
            pa_r, pb_r = pltpu.roll(pa, HEAD_DIM, axis=1), pltpu.roll(pb, HEAD_DIM, axis=1)
            if jh == 0:
                oa, ob = pa / (pa_r + e_sink[kv][ra]), pb_r / (pb + e_sink[kv][rb])
            else:
                oa, ob = pa_r / (pa + e_sink[kv][ra]), pb / (pb_r + e_sink[kv][rb])
            og = kv * (Q_PER_KV // 2) + go
            o_ref[:, og * LANES:(og + 1) * LANES] = jnp.where(first_half, oa, ob).astype(o_ref.dtype)


def _attn_prompt(q, k, v, sinks):
    b, t, _ = q.shape
    nb = t // WINDOW
    cur = lambda bi, n: (bi, n, 0)
    prev = lambda bi, n: (bi, jnp.maximum(n - 1, 0), 0)
    win = lambda bi, n: (bi, 0, 0)
    return pl.pallas_call(
        _attn_prompt_kernel,
        grid=(b, nb),
        in_specs=[
            pl.BlockSpec(memory_space=pltpu.SMEM),
            pl.BlockSpec((None, WINDOW, ATT_Q), cur),
            pl.BlockSpec((None, WINDOW, ATT_KV), prev),
            pl.BlockSpec((None, WINDOW, ATT_KV), cur),
            pl.BlockSpec((None, WINDOW, ATT_KV), prev),
            pl.BlockSpec((None, WINDOW, ATT_KV), cur),
        ],
        out_specs=[
            pl.BlockSpec((None, WINDOW, ATT_Q), cur),
            pl.BlockSpec((None, WINDOW, ATT_KV), win),
            pl.BlockSpec((None, WINDOW, ATT_KV), win),
        ],
        out_shape=[
            jax.ShapeDtypeStruct((b, t, ATT_Q), BF16),
            jax.ShapeDtypeStruct((b, WINDOW, ATT_KV), F32),
            jax.ShapeDtypeStruct((b, WINDOW, ATT_KV), F32),
        ],
        compiler_params=_params(2),
    )(sinks, q, k, k, v, v)


_SAMPLE_BT = 8


def _attn_sample_kernel(sink_ref, q_ref, k_ref, v_ref, ck_ref, cv_ref, o_ref, nk_ref, nv_ref, *, t):
    w = WINDOW
    rows_per_grp = 2 * Q_PER_KV * t
    lane = lax.broadcasted_iota(jnp.int32, (t, LANES), 1)
    first_half = lane < HEAD_DIM
    r_c = lax.broadcasted_iota(jnp.int32, (rows_per_grp, w), 0) % t
    c_c = lax.broadcasted_iota(jnp.int32, (rows_per_grp, w), 1)
    mask_c = c_c > r_c
    r_n = lax.broadcasted_iota(jnp.int32, (rows_per_grp, t), 0) % t
    c_n = lax.broadcasted_iota(jnp.int32, (rows_per_grp, t), 1)
    mask_n = c_n <= r_n

    for bi in range(_SAMPLE_BT):
        nk_ref[bi, 0:w - t, :] = ck_ref[bi, t:w, :]
        nk_ref[bi, w - t:w, :] = k_ref[bi]
        nv_ref[bi, 0:w - t, :] = cv_ref[bi, t:w, :]
        nv_ref[bi, w - t:w, :] = v_ref[bi]

    chains = [(bi, grp) for bi in range(_SAMPLE_BT) for grp in range(ATT_KV // LANES)]
    gsl = lambda grp: slice(grp * LANES, (grp + 1) * LANES)
    kc = [_bf(ck_ref[bi, :, gsl(grp)]) for bi, grp in chains]
    vc = [_bf(cv_ref[bi, :, gsl(grp)]) for bi, grp in chains]
    kn = [_bf(k_ref[bi, :, gsl(grp)]) for bi, grp in chains]
    vn = [_bf(v_ref[bi, :, gsl(grp)]) for bi, grp in chains]
    sink_rows = [[jnp.full((t, 1), sink_ref[(grp * 2 + jh) * Q_PER_KV + gq], F32)
                  for jh in range(2) for gq in range(Q_PER_KV)] for grp in range(ATT_KV // LANES)]
    sink_g = [jnp.concatenate(r_, axis=0) for r_ in sink_rows]
    qs = []
    for bi, grp in chains:
        rows = []
        for jh in range(2):
            keep = first_half if jh == 0 else jnp.logical_not(first_half)
            for gq in range(Q_PER_KV):
                hq = (grp * 2 + jh) * Q_PER_KV + gq
                qg = q_ref[bi, :, (hq // 2) * LANES:(hq // 2 + 1) * LANES]
                if hq % 2 != jh:
                    qg = pltpu.roll(qg, HEAD_DIM, axis=1)
                rows.append(jnp.where(keep, qg * ATTN_SCALE, 0.0))
        qs.append(_bf(jnp.concatenate(rows, axis=0)))
    sink = [sink_g[grp] for _, grp in chains]
    s_c = [jnp.where(mask_c, _nt(q_, k_), -jnp.inf) for q_, k_ in zip(qs, kc)]
    s_n = [jnp.where(mask_n, _nt(q_, k_), -jnp.inf) for q_, k_ in zip(qs, kn)]
    m = [jnp.maximum(jnp.maximum(jnp.max(a, axis=-1, keepdims=True), jnp.max(b_, axis=-1, keepdims=True)), sk)
         for a, b_, sk in zip(s_c, s_n, sink)]
    e_c = [jnp.exp(a - m_) for a, m_ in zip(s_c, m)]
    e_n = [jnp.exp(a - m_) for a, m_ in zip(s_n, m)]
    denom = [jnp.sum(a, axis=-1, keepdims=True) + jnp.sum(b_, axis=-1, keepdims=True) + jnp.exp(sk - m_)
             for a, b_, sk, m_ in zip(e_c, e_n, sink, m)]
    pv = [_nn(_bf(a / d_), va) + _nn(_bf(b_ / d_), vb)
          for a, b_, d_, va, vb in zip(e_c, e_n, denom, vc, vn)]
    for (bi, grp), pv_ in zip(chains, pv):
        for jh in range(2):
            for go in range(Q_PER_KV // 2):
                r0 = (jh * Q_PER_KV + 2 * go) * t
                pa, pb = pv_[r0:r0 + t], pv_[r0 + t:r0 + 2 * t]
                if jh == 0:
                    pb = pltpu.roll(pb, HEAD_DIM, axis=1)
                else:
                    pa = pltpu.roll(pa, HEAD_DIM, axis=1)
                og = (grp * 2 + jh) * (Q_PER_KV // 2) + go
                o_ref[bi, :, og * LANES:(og + 1) * LANES] = jnp.where(first_half, pa, pb)


def _attn_sample(q, k, v, sinks, cache_k, cache_v):
    b, t, _ = q.shape
    bt = _SAMPLE_BT
    blk = lambda i: (i, 0, 0)
    return pl.pallas_call(
        functools.partial(_attn_sample_kernel, t=t),
        grid=(b // bt,),
        in_specs=[
            pl.BlockSpec(memory_space=pltpu.SMEM),
            pl.BlockSpec((bt, t, ATT_Q), blk),
            pl.BlockSpec((bt, t, ATT_KV), blk),
            pl.BlockSpec((bt, t, ATT_KV), blk),
            pl.BlockSpec((bt, WINDOW, ATT_KV), blk),
            pl.BlockSpec((bt, WINDOW, ATT_KV), blk),
        ],
        out_specs=[
            pl.BlockSpec((bt, t, ATT_Q), blk),
            pl.BlockSpec((bt, WINDOW, ATT_KV), blk),
            pl.BlockSpec((bt, WINDOW, ATT_KV), blk),
        ],
        out_shape=[
            jax.ShapeDtypeStruct((b, t, ATT_Q), F32),
            jax.ShapeDtypeStruct((b, WINDOW, ATT_KV), F32),
            jax.ShapeDtypeStruct((b, WINDOW, ATT_KV), F32),
        ],
        compiler_params=_params(1),
    )(sinks, q, k, v, cache_k, cache_v)


def _rwkv_kernel(z_ref, s0_ref, shift_ref, mu_ref, w0_ref, wa_up_ref, a0_ref, g_up_ref,
                 kk_ref, ka_ref, rk_ref, lng_ref, lnb_ref, o_ref, sout_ref, s_ref, prev_ref, *, nb, c, nseg):
    tseg = c // nseg
    rows_all = nb * c
    step = pl.program_id(1)
    pairs = range(RW_PAIRS)
    sls = [slice(p * LANES, (p + 1) * LANES) for p in pairs]
    rsl = [slice(j * c, (j + 1) * c) for j in range(nb)]

    @pl.when(step == 0)
    def _():
        zero = jnp.zeros((RW_HEAD, RW_HEAD), F32)
        for q in range(nb * nseg):
            for p in pairs:
                top = jnp.concatenate([s0_ref[q, 2 * p], zero], axis=1)
                bot = jnp.concatenate([zero, s0_ref[q, 2 * p + 1]], axis=1)
                s_ref[q, p] = jnp.concatenate([top, bot], axis=0)
        if nseg == 1:
            prev_ref[...] = shift_ref[...].reshape(nb, RW_COLS)

    z = z_ref[...].reshape(rows_all, RW_COLS)
    row1 = lax.broadcasted_iota(jnp.int32, (rows_all, 1), 0)
    zprev = pltpu.roll(z, 1, axis=0)
    if nseg == 1:
        for j in range(nb):
            zprev = jnp.where(row1 == j * c, prev_ref[j:j + 1, :], zprev)
            prev_ref[j:j + 1, :] = z[(j + 1) * c - 1:(j + 1) * c, :]
    else:
        zprev = jnp.where(row1 % tseg == 0, shift_ref[...].reshape(rows_all, RW_COLS), zprev)
    zs = z + (zprev - z) * mu_ref[...]

    w3 = 3 * RW_WIDTH
    r, k, v = zs[:, 0:RW_WIDTH], zs[:, RW_WIDTH:2 * RW_WIDTH], zs[:, 2 * RW_WIDTH:w3]
    xwa = zs[:, w3:w3 + LANES]
    xg = zs[:, w3 + LANES:w3 + 2 * LANES]
    lane_all = lax.broadcasted_iota(jnp.int32, (rows_all, LANES), 1)
    lora = _nn(_bf(jnp.where(lane_all < RW_HEAD, jnp.tanh(xwa), xwa)), wa_up_ref[...])
    lane = lax.broadcasted_iota(jnp.int32, (c, LANES), 1)
    head0 = lane < RW_HEAD
    lw = -DECAY_SCALE * _sigmoid(w0_ref[...] + lora[:, 0:RW_WIDTH])
    a_sig = _sigmoid(a0_ref[...] + lora[:, RW_WIDTH:2 * RW_WIDTH])
    g = _nn(_bf(_sigmoid(xg)), g_up_ref[...])
    kk = k * kk_ref[...]
    k = k * (1.0 + (a_sig - 1.0) * ka_ref[...])
    rkr = r * k * rk_ref[...]

    ti = lax.broadcasted_iota(jnp.int32, (rows_all, rows_all), 0)
    tj = lax.broadcasted_iota(jnp.int32, (rows_all, rows_all), 1)
    same_seq = (ti // tseg) == (tj // tseg)
    tri = jnp.where((tj <= ti) & same_seq, 1.0, 0.0).astype(BF16)
    lw3 = _split3(lw)
    cum = _nn(tri, lw3[0]) + _nn(tri, lw3[1]) + _nn(tri, lw3[2])
    if nseg == 1:
        ends = [jnp.broadcast_to(cum[(j + 1) * c - 1:(j + 1) * c, :], (c, RW_WIDTH)) for j in range(nb)]
        c_end = ends[0] if nb == 1 else jnp.concatenate(ends, axis=0)
    else:
        seq1 = jnp.where(same_seq, 1.0, 0.0).astype(BF16)
        c_end = _nn(seq1, lw3[0]) + _nn(seq1, lw3[1]) + _nn(seq1, lw3[2])
    e_c, e_ci, e_cm = jnp.exp(cum), jnp.exp(-cum), jnp.exp(cum - lw)
    e_ce, w_end = jnp.exp(c_end - cum), jnp.exp(c_end)

    gi = lax.broadcasted_iota(jnp.int32, (2 * LANES, LANES), 0) % LANES
    gj = lax.broadcasted_iota(jnp.int32, (2 * LANES, LANES), 1)
    ones2 = jnp.where((gi // RW_HEAD) == (gj // RW_HEAD), 1.0, 0.0).astype(BF16)
    bi_ = lax.broadcasted_iota(jnp.int32, (LANES, LANES), 0)
    bj_ = lax.broadcasted_iota(jnp.int32, (LANES, LANES), 1)
    same_head = (bi_ // RW_HEAD) == (bj_ // RW_HEAD)
    ci = lax.broadcasted_iota(jnp.int32, (c, 2 * c), 0)
    cj = lax.broadcasted_iota(jnp.int32, (c, 2 * c), 1)
    cjm = cj % c
    seq_ok = (ci // tseg) == (cjm // tseg)
    strict = (cjm < ci) & seq_ok
    incl = (cjm <= ci) & seq_ok
    eye_cat = jnp.where(cjm == ci, 1.0, 0.0)
    left = cj < c

    def seg_sum(x):
        hi, lo = _split2(x)
        return _nn(jnp.concatenate([hi, lo], axis=1), ones2)

    def rows2(x):
        return jnp.concatenate([jnp.where(head0, x, 0.0), jnp.where(head0, 0.0, x)], axis=0)

    def bd(cat):
        return _bf(jnp.concatenate([jnp.where(left, cat, 0.0), jnp.where(left, 0.0, cat)], axis=0))

    def pair_mm(cat, x):
        return _nn(_bf(cat), _bf(rows2(x)))

    def seg_sums(xs):
        tot = seg_sum(jnp.concatenate(xs, axis=0))
        return [tot[i * c:(i + 1) * c] for i in range(len(xs))]

    segs = range(nseg)
    rs = [slice(q * tseg, (q + 1) * tseg) for q in segs]

    def run(chains):
        gs = [(rsl[j], sls[p]) for j, p in chains]
        idx = range(len(chains))
        ss = seg_sums([kk[s] * kk[s] for s in gs])
        kkn = [kk[s] * lax.rsqrt(jnp.maximum(q, 1e-24)) for s, q in zip(gs, ss)]
        bv = [n_ * a_sig[s] for s, n_ in zip(gs, kkn)]
        rt = [r[s] * e_c[s] for s in gs]
        kt = [k[s] * e_ci[s] for s in gs]
        at = [-n_ * e_cm[s] for s, n_ in zip(gs, kkn)]
        bt = [b_ * e_ci[s] for s, b_ in zip(gs, bv)]
        bh = [b_ * e_ce[s] for s, b_ in zip(gs, bv)]
        kh = [k[s] * e_ce[s] for s in gs]
        vv = [v[s] for s in gs]

        ar = [_bf(jnp.concatenate([a_, r_], axis=0)) for a_, r_ in zip(at, rt)]
        xbk = [_nt(x, _bf(jnp.concatenate([rows2(b_), rows2(k_)], axis=0)))
               for x, b_, k_ in zip(ar, bt, kt)]
        l_ab = [jnp.where(strict, x[0:c, 0:2 * c], 0.0) for x in xbk]
        l_ak = [jnp.where(strict, x[0:c, 2 * c:4 * c], 0.0) for x in xbk]
        m_rb = [jnp.where(incl, x[c:2 * c, 0:2 * c], 0.0) for x in xbk]
        m_rk = [jnp.where(incl, x[c:2 * c, 2 * c:4 * c], 0.0) for x in xbk]

        t_inv = [l + eye_cat for l in l_ab]
        pw = l_ab
        pw_bd = [bd(x) for x in pw]
        for _ in range(int(math.log2(tseg)) - 1):
            pw = [_nn(_bf(x), xb_) for x, xb_ in zip(pw, pw_bd)]
            pw_bd = [bd(x) for x in pw]
            t_inv = [t + _nn(_bf(t), xb_) for t, xb_ in zip(t_inv, pw_bd)]

        lvy = [pair_mm(jnp.concatenate([l, m], axis=0), v_) for l, m, v_ in zip(l_ak, m_rk, vv)]
        y1 = [x[c:2 * c] for x in lvy]
        au = [_nn(_bf(t), _bf(jnp.concatenate([rows2(a_), rows2(x[0:c])], axis=1)))
              for t, a_, x in zip(t_inv, at, lvy)]
        a_hat = [x[:, 0:LANES] for x in au]
        u0 = [x[:, LANES:2 * LANES] for x in au]

        s_old = [[s_ref[j * nseg + q, p] for q in segs] for j, p in chains]
        pp = [[_nt(_bf(jnp.concatenate([a_hat[i][rs[q]], rt[i][rs[q]]], axis=0)), _bf(s_old[i][q]))
               for q in segs] for i in idx]
        u = [jnp.concatenate([pp[i][q][0:tseg] for q in segs], axis=0) + u0[i] for i in idx]
        y0 = [jnp.concatenate([pp[i][q][tseg:2 * tseg] for q in segs], axis=0) for i in idx]
        y = [y0[i] + pair_mm(m_rb[i], u[i]) + y1[i] for i in idx]
        for i in idx:
            for q in segs:
                upd = _tn(_bf(jnp.concatenate([u[i][rs[q]], vv[i][rs[q]]], axis=0)),
                          _bf(jnp.concatenate([bh[i][rs[q]], kh[i][rs[q]]], axis=0)))
                j, p = chains[i]
                row0 = (j * c + q * tseg) % w_end.shape[0]
                w_q = w_end[row0:row0 + 1, sls[p]]
                s_ref[j * nseg + q, p] = s_old[i][q] * w_q + jnp.where(same_head, upd, 0.0)

        mean = [x * (1.0 / RW_HEAD) for x in seg_sums(y)]
        d = [x - m for x, m in zip(y, mean)]
        var = [x * (1.0 / RW_HEAD) for x in seg_sums([x * x for x in d])]
        bonus = [x * v_ for x, v_ in zip(seg_sums([rkr[s] for s in gs]), vv)]
        for i in idx:
            j, p = chains[i]
            yn = d[i] * lax.rsqrt(var[i] + LNX_EPS) * lng_ref[:, sls[p]] + lnb_ref[:, sls[p]]
            o_ref[j, :, sls[p]] = ((yn + bonus[i]) * g[gs[i]]).astype(o_ref.dtype)

    run([(j, p) for j in range(nb) for p in pairs])

    @pl.when(step == pl.num_programs(1) - 1)
    def _():
        for q in range(nb * nseg):
            for p in pairs:
                tile = s_ref[q, p]
                sout_ref[q, 2 * p] = tile[0:RW_HEAD, 0:RW_HEAD]
                sout_ref[q, 2 * p + 1] = tile[RW_HEAD:2 * RW_HEAD, RW_HEAD:2 * RW_HEAD]


def _rwkv(zr, s0, shift0, lw, nb, c, nseg):
    b, t, _ = zr.shape
    if nseg == 1:
        ngrp, nchunk = b // nb, t // c
        z3 = zr
        shift = shift0.reshape(b, 1, RW_COLS)
        shift_spec = pl.BlockSpec((nb, 1, RW_COLS), lambda bi, i: (bi, 0, 0))
    else:
        assert c == nseg * t
        nsets = b // nseg
        ngrp, nchunk = nsets // nb, 1
        z3 = zr.reshape(nsets, c, RW_COLS)
        shift = jnp.pad(shift0[:, None, :], ((0, 0), (0, t - 1), (0, 0))).reshape(nsets, c, RW_COLS)
        shift_spec = pl.BlockSpec((nb, c, RW_COLS), lambda bi, i: (bi, 0, 0))
    vec = lambda name: lw[name].reshape(1, -1).astype(F32)
    cst = lambda bi, i: (0, 0)
    vspec = lambda wd: pl.BlockSpec((1, wd), cst)
    state_spec = pl.BlockSpec((nb * nseg, RW_HEADS, RW_HEAD, RW_HEAD), lambda bi, i: (bi, 0, 0, 0))
    o, s_new = pl.pallas_call(
        functools.partial(_rwkv_kernel, nb=nb, c=c, nseg=nseg),
        grid=(ngrp, nchunk),
        in_specs=[
            pl.BlockSpec((nb, c, RW_COLS), lambda bi, i: (bi, i, 0)),
            state_spec,
            shift_spec,
            vspec(RW_COLS), vspec(RW_WIDTH),
            pl.BlockSpec((LANES, 2 * RW_WIDTH), cst),
            vspec(RW_WIDTH),
            pl.BlockSpec((G_LORA, RW_WIDTH), cst),
            vspec(RW_WIDTH), vspec(RW_WIDTH), vspec(RW_WIDTH), vspec(RW_WIDTH), vspec(RW_WIDTH),
        ],
        out_specs=[
            pl.BlockSpec((nb, c, RW_WIDTH), lambda bi, i: (bi, i, 0)),
            state_spec,
        ],
        out_shape=[
            jax.ShapeDtypeStruct((ngrp * nb, nchunk * c, RW_WIDTH), BF16),
            jax.ShapeDtypeStruct((b, RW_HEADS, RW_HEAD, RW_HEAD), F32),
        ],
        scratch_shapes=[pltpu.VMEM((nb * nseg, RW_PAIRS, LANES, LANES), F32), pltpu.VMEM((nb, RW_COLS), F32)],
        compiler_params=_params(2),
    )(z3, s0, shift, vec("rw_mu"), vec("rw_w0"), lw["wa_up"], vec("rw_a0"),
      lw["g_up"], vec("rw_k_k"), vec("rw_k_a"), vec("rw_r_k"), vec("rw_lnx_g"), vec("rw_lnx_b"))
    return o.reshape(b, t, RW_WIDTH), s_new


def _mix_kernel(x_ref, oa_ref, or_ref, zg_ref, wba_ref, wbr_ref, wo_ref, g_ref, wr_ref, br_ref,
                x1_ref, hn_ref, comb_ref, cnt_ref):
    ya = _nn(_bf(oa_ref[...]), wba_ref[...])
    yr = _nn(_bf(or_ref[...]), wbr_ref[...])
    merged = _sigmoid(zg_ref[:, 0:D_MODEL]) * ya + _sigmoid(zg_ref[:, D_MODEL:2 * D_MODEL]) * yr
    x1 = x_ref[...] + _nn(_bf(merged), wo_ref[...])
    x1_ref[...] = x1
    ms = jnp.mean(x1 * x1, axis=-1, keepdims=True)
    hn = x1 * lax.rsqrt(ms + RMS_EPS) * g_ref[...]
    hn_ref[...] = _bf(hn)
    logits = _nn(_bf(hn), wr_ref[...]) + br_ref[...]
    lane = lax.broadcasted_iota(jnp.int32, logits.shape, 1).astype(F32)
    work = logits
    top = None
    for _ in range(TOP_K):
        m = jnp.max(work, axis=-1, keepdims=True)
        if top is None:
            top = m
        idx = jnp.min(jnp.where(work == m, lane, float(LANES)), axis=-1, keepdims=True)
        work = jnp.where(lane == idx, -jnp.inf, work)
    e = jnp.where(work != logits, jnp.exp(logits - top), 0.0)
    comb = e / jnp.sum(e, axis=-1, keepdims=True)
    comb_ref[...] = comb
    cnt_ref[...] = jnp.sum(jnp.where(comb > 0.0, 1.0, 0.0), axis=0, keepdims=True)


def _mix(x2d, oa, orw, zg, lw, tm):
    n = x2d.shape[0]
    row = lambda i: (i, 0)
    cst = lambda i: (0, 0)
    wspec = pl.BlockSpec((D_MODEL, D_MODEL), cst)
    return pl.pallas_call(
        _mix_kernel,
        grid=(n // tm,),
        in_specs=[
            pl.BlockSpec((tm, D_MODEL), row), pl.BlockSpec((tm, D_MODEL), row), pl.BlockSpec((tm, D_MODEL), row),
            pl.BlockSpec((tm, GATE_COLS), row),
            wspec, wspec, wspec,
            pl.BlockSpec((1, D_MODEL), cst),
            pl.BlockSpec((D_MODEL, LANES), cst),
            pl.BlockSpec((1, LANES), cst),
        ],
        out_specs=[pl.BlockSpec((tm, D_MODEL), row), pl.BlockSpec((tm, D_MODEL), row),
                   pl.BlockSpec((tm, LANES), row), pl.BlockSpec((None, 1, LANES), lambda i: (i, 0, 0))],
        out_shape=[jax.ShapeDtypeStruct((n, D_MODEL), F32), jax.ShapeDtypeStruct((n, D_MODEL), BF16),
                   jax.ShapeDtypeStruct((n, LANES), F32), jax.ShapeDtypeStruct((n // tm, 1, LANES), F32)],
        compiler_params=_params(1),
    )(x2d, oa, orw, zg, lw["w_ba"], lw["w_br"], lw["w_o"], lw["norm_ffn_g"], lw["w_r"], lw["b_r"])


_GU_GROUP = 2 * LANES


def _gu_regroup_kernel(w_ref, o_ref):
    src = lax.broadcasted_iota(jnp.int32, (_GU_GROUP, _GU_GROUP), 0)
    dst = lax.broadcasted_iota(jnp.int32, (_GU_GROUP, _GU_GROUP), 1)
    want = jnp.where(dst < LANES, 2 * dst, 2 * (dst - LANES) + 1)
    perm = jnp.where(src == want, 1.0, 0.0).astype(BF16)
    for j in range(0, 2 * D_EXPERT, _GU_GROUP):
        o_ref[:, j:j + _GU_GROUP] = _nn(_bf(w_ref[:, j:j + _GU_GROUP]), perm).astype(o_ref.dtype)


def _gu_regroup(w_gate_up):
    e, d, n = w_gate_up.shape
    tk = 512
    spec = pl.BlockSpec((None, tk, n), lambda i, j: (i, j, 0))
    return pl.pallas_call(
        _gu_regroup_kernel,
        grid=(e, d // tk),
        in_specs=[spec],
        out_specs=spec,
        out_shape=jax.ShapeDtypeStruct((e, d, n), BF16),
        compiler_params=_params(2),
    )(w_gate_up)


_MOE_DOMAIN = 1024
_MOE_ROWS = 160
_RANK_CHUNK = 256


def _moe_kernel(nsub_ref, hn_ref, comb_ref, wgu_ref, bgu_ref, wd_ref, bd_ref, out_ref,
                rank_ref, rank_t_ref, comb_t_ref, act_ref):
    blk, e = pl.program_id(0), pl.program_id(1)
    tb = hn_ref.shape[0]
    dom, rows = min(_MOE_DOMAIN, tb), _MOE_ROWS
    ndom = tb // dom

    @pl.when(e == 0)
    def _():
        out_ref[...] = jnp.zeros_like(out_ref)
        ci = lax.broadcasted_iota(jnp.int32, (_RANK_CHUNK, _RANK_CHUNK), 0)
        cj = lax.broadcasted_iota(jnp.int32, (_RANK_CHUNK, _RANK_CHUNK), 1)
        before = jnp.where(cj < ci, 1.0, 0.0).astype(BF16)
        for d0 in range(0, tb, dom):
            seen = jnp.zeros((1, LANES), F32)
            for c0 in range(d0, d0 + dom, _RANK_CHUNK):
                routed = comb_ref[c0:c0 + _RANK_CHUNK, :] > 0.0
                hot = jnp.where(routed, 1.0, 0.0)
                rank_ref[c0:c0 + _RANK_CHUNK, :] = jnp.where(routed, _nn(before, _bf(hot)) + seen, -1.0)
                seen = seen + jnp.sum(hot, axis=0, keepdims=True)
        rank_t_ref[...] = rank_ref[...].T
        comb_t_ref[...] = comb_ref[...].T

    slot_r = lax.broadcasted_iota(jnp.int32, (rows, dom), 0).astype(F32)

    for d in range(ndom):
        ds_ = slice(d * dom, (d + 1) * dom)
        rank_row = rank_t_ref[pl.ds(e, 1), ds_]
        w_row = comb_t_ref[pl.ds(e, 1), ds_]

        def sub_tile(s, carry, ds_=ds_, rank_row=rank_row, w_row=w_row):
            base = (s * rows).astype(F32)
            hit = rank_row - base == slot_r
            gather = jnp.where(hit, 1.0, 0.0).astype(BF16)
            x = _bf(_nn(gather, hn_ref[ds_, :]))
            w_rows = jnp.sum(jnp.where(hit, w_row, 0.0), axis=1, keepdims=True)
            for j in range(D_EXPERT // LANES):
                gs = slice(j * _GU_GROUP, (j + 1) * _GU_GROUP)
                gu = _nn(x, wgu_ref[:, gs]) + bgu_ref[:, gs]
                glu = jnp.minimum(gu[:, 0:LANES], SWIGLU_LIMIT)
                lin = jnp.clip(gu[:, LANES:_GU_GROUP], -SWIGLU_LIMIT, SWIGLU_LIMIT)
                act_ref[:, j * LANES:(j + 1) * LANES] = _bf(glu * _sigmoid(SWIGLU_ALPHA * glu) * (lin + 1.0))
            y = (_nn(act_ref[...], _bf(wd_ref[...])) + bd_ref[...]) * w_rows
            out_ref[ds_, :] += _tn(gather, _bf(y))
            return carry

        lax.fori_loop(0, nsub_ref[(blk * ndom + d) * N_EXPERTS + e], sub_tile, 0)


def _moe(hn, comb, tile_counts, lw, tb):
    n = hn.shape[0]
    nblk = n // tb
    ndomains = n // min(_MOE_DOMAIN, tb)
    counts = tile_counts.reshape(ndomains, -1, LANES).sum(axis=1)[:, :N_EXPERTS].astype(jnp.int32)
    nsub = ((counts + _MOE_ROWS - 1) // _MOE_ROWS).reshape(-1)
    row = lambda i, e, ns: (i, 0)
    ex = lambda i, e, ns: (e, 0, 0)
    return pl.pallas_call(
        _moe_kernel,
        grid_spec=pltpu.PrefetchScalarGridSpec(
            num_scalar_prefetch=1,
            grid=(nblk, N_EXPERTS),
            in_specs=[
                pl.BlockSpec((tb, D_MODEL), row), pl.BlockSpec((tb, LANES), row),
                pl.BlockSpec((None, D_MODEL, 2 * D_EXPERT), ex), pl.BlockSpec((None, 1, 2 * D_EXPERT), ex),
                pl.BlockSpec((None, D_EXPERT, D_MODEL), ex), pl.BlockSpec((None, 1, D_MODEL), ex),
            ],
            out_specs=pl.BlockSpec((tb, D_MODEL), row),
            scratch_shapes=[pltpu.VMEM((tb, LANES), F32), pltpu.VMEM((LANES, tb), F32),
                            pltpu.VMEM((LANES, tb), F32), pltpu.VMEM((_MOE_ROWS, D_EXPERT), BF16)],
        ),
        out_shape=jax.ShapeDtypeStruct((n, D_MODEL), F32),
        compiler_params=_params(2),
    )(nsub, hn, comb, lw["w_gu"], lw["b_gu"], lw["w_down"], lw["b_down"])


def _final_kernel(x1_ref, moe_ref, g_ref, y_ref):
    xo = x1_ref[...] + moe_ref[...]
    ms = jnp.mean(xo * xo, axis=-1, keepdims=True)
    y_ref[...] = xo * lax.rsqrt(ms + RMS_EPS) * g_ref[...]


def _final(x1, moe, gf, tm):
    n = x1.shape[0]
    row = pl.BlockSpec((tm, D_MODEL), lambda i: (i, 0))
    return pl.pallas_call(
        _final_kernel,
        grid=(n // tm,),
        in_specs=[row, row, pl.BlockSpec((1, D_MODEL), lambda i: (0, 0))],
        out_specs=row,
        out_shape=jax.ShapeDtypeStruct((n, D_MODEL), F32),
        compiler_params=_params(1),
    )(x1, moe, gf)


def _prep_layer(l, norm_mix_g, w_in, attn_sinks, rw_mu, rw_w0, rw_w_up, rw_a0, rw_a_up, rw_g_up,
                rw_k_k, rw_k_a, rw_r_k, rw_lnx_g, rw_lnx_b, w_branch_attn, w_branch_rwkv, w_out,
                norm_ffn_g, w_router, b_router, w_gate_up, b_gate_up, w_down, b_down):
    zeros = jnp.zeros((W_LORA, RW_WIDTH), F32)
    wa_up = jnp.concatenate([jnp.concatenate([rw_w_up[l], zeros], axis=1),
                             jnp.concatenate([zeros, rw_a_up[l]], axis=1)], axis=0)
    pad = LANES - N_EXPERTS
    return {
        "norm_mix_g": norm_mix_g[l].reshape(1, D_MODEL),
        "w_in": _bf(w_in[l]),
        "attn_sinks": attn_sinks[l].astype(F32),
        "rw_mu": rw_mu[l], "rw_w0": rw_w0[l], "rw_a0": rw_a0[l], "rw_k_k": rw_k_k[l], "rw_k_a": rw_k_a[l],
        "rw_r_k": rw_r_k[l], "rw_lnx_g": rw_lnx_g[l], "rw_lnx_b": rw_lnx_b[l],
        "wa_up": _bf(wa_up), "g_up": _bf(rw_g_up[l]),
        "w_ba": _bf(w_branch_attn[l]), "w_br": _bf(w_branch_rwkv[l]), "w_o": _bf(w_out[l]),
        "norm_ffn_g": norm_ffn_g[l].reshape(1, D_MODEL),
        "w_r": _bf(jnp.pad(w_router[l], ((0, 0), (0, pad)))),
        "b_r": jnp.pad(b_router[l], (0, pad), constant_values=-jnp.inf).reshape(1, LANES),
        "w_gu": _gu_regroup(w_gate_up[l]),
        "b_gu": b_gate_up[l].reshape(N_EXPERTS, D_EXPERT // LANES, LANES, 2).transpose(0, 1, 3, 2)
                .reshape(N_EXPERTS, 1, 2 * D_EXPERT),
        "w_down": w_down[l], "b_down": b_down[l][:, None, :],
    }


def _tile(n, pref):
    tm = pref
    while n % tm:
        tm //= 2
    return tm


def _layer(x, tables, cache_k, cache_v, s0, shift0, lw, gf, rw_sets, rw_segs):
    b, t, _ = x.shape
    n = b * t
    x2d = x.reshape(n, D_MODEL)
    tm = _tile(n, IN_TILE)
    q, k, v, zr, zg = _inproj(x2d, lw["norm_mix_g"], lw["w_in"], tables, tm)
    q3, k3, v3 = q.reshape(b, t, ATT_Q), k.reshape(b, t, ATT_KV), v.reshape(b, t, ATT_KV)
    if cache_k is None:
        o_att, k_win, v_win = _attn_prompt(q3, k3, v3, lw["attn_sinks"])
    else:
        o_att, k_win, v_win = _attn_sample(q3, k3, v3, lw["attn_sinks"],
                                           cache_k.reshape(b, WINDOW, ATT_KV), cache_v.reshape(b, WINDOW, ATT_KV))
    zr3 = zr.reshape(b, t, RW_COLS)
    o_rw, s_new = _rwkv(zr3, s0, shift0, lw, rw_sets, RW_CHUNK, rw_segs)
    tm2 = _tile(n, 512)
    x1, hn, comb, tile_counts = _mix(x2d, o_att.reshape(n, ATT_Q), o_rw.reshape(n, RW_WIDTH), zg, lw, tm2)
    moe = _moe(hn, comb, tile_counts, lw, _tile(n, MOE_BLOCK))
    y = _final(x1, moe, gf, tm2)
    kv_shape = (b, WINDOW, N_KV_HEADS, HEAD_DIM)
    return (y.reshape(b, t, D_MODEL), k_win.reshape(kv_shape), v_win.reshape(kv_shape),
            s_new, zr3[:, t - 1, :])


def kernel(x_prompt, x_sample, cache_k, cache_v, state_wkv, state_shift, norm_mix_g, w_in, attn_sinks, rw_mu, rw_w0, rw_w_up, rw_a0, rw_a_up, rw_g_up, rw_k_k, rw_k_a, rw_r_k, rw_lnx_g, rw_lnx_b, w_branch_attn, w_branch_rwkv, w_out, norm_ffn_g, w_router, b_router, w_gate_up, b_gate_up, w_down, b_down, norm_final_g):
    assert w_in.shape[0] == 1, "single-layer trunk"
    bp, tp, _ = x_prompt.shape
    bs, ts, _ = x_sample.shape
    lw = _prep_layer(0, norm_mix_g, w_in, attn_sinks, rw_mu, rw_w0, rw_w_up, rw_a0, rw_a_up, rw_g_up,
                     rw_k_k, rw_k_a, rw_r_k, rw_lnx_g, rw_lnx_b, w_branch_attn, w_branch_rwkv, w_out,
                     norm_ffn_g, w_router, b_router, w_gate_up, b_gate_up, w_down, b_down)
    gf = norm_final_g.reshape(1, D_MODEL)

    tab_p = _rope_tables(jnp.arange(tp, dtype=jnp.int32))
    tm_s = _tile(bs * ts, IN_TILE)
    pos_s = PAST_LEN + jnp.arange(ts, dtype=jnp.int32)
    tab_s = tuple(jnp.tile(u, (tm_s // ts, 1)) for u in _rope_tables(pos_s))

    s0p = jnp.zeros((bp, RW_HEADS, RW_HEAD, RW_HEAD), state_wkv.dtype)
    sh0p = jnp.zeros((bp, RW_COLS), state_shift.dtype)
    sets_p = RW_SEQS_PER_STEP if bp % RW_SEQS_PER_STEP == 0 else 1
    segs_s = RW_CHUNK // ts
    sets_s = 2 if bs % (2 * segs_s) == 0 else 1
    yp, kp, vp, sp, shp = _layer(x_prompt, tab_p, None, None, s0p, sh0p, lw, gf, sets_p, 1)
    ys, ks, vs, ss, shs = _layer(x_sample, tab_s, cache_k[0], cache_v[0], state_wkv[0], state_shift[0],
                                 lw, gf, sets_s, segs_s)
    ex = lambda u: u[None]
    return (yp, ys, ex(kp), ex(vp), ex(sp), ex(shp), ex(ks), ex(vs), ex(ss), ex(shs))
```

```python
import functools
import math

import jax
import jax.numpy as jnp
from jax import lax
from jax.experimental import pallas as pl
from jax.experimental.pallas import tpu as pltpu

F32 = jnp.float32
BF16 = jnp.bfloat16

LANES = 128
SUBLANES = 8
VMEM_LIMIT_BYTES = 56 * 1024 * 1024

D_MODEL = 1024
HEAD_DIM = 64
N_Q_HEADS = 16
N_KV_HEADS = 4
Q_PER_KV = 4
WINDOW = 128
ROPE_THETA = 500000.0
ROPE_DIM = 16
ROPE_HALF = 8
ATTN_SCALE = HEAD_DIM ** -0.5
PAST_LEN = 16384
RW_HEAD = 64
RW_HEADS = 16
RW_PAIRS = RW_HEADS // 2
W_LORA = 64
A_LORA = 64
G_LORA = 128
LNX_EPS = 64e-5
N_EXPERTS = 32
TOP_K = 4
D_EXPERT = 1024
SWIGLU_LIMIT = 7.0
SWIGLU_ALPHA = 1.702
RMS_EPS = 1e-5
ATT_Q = N_Q_HEADS * HEAD_DIM
ATT_KV = N_KV_HEADS * HEAD_DIM
RW_WIDTH = RW_HEADS * RW_HEAD
RW_COLS = 3 * RW_WIDTH + W_LORA + A_LORA + G_LORA
GATE_COLS = 2 * D_MODEL
IN_COLS = ATT_Q + 2 * ATT_KV + RW_COLS + GATE_COLS
DECAY_SCALE = math.exp(-0.5)
RW_CHUNK = 64
IN_TILE = 512
MOE_BLOCK = 2048
RW_SEQS_PER_STEP = 4


def _nn(a, b):
    return jnp.dot(a, b, preferred_element_type=F32)


def _nt(a, b):
    return lax.dot_general(a, b, (((1,), (1,)), ((), ())), preferred_element_type=F32)


def _tn(a, b):
    return lax.dot_general(a, b, (((0,), (0,)), ((), ())), preferred_element_type=F32)


def _bf(x):
    return x.astype(BF16)


def _sigmoid(x):
    return 0.5 * jnp.tanh(0.5 * x) + 0.5


def _split2(x):
    hi = x.astype(BF16)
    lo = (x - hi.astype(F32)).astype(BF16)
    return hi, lo


def _split3(x):
    hi = x.astype(BF16)
    r1 = x - hi.astype(F32)
    mid = r1.astype(BF16)
    lo = (r1 - mid.astype(F32)).astype(BF16)
    return hi, mid, lo


def _params(n_axes):
    return pltpu.CompilerParams(
        dimension_semantics=("arbitrary",) * n_axes, vmem_limit_bytes=VMEM_LIMIT_BYTES)


_IN_CHUNK = 512


def _inproj_kernel(x_ref, g_ref, w_ref, cos_ref, sa_ref, sb_ref,
                   q_ref, k_ref, v_ref, zr_ref, zg_ref):
    x = x_ref[...]
    ms = jnp.mean(x * x, axis=-1, keepdims=True)
    h = _bf(x * lax.rsqrt(ms + RMS_EPS) * g_ref[...])
    cos, sa, sb = cos_ref[...], sa_ref[...], sb_ref[...]

    def rope(y):
        return (y * cos + pltpu.roll(y, LANES - ROPE_HALF, axis=1) * sa
                + pltpu.roll(y, ROPE_HALF, axis=1) * sb)

    def project(out_ref, col0, width, with_rope):
        for c in range(0, width, _IN_CHUNK):
            cw = min(_IN_CHUNK, width - c)
            acc = _nn(h, w_ref[:, col0 + c:col0 + c + cw])
            if with_rope:
                for j in range(0, cw, LANES):
                    out_ref[:, c + j:c + j + LANES] = rope(acc[:, j:j + LANES])
            else:
                out_ref[:, c:c + cw] = acc

    project(q_ref, 0, ATT_Q, True)
    project(k_ref, ATT_Q, ATT_KV, True)
    project(v_ref, ATT_Q + ATT_KV, ATT_KV, False)
    project(zr_ref, ATT_Q + 2 * ATT_KV, RW_COLS, False)
    project(zg_ref, ATT_Q + 2 * ATT_KV + RW_COLS, GATE_COLS, False)


def _rope_tables(positions):
    inv_freq = ROPE_THETA ** (-jnp.arange(ROPE_HALF, dtype=F32) / ROPE_HALF)
    ang = positions.astype(F32)[:, None] * inv_freq[None, :]
    cos, sin = jnp.cos(ang), jnp.sin(ang)
    t = positions.shape[0]
    one = jnp.ones((t, HEAD_DIM - ROPE_DIM), F32)
    zero = jnp.zeros((t, HEAD_DIM - ROPE_DIM), F32)
    z8 = jnp.zeros((t, ROPE_HALF), F32)
    cos_h = jnp.concatenate([cos, cos, one], axis=1)
    sa_h = jnp.concatenate([-sin, z8, zero], axis=1)
    sb_h = jnp.concatenate([z8, sin, zero], axis=1)
    two = lambda u: jnp.concatenate([u, u], axis=1)
    return two(cos_h), two(sa_h), two(sb_h)


def _inproj(x2d, g, w_bf, tables, tm):
    n = x2d.shape[0]
    cos, sa, sb = tables
    nper = cos.shape[0] // tm
    row = lambda i: (i, 0)
    tab = lambda i: (i % nper, 0)
    const = lambda i: (0, 0)
    out_shapes = [jax.ShapeDtypeStruct((n, w), F32) for w in (ATT_Q, ATT_KV, ATT_KV, RW_COLS, GATE_COLS)]
    return pl.pallas_call(
        _inproj_kernel,
        grid=(n // tm,),
        in_specs=[
            pl.BlockSpec((tm, D_MODEL), row),
            pl.BlockSpec((1, D_MODEL), const),
            pl.BlockSpec((D_MODEL, IN_COLS), const, pipeline_mode=pl.Buffered(1)),
            pl.BlockSpec((tm, LANES), tab),
            pl.BlockSpec((tm, LANES), tab),
            pl.BlockSpec((tm, LANES), tab),
        ],
        out_specs=[pl.BlockSpec((tm, w), row) for w in (ATT_Q, ATT_KV, ATT_KV, RW_COLS, GATE_COLS)],
        out_shape=out_shapes,
        compiler_params=_params(1),
    )(x2d, g, w_bf, cos, sa, sb)


def _attn_prompt_kernel(sink_ref, q_ref, kp_ref, kc_ref, vp_ref, vc_ref, o_ref, kw_ref, vw_ref):
    n = pl.program_id(1)
    w = WINDOW
    lane = lax.broadcasted_iota(jnp.int32, (w, LANES), 1)
    first_half = lane < HEAD_DIM
    qi = lax.broadcasted_iota(jnp.int32, (Q_PER_KV * w, 2 * w), 0) % w
    kj = lax.broadcasted_iota(jnp.int32, (Q_PER_KV * w, 2 * w), 1)
    first_key = jnp.where(n > 0, 0, w)
    band4 = (kj > qi) & (kj <= qi + w) & (kj >= first_key)

    kw_ref[...] = kc_ref[...]
    vw_ref[...] = vc_ref[...]

    lane2 = lax.broadcasted_iota(jnp.int32, (2 * w, LANES), 1)
    kvs = range(N_KV_HEADS)
    k2, v2 = [], []
    for kv in kvs:
        gs = slice((kv // 2) * LANES, (kv // 2 + 1) * LANES)
        mine = (lane2 < HEAD_DIM) if kv % 2 == 0 else (lane2 >= HEAD_DIM)
        k2.append(_bf(jnp.concatenate([kp_ref[:, gs], kc_ref[:, gs]], axis=0)))
        v2.append(_bf(jnp.where(mine, jnp.concatenate([vp_ref[:, gs], vc_ref[:, gs]], axis=0), 1.0)))
    qs, sink = [], []
    for kv in kvs:
        jh = kv % 2
        keep = first_half if jh == 0 else jnp.logical_not(first_half)
        rows, sinks = [], []
        for gq in range(Q_PER_KV):
            hq = kv * Q_PER_KV + gq
            qg = q_ref[:, (hq // 2) * LANES:(hq // 2 + 1) * LANES]
            if hq % 2 != jh:
                qg = pltpu.roll(qg, HEAD_DIM, axis=1)
            rows.append(jnp.where(keep, qg * ATTN_SCALE, 0.0))
            sinks.append(jnp.full((w, 1), sink_ref[hq], F32))
        qs.append(_bf(jnp.concatenate(rows, axis=0)))
        sink.append(jnp.concatenate(sinks, axis=0))
    s = [jnp.where(band4, _nt(q_, k_), -jnp.inf) for q_, k_ in zip(qs, k2)]
    m = [jnp.maximum(jnp.max(s_, axis=-1, keepdims=True), sk) for s_, sk in zip(s, sink)]
    e = [_bf(jnp.exp(s_ - m_)) for s_, m_ in zip(s, m)]
    e_sink = [jnp.exp(sk - m_) for sk, m_ in zip(sink, m)]
    pv = [_nn(e_, v_) for e_, v_ in zip(e, v2)]
    for kv in kvs:
        jh = kv % 2
        for go in range(Q_PER_KV // 2):
            ra, rb = slice(2 * go * w, (2 * go + 1) * w), slice((2 * go + 1) * w, (2 * go + 2) * w)
            pa, pb = pv[kv][ra], pv[kv][rb]
            pa_r, pb_r = pltpu.roll(pa, HEAD_DIM, axis=1), pltpu.roll(pb, HEAD_DIM, axis=1)
            if jh == 0:
                oa, ob = pa / (pa_r + e_sink[kv][ra]), pb_r / (pb + e_sink[kv][rb])
            else:
                oa, ob = pa_r / (pa + e_sink[kv][ra]), pb / (pb_r + e_sink[kv][rb])
            og = kv * (Q_PER_KV // 2) + go
            o_ref[:, og * LANES:(og + 1) * LANES] = jnp.where(first_half, oa, ob).astype(o_ref.dtype)


def _attn_prompt(q, k, v, sinks):
    b, t, _ = q.shape
    nb = t // WINDOW
    cur = lambda bi, n: (bi, n, 0)
    prev = lambda bi, n: (bi, jnp.maximum(n - 1, 0), 0)
    win = lambda bi, n: (bi, 0, 0)
    return pl.pallas_call(
        _attn_prompt_kernel,
        grid=(b, nb),
        in_specs=[
            pl.BlockSpec(memory_space=pltpu.SMEM),
            pl.BlockSpec((None, WINDOW, ATT_Q), cur),
            pl.BlockSpec((None, WINDOW, ATT_KV), prev),
            pl.BlockSpec((None, WINDOW, ATT_KV), cur),
            pl.BlockSpec((None, WINDOW, ATT_KV), prev),
            pl.BlockSpec((None, WINDOW, ATT_KV), cur),
        ],
        out_specs=[
            pl.BlockSpec((None, WINDOW, ATT_Q), cur),
            pl.BlockSpec((None, WINDOW, ATT_KV), win),
            pl.BlockSpec((None, WINDOW, ATT_KV), win),
        ],
        out_shape=[
            jax.ShapeDtypeStruct((b, t, ATT_Q), BF16),
            jax.ShapeDtypeStruct((b, WINDOW, ATT_KV), F32),
            jax.ShapeDtypeStruct((b, WINDOW, ATT_KV), F32),
        ],
        compiler_params=_params(2),
    )(sinks, q, k, k, v, v)


_SAMPLE_BT = 8


def _attn_sample_kernel(sink_ref, q_ref, k_ref, v_ref, ck_ref, cv_ref, o_ref, nk_ref, nv_ref, *, t):
    w = WINDOW
    rows_per_grp = 2 * Q_PER_KV * t
    lane = lax.broadcasted_iota(jnp.int32, (t, LANES), 1)
    first_half = lane < HEAD_DIM
    r_c = lax.broadcasted_iota(jnp.int32, (rows_per_grp, w), 0) % t
    c_c = lax.broadcasted_iota(jnp.int32, (rows_per_grp, w), 1)
    mask_c = c_c > r_c
    r_n = lax.broadcasted_iota(jnp.int32, (rows_per_grp, t), 0) % t
    c_n = lax.broadcasted_iota(jnp.int32, (rows_per_grp, t), 1)
    mask_n = c_n <= r_n

    for bi in range(_SAMPLE_BT):
        nk_ref[bi, 0:w - t, :] = ck_ref[bi, t:w, :]
        nk_ref[bi, w - t:w, :] = k_ref[bi]
        nv_ref[bi, 0:w - t, :] = cv_ref[bi, t:w, :]
        nv_ref[bi, w - t:w, :] = v_ref[bi]

    chains = [(bi, grp) for bi in range(_SAMPLE_BT) for grp in range(ATT_KV // LANES)]
    gsl = lambda grp: slice(grp * LANES, (grp + 1) * LANES)
    kc = [_bf(ck_ref[bi, :, gsl(grp)]) for bi, grp in chains]
    vc = [_bf(cv_ref[bi, :, gsl(grp)]) for bi, grp in chains]
    kn = [_bf(k_ref[bi, :, gsl(grp)]) for bi, grp in chains]
    vn = [_bf(v_ref[bi, :, gsl(grp)]) for bi, grp in chains]
    sink_rows = [[jnp.full((t, 1), sink_ref[(grp * 2 + jh) * Q_PER_KV + gq], F32)
                  for jh in range(2) for gq in range(Q_PER_KV)] for grp in range(ATT_KV // LANES)]
    sink_g = [jnp.concatenate(r_, axis=0) for r_ in sink_rows]
    qs = []
    for bi, grp in chains:
        rows = []
        for jh in range(2):
            keep = first_half if jh == 0 else jnp.logical_not(first_half)
            for gq in range(Q_PER_KV):
                hq = (grp * 2 + jh) * Q_PER_KV + gq
                qg = q_ref[bi, :, (hq // 2) * LANES:(hq // 2 + 1) * LANES]
                if hq % 2 != jh:
                    qg = pltpu.roll(qg, HEAD_DIM, axis=1)
                rows.append(jnp.where(keep, qg * ATTN_SCALE, 0.0))
        qs.append(_bf(jnp.concatenate(rows, axis=0)))
    sink = [sink_g[grp] for _, grp in chains]
    s_c = [jnp.where(mask_c, _nt(q_, k_), -jnp.inf) for q_, k_ in zip(qs, kc)]
    s_n = [jnp.where(mask_n, _nt(q_, k_), -jnp.inf) for q_, k_ in zip(qs, kn)]
    m = [jnp.maximum(jnp.maximum(jnp.max(a, axis=-1, keepdims=True), jnp.max(b_, axis=-1, keepdims=True)), sk)
         for a, b_, sk in zip(s_c, s_n, sink)]
    e_c = [jnp.exp(a - m_) for a, m_ in zip(s_c, m)]
    e_n = [jnp.exp(a - m_) for a, m_ in zip(s_n, m)]
    denom = [jnp.sum(a, axis=-1, keepdims=True) + jnp.sum(b_, axis=-1, keepdims=True) + jnp.exp(sk - m_)
             for a, b_, sk, m_ in zip(e_c, e_n, sink, m)]
    pv = [_nn(_bf(a / d_), va) + _nn(_bf(b_ / d_), vb)
          for a, b_, d_, va, vb in zip(e_c, e_n, denom, vc, vn)]
    for (bi, grp), pv_ in zip(chains, pv):
        for jh in range(2):
            for go in range(Q_PER_KV // 2):
                r0 = (jh * Q_PER_KV + 2 * go) * t
                pa, pb = pv_[r0:r0 + t], pv_[r0 + t:r0 + 2 * t]
                if jh == 0:
                    pb = pltpu.roll(pb, HEAD_DIM, axis=1)
                else:
                    pa = pltpu.roll(pa, HEAD_DIM, axis=1)
                og = (grp * 2 + jh) * (Q_PER_KV // 2) + go
                o_ref[bi, :, og * LANES:(og + 1) * LANES] = jnp.where(first_half, pa, pb)


def _attn_sample(q, k, v, sinks, cache_k, cache_v):
    b, t, _ = q.shape
    bt = _SAMPLE_BT
    blk = lambda i: (i, 0, 0)
    return pl.pallas_call(
        functools.partial(_attn_sample_kernel, t=t),
        grid=(b // bt,),
        in_specs=[
            pl.BlockSpec(memory_space=pltpu.SMEM),
            pl.BlockSpec((bt, t, ATT_Q), blk),
            pl.BlockSpec((bt, t, ATT_KV), blk),
            pl.BlockSpec((bt, t, ATT_KV), blk),
            pl.BlockSpec((bt, WINDOW, ATT_KV), blk),
            pl.BlockSpec((bt, WINDOW, ATT_KV), blk),
        ],
        out_specs=[
            pl.BlockSpec((bt, t, ATT_Q), blk),
            pl.BlockSpec((bt, WINDOW, ATT_KV), blk),
            pl.BlockSpec((bt, WINDOW, ATT_KV), blk),
        ],
        out_shape=[
            jax.ShapeDtypeStruct((b, t, ATT_Q), F32),
            jax.ShapeDtypeStruct((b, WINDOW, ATT_KV), F32),
            jax.ShapeDtypeStruct((b, WINDOW, ATT_KV), F32),
        ],
        compiler_params=_params(1),
    )(sinks, q, k, v, cache_k, cache_v)


def _rwkv_kernel(z_ref, s0_ref, shift_ref, mu_ref, w0_ref, wa_up_ref, a0_ref, g_up_ref,
                 kk_ref, ka_ref, rk_ref, lng_ref, lnb_ref, o_ref, sout_ref, s_ref, prev_ref, *, nb, c, nseg):
    tseg = c // nseg
    rows_all = nb * c
    step = pl.program_id(1)
    pairs = range(RW_PAIRS)
    sls = [slice(p * LANES, (p + 1) * LANES) for p in pairs]
    rsl = [slice(j * c, (j + 1) * c) for j in range(nb)]

    @pl.when(step == 0)
    def _():
        zero = jnp.zeros((RW_HEAD, RW_HEAD), F32)
        for q in range(nb * nseg):
            for p in pairs:
                top = jnp.concatenate([s0_ref[q, 2 * p], zero], axis=1)
                bot = jnp.concatenate([zero, s0_ref[q, 2 * p + 1]], axis=1)
                s_ref[q, p] = jnp.concatenate([top, bot], axis=0)
        if nseg == 1:
            prev_ref[...] = shift_ref[...].reshape(nb, RW_COLS)

    z = z_ref[...].reshape(rows_all, RW_COLS)
    row1 = lax.broadcasted_iota(jnp.int32, (rows_all, 1), 0)
    zprev = pltpu.roll(z, 1, axis=0)
    if nseg == 1:
        for j in range(nb):
            zprev = jnp.where(row1 == j * c, prev_ref[j:j + 1, :], zprev)
            prev_ref[j:j + 1, :] = z[(j + 1) * c - 1:(j + 1) * c, :]
    else:
        zprev = jnp.where(row1 % tseg == 0, shift_ref[...].reshape(rows_all, RW_COLS), zprev)
    zs = z + (zprev - z) * mu_ref[...]

    w3 = 3 * RW_WIDTH
    r, k, v = zs[:, 0:RW_WIDTH], zs[:, RW_WIDTH:2 * RW_WIDTH], zs[:, 2 * RW_WIDTH:w3]
    xwa = zs[:, w3:w3 + LANES]
    xg = zs[:, w3 + LANES:w3 + 2 * LANES]
    lane_all = lax.broadcasted_iota(jnp.int32, (rows_all, LANES), 1)
    lora = _nn(_bf(jnp.where(lane_all < RW_HEAD, jnp.tanh(xwa), xwa)), wa_up_ref[...])
    lane = lax.broadcasted_iota(jnp.int32, (c, LANES), 1)
    head0 = lane < RW_HEAD
    lw = -DECAY_SCALE * _sigmoid(w0_ref[...] + lora[:, 0:RW_WIDTH])
    a_sig = _sigmoid(a0_ref[...] + lora[:, RW_WIDTH:2 * RW_WIDTH])
    g = _nn(_bf(_sigmoid(xg)), g_up_ref[...])
    kk = k * kk_ref[...]
    k = k * (1.0 + (a_sig - 1.0) * ka_ref[...])
    rkr = r * k * rk_ref[...]

    ti = lax.broadcasted_iota(jnp.int32, (rows_all, rows_all), 0)
    tj = lax.broadcasted_iota(jnp.int32, (rows_all, rows_all), 1)
    same_seq = (ti // tseg) == (tj // tseg)
    tri = jnp.where((tj <= ti) & same_seq, 1.0, 0.0).astype(BF16)
    lw3 = _split3(lw)
    cum = _nn(tri, lw3[0]) + _nn(tri, lw3[1]) + _nn(tri, lw3[2])
    if nseg == 1:
        ends = [jnp.broadcast_to(cum[(j + 1) * c - 1:(j + 1) * c, :], (c, RW_WIDTH)) for j in range(nb)]
        c_end = ends[0] if nb == 1 else jnp.concatenate(ends, axis=0)
    else:
        seq1 = jnp.where(same_seq, 1.0, 0.0).astype(BF16)
        c_end = _nn(seq1, lw3[0]) + _nn(seq1, lw3[1]) + _nn(seq1, lw3[2])
    e_c, e_ci, e_cm = jnp.exp(cum), jnp.exp(-cum), jnp.exp(cum - lw)
    e_ce, w_end = jnp.exp(c_end - cum), jnp.exp(c_end)

    gi = lax.broadcasted_iota(jnp.int32, (2 * LANES, LANES), 0) % LANES
    gj = lax.broadcasted_iota(jnp.int32, (2 * LANES, LANES), 1)
    ones2 = jnp.where((gi // RW_HEAD) == (gj // RW_HEAD), 1.0, 0.0).astype(BF16)
    bi_ = lax.broadcasted_iota(jnp.int32, (LANES, LANES), 0)
    bj_ = lax.broadcasted_iota(jnp.int32, (LANES, LANES), 1)
    same_head = (bi_ // RW_HEAD) == (bj_ // RW_HEAD)
    ci = lax.broadcasted_iota(jnp.int32, (c, 2 * c), 0)
    cj = lax.broadcasted_iota(jnp.int32, (c, 2 * c), 1)
    cjm = cj % c
    seq_ok = (ci // tseg) == (cjm // tseg)
    strict = (cjm < ci) & seq_ok
    incl = (cjm <= ci) & seq_ok
    eye_cat = jnp.where(cjm == ci, 1.0, 0.0)
    left = cj < c

    def seg_sum(x):
        hi, lo = _split2(x)
        return _nn(jnp.concatenate([hi, lo], axis=1), ones2)

    def rows2(x):
        return jnp.concatenate([jnp.where(head0, x, 0.0), jnp.where(head0, 0.0, x)], axis=0)

    def bd(cat):
        return _bf(jnp.concatenate([jnp.where(left, cat, 0.0), jnp.where(left, 0.0, cat)], axis=0))

    def pair_mm(cat, x):
        return _nn(_bf(cat), _bf(rows2(x)))

    def seg_sums(xs):
        tot = seg_sum(jnp.concatenate(xs, axis=0))
        return [tot[i * c:(i + 1) * c] for i in range(len(xs))]

    segs = range(nseg)
    rs = [slice(q * tseg, (q + 1) * tseg) for q in segs]

    def run(chains):
        gs = [(rsl[j], sls[p]) for j, p in chains]
        idx = range(len(chains))
        ss = seg_sums([kk[s] * kk[s] for s in gs])
        kkn = [kk[s] * lax.rsqrt(jnp.maximum(q, 1e-24)) for s, q in zip(gs, ss)]
        bv = [n_ * a_sig[s] for s, n_ in zip(gs, kkn)]
        rt = [r[s] * e_c[s] for s in gs]
        kt = [k[s] * e_ci[s] for s in gs]
        at = [-n_ * e_cm[s] for s, n_ in zip(gs, kkn)]
        bt = [b_ * e_ci[s] for s, b_ in zip(gs, bv)]
        bh = [b_ * e_ce[s] for s, b_ in zip(gs, bv)]
        kh = [k[s] * e_ce[s] for s in gs]
        vv = [v[s] for s in gs]

        ar = [_bf(jnp.concatenate([a_, r_], axis=0)) for a_, r_ in zip(at, rt)]
        xbk = [_nt(x, _bf(jnp.concatenate([rows2(b_), rows2(k_)], axis=0)))
               for x, b_, k_ in zip(ar, bt, kt)]
        l_ab = [jnp.where(strict, x[0:c, 0:2 * c], 0.0) for x in xbk]
        l_ak = [jnp.where(strict, x[0:c, 2 * c:4 * c], 0.0) for x in xbk]
        m_rb = [jnp.where(incl, x[c:2 * c, 0:2 * c], 0.0) for x in xbk]
        m_rk = [jnp.where(incl, x[c:2 * c, 2 * c:4 * c], 0.0) for x in xbk]

        t_inv = [l + eye_cat for l in l_ab]
        pw = l_ab
        pw_bd = [bd(x) for x in pw]
        for _ in range(int(math.log2(tseg)) - 1):
            pw = [_nn(_bf(x), xb_) for x, xb_ in zip(pw, pw_bd)]
            pw_bd = [bd(x) for x in pw]
            t_inv = [t + _nn(_bf(t), xb_) for t, xb_ in zip(t_inv, pw_bd)]

        lvy = [pair_mm(jnp.concatenate([l, m], axis=0), v_) for l, m, v_ in zip(l_ak, m_rk, vv)]
        y1 = [x[c:2 * c] for x in lvy]
        au = [_nn(_bf(t), _bf(jnp.concatenate([rows2(a_), rows2(x[0:c])], axis=1)))
              for t, a_, x in zip(t_inv, at, lvy)]
        a_hat = [x[:, 0:LANES] for x in au]
        u0 = [x[:, LANES:2 * LANES] for x in au]

        s_old = [[s_ref[j * nseg + q, p] for q in segs] for j, p in chains]
        pp = [[_nt(_bf(jnp.concatenate([a_hat[i][rs[q]], rt[i][rs[q]]], axis=0)), _bf(s_old[i][q]))
               for q in segs] for i in idx]
        u = [jnp.concatenate([pp[i][q][0:tseg] for q in segs], axis=0) + u0[i] for i in idx]
        y0 = [jnp.concatenate([pp[i][q][tseg:2 * tseg] for q in segs], axis=0) for i in idx]
        y = [y0[i] + pair_mm(m_rb[i], u[i]) + y1[i] for i in idx]
        for i in idx:
            for q in segs:
                upd = _tn(_bf(jnp.concatenate([u[i][rs[q]], vv[i][rs[q]]], axis=0)),
                          _bf(jnp.concatenate([bh[i][rs[q]], kh[i][rs[q]]], axis=0)))
                j, p = chains[i]
                row0 = (j * c + q * tseg) % w_end.shape[0]
                w_q = w_end[row0:row0 + 1, sls[p]]
                s_ref[j * nseg + q, p] = s_old[i][q] * w_q + jnp.where(same_head, upd, 0.0)

        mean = [x * (1.0 / RW_HEAD) for x in seg_sums(y)]
        d = [x - m for x, m in zip(y, mean)]
        var = [x * (1.0 / RW_HEAD) for x in seg_sums([x * x for x in d])]
        bonus = [x * v_ for x, v_ in zip(seg_sums([rkr[s] for s in gs]), vv)]
        for i in idx:
            j, p = chains[i]
            yn = d[i] * lax.rsqrt(var[i] + LNX_EPS) * lng_ref[:, sls[p]] + lnb_ref[:, sls[p]]
            o_ref[j, :, sls[p]] = ((yn + bonus[i]) * g[gs[i]]).astype(o_ref.dtype)

    run([(j, p) for j in range(nb) for p in pairs])

    @pl.when(step == pl.num_programs(1) - 1)
    def _():
        for q in range(nb * nseg):
            for p in pairs:
                tile = s_ref[q, p]
                sout_ref[q, 2 * p] = tile[0:RW_HEAD, 0:RW_HEAD]
                sout_ref[q, 2 * p + 1] = tile[RW_HEAD:2 * RW_HEAD, RW_HEAD:2 * RW_HEAD]


def _rwkv(zr, s0, shift0, lw, nb, c, nseg):
    b, t, _ = zr.shape
    if nseg == 1:
        ngrp, nchunk = b // nb, t // c
        z3 = zr
        shift = shift0.reshape(b, 1, RW_COLS)
        shift_spec = pl.BlockSpec((nb, 1, RW_COLS), lambda bi, i: (bi, 0, 0))
    else:
        assert c == nseg * t
        nsets = b // nseg
        ngrp, nchunk = nsets // nb, 1
        z3 = zr.reshape(nsets, c, RW_COLS)
        shift = jnp.pad(shift0[:, None, :], ((0, 0), (0, t - 1), (0, 0))).reshape(nsets, c, RW_COLS)
        shift_spec = pl.BlockSpec((nb, c, RW_COLS), lambda bi, i: (bi, 0, 0))
    vec = lambda name: lw[name].reshape(1, -1).astype(F32)
    cst = lambda bi, i: (0, 0)
    vspec = lambda wd: pl.BlockSpec((1, wd), cst)
    state_spec = pl.BlockSpec((nb * nseg, RW_HEADS, RW_HEAD, RW_HEAD), lambda bi, i: (bi, 0, 0, 0))
    o, s_new = pl.pallas_call(
        functools.partial(_rwkv_kernel, nb=nb, c=c, nseg=nseg),
        grid=(ngrp, nchunk),
        in_specs=[
            pl.BlockSpec((nb, c, RW_COLS), lambda bi, i: (bi, i, 0)),
            state_spec,
            shift_spec,
            vspec(RW_COLS), vspec(RW_WIDTH),
            pl.BlockSpec((LANES, 2 * RW_WIDTH), cst),
            vspec(RW_WIDTH),
            pl.BlockSpec((G_LORA, RW_WIDTH), cst),
            vspec(RW_WIDTH), vspec(RW_WIDTH), vspec(RW_WIDTH), vspec(RW_WIDTH), vspec(RW_WIDTH),
        ],
        out_specs=[
            pl.BlockSpec((nb, c, RW_WIDTH), lambda bi, i: (bi, i, 0)),
            state_spec,
        ],
        out_shape=[
            jax.ShapeDtypeStruct((ngrp * nb, nchunk * c, RW_WIDTH), BF16),
            jax.ShapeDtypeStruct((b, RW_HEADS, RW_HEAD, RW_HEAD), F32),
        ],
        scratch_shapes=[pltpu.VMEM((nb * nseg, RW_PAIRS, LANES, LANES), F32), pltpu.VMEM((nb, RW_COLS), F32)],
        compiler_params=_params(2),
    )(z3, s0, shift, vec("rw_mu"), vec("rw_w0"), lw["wa_up"], vec("rw_a0"),
      lw["g_up"], vec("rw_k_k"), vec("rw_k_a"), vec("rw_r_k"), vec("rw_lnx_g"), vec("rw_lnx_b"))
    return o.reshape(b, t, RW_WIDTH), s_new


def _mix_kernel(x_ref, oa_ref, or_ref, zg_ref, wba_ref, wbr_ref, wo_ref, g_ref, wr_ref, br_ref,
                x1_ref, hn_ref, comb_ref, cnt_ref):
    ya = _nn(_bf(oa_ref[...]), wba_ref[...])
    yr = _nn(_bf(or_ref[...]), wbr_ref[...])
    merged = _sigmoid(zg_ref[:, 0:D_MODEL]) * ya + _sigmoid(zg_ref[:, D_MODEL:2 * D_MODEL]) * yr
    x1 = x_ref[...] + _nn(_bf(merged), wo_ref[...])
    x1_ref[...] = x1
    ms = jnp.mean(x1 * x1, axis=-1, keepdims=True)
    hn = x1 * lax.rsqrt(ms + RMS_EPS) * g_ref[...]
    hn_ref[...] = _bf(hn)
    logits = _nn(_bf(hn), wr_ref[...]) + br_ref[...]
    lane = lax.broadcasted_iota(jnp.int32, logits.shape, 1).astype(F32)
    work = logits
    top = None
    for _ in range(TOP_K):
        m = jnp.max(work, axis=-1, keepdims=True)
        if top is None:
            top = m
        idx = jnp.min(jnp.where(work == m, lane, float(LANES)), axis=-1, keepdims=True)
        work = jnp.where(lane == idx, -jnp.inf, work)
    e = jnp.where(work != logits, jnp.exp(logits - top), 0.0)
    comb = e / jnp.sum(e, axis=-1, keepdims=True)
    comb_ref[...] = comb
    cnt_ref[...] = jnp.sum(jnp.where(comb > 0.0, 1.0, 0.0), axis=0, keepdims=True)


def _mix(x2d, oa, orw, zg, lw, tm):
    n = x2d.shape[0]
    row = lambda i: (i, 0)
    cst = lambda i: (0, 0)
    wspec = pl.BlockSpec((D_MODEL, D_MODEL), cst)
    return pl.pallas_call(
        _mix_kernel,
        grid=(n // tm,),
        in_specs=[
            pl.BlockSpec((tm, D_MODEL), row), pl.BlockSpec((tm, D_MODEL), row), pl.BlockSpec((tm, D_MODEL), row),
            pl.BlockSpec((tm, GATE_COLS), row),
            wspec, wspec, wspec,
            pl.BlockSpec((1, D_MODEL), cst),
            pl.BlockSpec((D_MODEL, LANES), cst),
            pl.BlockSpec((1, LANES), cst),
        ],
        out_specs=[pl.BlockSpec((tm, D_MODEL), row), pl.BlockSpec((tm, D_MODEL), row),
                   pl.BlockSpec((tm, LANES), row), pl.BlockSpec((None, 1, LANES), lambda i: (i, 0, 0))],
        out_shape=[jax.ShapeDtypeStruct((n, D_MODEL), F32), jax.ShapeDtypeStruct((n, D_MODEL), BF16),
                   jax.ShapeDtypeStruct((n, LANES), F32), jax.ShapeDtypeStruct((n // tm, 1, LANES), F32)],
        compiler_params=_params(1),
    )(x2d, oa, orw, zg, lw["w_ba"], lw["w_br"], lw["w_o"], lw["norm_ffn_g"], lw["w_r"], lw["b_r"])


_GU_GROUP = 2 * LANES


def _gu_regroup_kernel(w_ref, o_ref):
    src = lax.broadcasted_iota(jnp.int32, (_GU_GROUP, _GU_GROUP), 0)
    dst = lax.broadcasted_iota(jnp.int32, (_GU_GROUP, _GU_GROUP), 1)
    want = jnp.where(dst < LANES, 2 * dst, 2 * (dst - LANES) + 1)
    perm = jnp.where(src == want, 1.0, 0.0).astype(BF16)
    for j in range(0, 2 * D_EXPERT, _GU_GROUP):
        o_ref[:, j:j + _GU_GROUP] = _nn(_bf(w_ref[:, j:j + _GU_GROUP]), perm).astype(o_ref.dtype)


def _gu_regroup(w_gate_up):
    e, d, n = w_gate_up.shape
    tk = 512
    spec = pl.BlockSpec((None, tk, n), lambda i, j: (i, j, 0))
    return pl.pallas_call(
        _gu_regroup_kernel,
        grid=(e, d // tk),
        in_specs=[spec],
        out_specs=spec,
        out_shape=jax.ShapeDtypeStruct((e, d, n), BF16),
        compiler_params=_params(2),
    )(w_gate_up)


_MOE_DOMAIN = 1024
_MOE_ROWS = 160
_RANK_CHUNK = 256


def _moe_kernel(nsub_ref, hn_ref, comb_ref, x1_ref, wgu_ref, bgu_ref, wd_ref, bd_ref, gf_ref, out_ref,
                rank_ref, rank_t_ref, comb_t_ref, act_ref):
    blk, e = pl.program_id(0), pl.program_id(1)
    tb = hn_ref.shape[0]
    dom, rows = min(_MOE_DOMAIN, tb), _MOE_ROWS
    ndom = tb // dom

    @pl.when(e == 0)
    def _():
        out_ref[...] = x1_ref[...]
        ci = lax.broadcasted_iota(jnp.int32, (_RANK_CHUNK, _RANK_CHUNK), 0)
        cj = lax.broadcasted_iota(jnp.int32, (_RANK_CHUNK, _RANK_CHUNK), 1)
        before = jnp.where(cj < ci, 1.0, 0.0).astype(BF16)
        for d0 in range(0, tb, dom):
            seen = jnp.zeros((1, LANES), F32)
            for c0 in range(d0, d0 + dom, _RANK_CHUNK):
                routed = comb_ref[c0:c0 + _RANK_CHUNK, :] > 0.0
                hot = jnp.where(routed, 1.0, 0.0)
                rank_ref[c0:c0 + _RANK_CHUNK, :] = jnp.where(routed, _nn(before, _bf(hot)) + seen, -1.0)
                seen = seen + jnp.sum(hot, axis=0, keepdims=True)
        rank_t_ref[...] = rank_ref[...].T
        comb_t_ref[...] = comb_ref[...].T

    slot_r = lax.broadcasted_iota(jnp.int32, (rows, dom), 0).astype(F32)

    for d in range(ndom):
        ds_ = slice(d * dom, (d + 1) * dom)
        rank_row = rank_t_ref[pl.ds(e, 1), ds_]
        w_row = comb_t_ref[pl.ds(e, 1), ds_]

        def sub_tile(s, carry, ds_=ds_, rank_row=rank_row, w_row=w_row):
            base = (s * rows).astype(F32)
            hit = rank_row - base == slot_r
            gather = jnp.where(hit, 1.0, 0.0).astype(BF16)
            x = _bf(_nn(gather, hn_ref[ds_, :]))
            w_rows = jnp.sum(jnp.where(hit, w_row, 0.0), axis=1, keepdims=True)
            for j in range(D_EXPERT // LANES):
                gs = slice(j * _GU_GROUP, (j + 1) * _GU_GROUP)
                gu = _nn(x, wgu_ref[:, gs]) + bgu_ref[:, gs]
                glu = jnp.minimum(gu[:, 0:LANES], SWIGLU_LIMIT)
                lin = jnp.clip(gu[:, LANES:_GU_GROUP], -SWIGLU_LIMIT, SWIGLU_LIMIT)
                act_ref[:, j * LANES:(j + 1) * LANES] = _bf(glu * _sigmoid(SWIGLU_ALPHA * glu) * (lin + 1.0))
            y = (_nn(act_ref[...], _bf(wd_ref[...])) + bd_ref[...]) * w_rows
            out_ref[ds_, :] += _tn(gather, _bf(y))
            return carry

        lax.fori_loop(0, nsub_ref[(blk * ndom + d) * N_EXPERTS + e], sub_tile, 0)

    @pl.when(e == N_EXPERTS - 1)
    def _():
        for r0 in range(0, tb, _RANK_CHUNK):
            xo = out_ref[r0:r0 + _RANK_CHUNK, :]
            ms = jnp.mean(xo * xo, axis=-1, keepdims=True)
            out_ref[r0:r0 + _RANK_CHUNK, :] = xo * lax.rsqrt(ms + RMS_EPS) * gf_ref[...]


def _moe(hn, comb, tile_counts, x1, lw, gf, tb):
    n = hn.shape[0]
    nblk = n // tb
    ndomains = n // min(_MOE_DOMAIN, tb)
    counts = tile_counts.reshape(ndomains, -1, LANES).sum(axis=1)[:, :N_EXPERTS].astype(jnp.int32)
    nsub = ((counts + _MOE_ROWS - 1) // _MOE_ROWS).reshape(-1)
    row = lambda i, e, ns: (i, 0)
    ex = lambda i, e, ns: (e, 0, 0)
    return pl.pallas_call(
        _moe_kernel,
        grid_spec=pltpu.PrefetchScalarGridSpec(
            num_scalar_prefetch=1,
            grid=(nblk, N_EXPERTS),
            in_specs=[
                pl.BlockSpec((tb, D_MODEL), row), pl.BlockSpec((tb, LANES), row),
                pl.BlockSpec((tb, D_MODEL), row, pipeline_mode=pl.Buffered(1)),
                pl.BlockSpec((None, D_MODEL, 2 * D_EXPERT), ex), pl.BlockSpec((None, 1, 2 * D_EXPERT), ex),
                pl.BlockSpec((None, D_EXPERT, D_MODEL), ex), pl.BlockSpec((None, 1, D_MODEL), ex),
                pl.BlockSpec((1, D_MODEL), lambda i, e, ns: (0, 0)),
            ],
            out_specs=pl.BlockSpec((tb, D_MODEL), row),
            scratch_shapes=[pltpu.VMEM((tb, LANES), F32), pltpu.VMEM((LANES, tb), F32),
                            pltpu.VMEM((LANES, tb), F32), pltpu.VMEM((_MOE_ROWS, D_EXPERT), BF16)],
        ),
        out_shape=jax.ShapeDtypeStruct((n, D_MODEL), F32),
        compiler_params=_params(2),
    )(nsub, hn, comb, x1, lw["w_gu"], lw["b_gu"], lw["w_down"], lw["b_down"], gf)


def _prep_layer(l, norm_mix_g, w_in, attn_sinks, rw_mu, rw_w0, rw_w_up, rw_a0, rw_a_up, rw_g_up,
                rw_k_k, rw_k_a, rw_r_k, rw_lnx_g, rw_lnx_b, w_branch_attn, w_branch_rwkv, w_out,
                norm_ffn_g, w_router, b_router, w_gate_up, b_gate_up, w_down, b_down):
    zeros = jnp.zeros((W_LORA, RW_WIDTH), F32)
    wa_up = jnp.concatenate([jnp.concatenate([rw_w_up[l], zeros], axis=1),
                             jnp.concatenate([zeros, rw_a_up[l]], axis=1)], axis=0)
    pad = LANES - N_EXPERTS
    return {
        "norm_mix_g": norm_mix_g[l].reshape(1, D_MODEL),
        "w_in": _bf(w_in[l]),
        "attn_sinks": attn_sinks[l].astype(F32),
        "rw_mu": rw_mu[l], "rw_w0": rw_w0[l], "rw_a0": rw_a0[l], "rw_k_k": rw_k_k[l], "rw_k_a": rw_k_a[l],
        "rw_r_k": rw_r_k[l], "rw_lnx_g": rw_lnx_g[l], "rw_lnx_b": rw_lnx_b[l],
        "wa_up": _bf(wa_up), "g_up": _bf(rw_g_up[l]),
        "w_ba": _bf(w_branch_attn[l]), "w_br": _bf(w_branch_rwkv[l]), "w_o": _bf(w_out[l]),
        "norm_ffn_g": norm_ffn_g[l].reshape(1, D_MODEL),
        "w_r": _bf(jnp.pad(w_router[l], ((0, 0), (0, pad)))),
        "b_r": jnp.pad(b_router[l], (0, pad), constant_values=-jnp.inf).reshape(1, LANES),
        "w_gu": _gu_regroup(w_gate_up[l]),
        "b_gu": b_gate_up[l].reshape(N_EXPERTS, D_EXPERT // LANES, LANES, 2).transpose(0, 1, 3, 2)
                .reshape(N_EXPERTS, 1, 2 * D_EXPERT),
        "w_down": w_down[l], "b_down": b_down[l][:, None, :],
    }


def _tile(n, pref):
    tm = pref
    while n % tm:
        tm //= 2
    return tm


def _layer(x, tables, cache_k, cache_v, s0, shift0, lw, gf, rw_sets, rw_segs):
    b, t, _ = x.shape
    n = b * t
    x2d = x.reshape(n, D_MODEL)
    tm = _tile(n, IN_TILE)
    q, k, v, zr, zg = _inproj(x2d, lw["norm_mix_g"], lw["w_in"], tables, tm)
    q3, k3, v3 = q.reshape(b, t, ATT_Q), k.reshape(b, t, ATT_KV), v.reshape(b, t, ATT_KV)
    if cache_k is None:
        o_att, k_win, v_win = _attn_prompt(q3, k3, v3, lw["attn_sinks"])
    else:
        o_att, k_win, v_win = _attn_sample(q3, k3, v3, lw["attn_sinks"],
                                           cache_k.reshape(b, WINDOW, ATT_KV), cache_v.reshape(b, WINDOW, ATT_KV))
    zr3 = zr.reshape(b, t, RW_COLS)
    o_rw, s_new = _rwkv(zr3, s0, shift0, lw, rw_sets, RW_CHUNK, rw_segs)
    tm2 = _tile(n, 512)
    x1, hn, comb, tile_counts = _mix(x2d, o_att.reshape(n, ATT_Q), o_rw.reshape(n, RW_WIDTH), zg, lw, tm2)
    y = _moe(hn, comb, tile_counts, x1, lw, gf, _tile(n, MOE_BLOCK))
    kv_shape = (b, WINDOW, N_KV_HEADS, HEAD_DIM)
    return (y.reshape(b, t, D_MODEL), k_win.reshape(kv_shape), v_win.reshape(kv_shape),
            s_new, zr3[:, t - 1, :])


def kernel(x_prompt, x_sample, cache_k, cache_v, state_wkv, state_shift, norm_mix_g, w_in, attn_sinks, rw_mu, rw_w0, rw_w_up, rw_a0, rw_a_up, rw_g_up, rw_k_k, rw_k_a, rw_r_k, rw_lnx_g, rw_lnx_b, w_branch_attn, w_branch_rwkv, w_out, norm_ffn_g, w_router, b_router, w_gate_up, b_gate_up, w_down, b_down, norm_final_g):
    assert w_in.shape[0] == 1, "single-layer trunk"
    bp, tp, _ = x_prompt.shape
    bs, ts, _ = x_sample.shape
    lw = _prep_layer(0, norm_mix_g, w_in, attn_sinks, rw_mu, rw_w0, rw_w_up, rw_a0, rw_a_up, rw_g_up,
                     rw_k_k, rw_k_a, rw_r_k, rw_lnx_g, rw_lnx_b, w_branch_attn, w_branch_rwkv, w_out,
                     norm_ffn_g, w_router, b_router, w_gate_up, b_gate_up, w_down, b_down)
    gf = norm_final_g.reshape(1, D_MODEL)

    tab_p = _rope_tables(jnp.arange(tp, dtype=jnp.int32))
    tm_s = _tile(bs * ts, IN_TILE)
    pos_s = PAST_LEN + jnp.arange(ts, dtype=jnp.int32)
    tab_s = tuple(jnp.tile(u, (tm_s // ts, 1)) for u in _rope_tables(pos_s))

    s0p = jnp.zeros((bp, RW_HEADS, RW_HEAD, RW_HEAD), state_wkv.dtype)
    sh0p = jnp.zeros((bp, RW_COLS), state_shift.dtype)
    sets_p = RW_SEQS_PER_STEP if bp % RW_SEQS_PER_STEP == 0 else 1
    segs_s = RW_CHUNK // ts
    sets_s = 2 if bs % (2 * segs_s) == 0 else 1
    yp, kp, vp, sp, shp = _layer(x_prompt, tab_p, None, None, s0p, sh0p, lw, gf, sets_p, 1)
    ys, ks, vs, ss, shs = _layer(x_sample, tab_s, cache_k[0], cache_v[0], state_wkv[0], state_shift[0],
                                 lw, gf, sets_s, segs_s)
    ex = lambda u: u[None]
    return (yp, ys, ex(kp), ex(vp), ex(sp), ex(shp), ex(ks), ex(vs), ex(ss), ex(shs))
```

```python
import functools
import math

import jax
import jax.numpy as jnp
from jax import lax
from jax.experimental import pallas as pl
from jax.experimental.pallas import tpu as pltpu

F32 = jnp.float32
BF16 = jnp.bfloat16

LANES = 128
SUBLANES = 8
VMEM_LIMIT_BYTES = 56 * 1024 * 1024

D_MODEL = 1024
HEAD_DIM = 64
N_Q_HEADS = 16
N_KV_HEADS = 4
Q_PER_KV = 4
WINDOW = 128
ROPE_THETA = 500000.0
ROPE_DIM = 16
ROPE_HALF = 8
ATTN_SCALE = HEAD_DIM ** -0.5
PAST_LEN = 16384
RW_HEAD = 64
RW_HEADS = 16
RW_PAIRS = RW_HEADS // 2
W_LORA = 64
A_LORA = 64
G_LORA = 128
LNX_EPS = 64e-5
N_EXPERTS = 32
TOP_K = 4
D_EXPERT = 1024
SWIGLU_LIMIT = 7.0
SWIGLU_ALPHA = 1.702
RMS_EPS = 1e-5
ATT_Q = N_Q_HEADS * HEAD_DIM
ATT_KV = N_KV_HEADS * HEAD_DIM
RW_WIDTH = RW_HEADS * RW_HEAD
RW_COLS = 3 * RW_WIDTH + W_LORA + A_LORA + G_LORA
GATE_COLS = 2 * D_MODEL
IN_COLS = ATT_Q + 2 * ATT_KV + RW_COLS + GATE_COLS
DECAY_SCALE = math.exp(-0.5)
RW_CHUNK = 64
IN_TILE = 512
MOE_BLOCK = 2048
RW_SEQS_PER_STEP = 4


def _nn(a, b):
    return jnp.dot(a, b, preferred_element_type=F32)


def _nt(a, b):
    return lax.dot_general(a, b, (((1,), (1,)), ((), ())), preferred_element_type=F32)


def _tn(a, b):
    return lax.dot_general(a, b, (((0,), (0,)), ((), ())), preferred_element_type=F32)


def _bf(x):
    return x.astype(BF16)


def _sigmoid(x):
    return 0.5 * jnp.tanh(0.5 * x) + 0.5


def _split2(x):
    hi = x.astype(BF16)
    lo = (x - hi.astype(F32)).astype(BF16)
    return hi, lo


def _split3(x):
    hi = x.astype(BF16)
    r1 = x - hi.astype(F32)
    mid = r1.astype(BF16)
    lo = (r1 - mid.astype(F32)).astype(BF16)
    return hi, mid, lo


def _params(n_axes):
    return pltpu.CompilerParams(
        dimension_semantics=("arbitrary",) * n_axes, vmem_limit_bytes=VMEM_LIMIT_BYTES)


_IN_CHUNK = 512


def _inproj_kernel(x_ref, g_ref, w_ref, cos_ref, sa_ref, sb_ref,
                   q_ref, k_ref, v_ref, zr_ref, zg_ref):
    x = x_ref[...]
    ms = jnp.mean(x * x, axis=-1, keepdims=True)
    h = _bf(x * lax.rsqrt(ms + RMS_EPS) * g_ref[...])
    cos, sa, sb = cos_ref[...], sa_ref[...], sb_ref[...]

    def rope(y):
        return (y * cos + pltpu.roll(y, LANES - ROPE_HALF, axis=1) * sa
                + pltpu.roll(y, ROPE_HALF, axis=1) * sb)

    def project(out_ref, col0, width, with_rope):
        for c in range(0, width, _IN_CHUNK):
            cw = min(_IN_CHUNK, width - c)
            acc = _nn(h, w_ref[:, col0 + c:col0 + c + cw])
            if with_rope:
                for j in range(0, cw, LANES):
                    out_ref[:, c + j:c + j + LANES] = rope(acc[:, j:j + LANES])
            else:
                out_ref[:, c:c + cw] = acc

    project(q_ref, 0, ATT_Q, True)
    project(k_ref, ATT_Q, ATT_KV, True)
    project(v_ref, ATT_Q + ATT_KV, ATT_KV, False)
    project(zr_ref, ATT_Q + 2 * ATT_KV, RW_COLS, False)
    project(zg_ref, ATT_Q + 2 * ATT_KV + RW_COLS, GATE_COLS, False)


def _rope_tables(positions):
    inv_freq = ROPE_THETA ** (-jnp.arange(ROPE_HALF, dtype=F32) / ROPE_HALF)
    ang = positions.astype(F32)[:, None] * inv_freq[None, :]
    cos, sin = jnp.cos(ang), jnp.sin(ang)
    t = positions.shape[0]
    one = jnp.ones((t, HEAD_DIM - ROPE_DIM), F32)
    zero = jnp.zeros((t, HEAD_DIM - ROPE_DIM), F32)
    z8 = jnp.zeros((t, ROPE_HALF), F32)
    cos_h = jnp.concatenate([cos, cos, one], axis=1)
    sa_h = jnp.concatenate([-sin, z8, zero], axis=1)
    sb_h = jnp.concatenate([z8, sin, zero], axis=1)
    two = lambda u: jnp.concatenate([u, u], axis=1)
    return two(cos_h), two(sa_h), two(sb_h)


def _inproj(x2d, g, w_bf, tables, tm):
    n = x2d.shape[0]
    cos, sa, sb = tables
    nper = cos.shape[0] // tm
    row = lambda i: (i, 0)
    tab = lambda i: (i % nper, 0)
    const = lambda i: (0, 0)
    out_shapes = [jax.ShapeDtypeStruct((n, w), F32) for w in (ATT_Q, ATT_KV, ATT_KV, RW_COLS, GATE_COLS)]
    return pl.pallas_call(
        _inproj_kernel,
        grid=(n // tm,),
        in_specs=[
            pl.BlockSpec((tm, D_MODEL), row),
            pl.BlockSpec((1, D_MODEL), const),
            pl.BlockSpec((D_MODEL, IN_COLS), const, pipeline_mode=pl.Buffered(1)),
            pl.BlockSpec((tm, LANES), tab),
            pl.BlockSpec((tm, LANES), tab),
            pl.BlockSpec((tm, LANES), tab),
        ],
        out_specs=[pl.BlockSpec((tm, w), row) for w in (ATT_Q, ATT_KV, ATT_KV, RW_COLS, GATE_COLS)],
        out_shape=out_shapes,
        compiler_params=_params(1),
    )(x2d, g, w_bf, cos, sa, sb)


def _attn_prompt_kernel(sink_ref, q_ref, kp_ref, kc_ref, vp_ref, vc_ref, o_ref, kw_ref, vw_ref):
    n = pl.program_id(1)
    w = WINDOW
    lane = lax.broadcasted_iota(jnp.int32, (w, LANES), 1)
    first_half = lane < HEAD_DIM
    qi = lax.broadcasted_iota(jnp.int32, (Q_PER_KV * w, 2 * w), 0) % w
    kj = lax.broadcasted_iota(jnp.int32, (Q_PER_KV * w, 2 * w), 1)
    first_key = jnp.where(n > 0, 0, w)
    band4 = (kj > qi) & (kj <= qi + w) & (kj >= first_key)

    kw_ref[...] = kc_ref[...]
    vw_ref[...] = vc_ref[...]

    lane2 = lax.broadcasted_iota(jnp.int32, (2 * w, LANES), 1)
    kvs = range(N_KV_HEADS)
    k2, v2 = [], []
    for kv in kvs:
        gs = slice((kv // 2) * LANES, (kv // 2 + 1) * LANES)
        mine = (lane2 < HEAD_DIM) if kv % 2 == 0 else (lane2 >= HEAD_DIM)
        k2.append(_bf(jnp.concatenate([kp_ref[:, gs], kc_ref[:, gs]], axis=0)))
        v2.append(_bf(jnp.where(mine, jnp.concatenate([vp_ref[:, gs], vc_ref[:, gs]], axis=0), 1.0)))
    qs, sink = [], []
    for kv in kvs:
        jh = kv % 2
        keep = first_half if jh == 0 else jnp.logical_not(first_half)
        rows, sinks = [], []
        for gq in range(Q_PER_KV):
            hq = kv * Q_PER_KV + gq
            qg = q_ref[:, (hq // 2) * LANES:(hq // 2 + 1) * LANES]
            if hq % 2 != jh:
                qg = pltpu.roll(qg, HEAD_DIM, axis=1)
            rows.append(jnp.where(keep, qg * ATTN_SCALE, 0.0))
            sinks.append(jnp.full((w, 1), sink_ref[hq], F32))
        qs.append(_bf(jnp.concatenate(rows, axis=0)))
        sink.append(jnp.concatenate(sinks, axis=0))
    s = [jnp.where(band4, _nt(q_, k_), -jnp.inf) for q_, k_ in zip(qs, k2)]
    m = [jnp.maximum(jnp.max(s_, axis=-1, keepdims=True), sk) for s_, sk in zip(s, sink)]
    e = [_bf(jnp.exp(s_ - m_)) for s_, m_ in zip(s, m)]
    e_sink = [jnp.exp(sk - m_) for sk, m_ in zip(sink, m)]
    pv = [_nn(e_, v_) for e_, v_ in zip(e, v2)]
    for kv in kvs:
        jh = kv % 2
        for go in range(Q_PER_KV // 2):
            ra, rb = slice(2 * go * w, (2 * go + 1) * w), slice((2 * go + 1) * w, (2 * go + 2) * w)
            pa, pb = pv[kv][ra], pv[kv][rb]
            pa_r, pb_r = pltpu.roll(pa, HEAD_DIM, axis=1), pltpu.roll(pb, HEAD_DIM, axis=1)
            if jh == 0:
                oa, ob = pa / (pa_r + e_sink[kv][ra]), pb_r / (pb + e_sink[kv][rb])
            else:
                oa, ob = pa_r / (pa + e_sink[kv][ra]), pb / (pb_r + e_sink[kv][rb])
            og = kv * (Q_PER_KV // 2) + go
            o_ref[:, og * LANES:(og + 1) * LANES] = jnp.where(first_half, oa, ob).astype(o_ref.dtype)


def _attn_prompt(q, k, v, sinks):
    b, t, _ = q.shape
    nb = t // WINDOW
    cur = lambda bi, n: (bi, n, 0)
    prev = lambda bi, n: (bi, jnp.maximum(n - 1, 0), 0)
    win = lambda bi, n: (bi, 0, 0)
    return pl.pallas_call(
        _attn_prompt_kernel,
        grid=(b, nb),
        in_specs=[
            pl.BlockSpec(memory_space=pltpu.SMEM),
            pl.BlockSpec((None, WINDOW, ATT_Q), cur),
            pl.BlockSpec((None, WINDOW, ATT_KV), prev),
            pl.BlockSpec((None, WINDOW, ATT_KV), cur),
            pl.BlockSpec((None, WINDOW, ATT_KV), prev),
            pl.BlockSpec((None, WINDOW, ATT_KV), cur),
        ],
        out_specs=[
            pl.BlockSpec((None, WINDOW, ATT_Q), cur),
            pl.BlockSpec((None, WINDOW, ATT_KV), win),
            pl.BlockSpec((None, WINDOW, ATT_KV), win),
        ],
        out_shape=[
            jax.ShapeDtypeStruct((b, t, ATT_Q), BF16),
            jax.ShapeDtypeStruct((b, WINDOW, ATT_KV), F32),
            jax.ShapeDtypeStruct((b, WINDOW, ATT_KV), F32),
        ],
        compiler_params=_params(2),
    )(sinks, q, k, k, v, v)


_SAMPLE_BT = 8


def _attn_sample_kernel(sink_ref, q_ref, k_ref, v_ref, ck_ref, cv_ref, o_ref, nk_ref, nv_ref, *, t):
    w = WINDOW
    rows_per_grp = 2 * Q_PER_KV * t
    lane = lax.broadcasted_iota(jnp.int32, (t, LANES), 1)
    first_half = lane < HEAD_DIM
    r_c = lax.broadcasted_iota(jnp.int32, (rows_per_grp, w), 0) % t
    c_c = lax.broadcasted_iota(jnp.int32, (rows_per_grp, w), 1)
    mask_c = c_c > r_c
    r_n = lax.broadcasted_iota(jnp.int32, (rows_per_grp, t), 0) % t
    c_n = lax.broadcasted_iota(jnp.int32, (rows_per_grp, t), 1)
    mask_n = c_n <= r_n

    for bi in range(_SAMPLE_BT):
        nk_ref[bi, 0:w - t, :] = ck_ref[bi, t:w, :]
        nk_ref[bi, w - t:w, :] = k_ref[bi]
        nv_ref[bi, 0:w - t, :] = cv_ref[bi, t:w, :]
        nv_ref[bi, w - t:w, :] = v_ref[bi]

    chains = [(bi, grp) for bi in range(_SAMPLE_BT) for grp in range(ATT_KV // LANES)]
    gsl = lambda grp: slice(grp * LANES, (grp + 1) * LANES)
    kc = [_bf(ck_ref[bi, :, gsl(grp)]) for bi, grp in chains]
    vc = [_bf(cv_ref[bi, :, gsl(grp)]) for bi, grp in chains]
    kn = [_bf(k_ref[bi, :, gsl(grp)]) for bi, grp in chains]
    vn = [_bf(v_ref[bi, :, gsl(grp)]) for bi, grp in chains]
    sink_rows = [[jnp.full((t, 1), sink_ref[(grp * 2 + jh) * Q_PER_KV + gq], F32)
                  for jh in range(2) for gq in range(Q_PER_KV)] for grp in range(ATT_KV // LANES)]
    sink_g = [jnp.concatenate(r_, axis=0) for r_ in sink_rows]
    qs = []
    for bi, grp in chains:
        rows = []
        for jh in range(2):
            keep = first_half if jh == 0 else jnp.logical_not(first_half)
            for gq in range(Q_PER_KV):
                hq = (grp * 2 + jh) * Q_PER_KV + gq
                qg = q_ref[bi, :, (hq // 2) * LANES:(hq // 2 + 1) * LANES]
                if hq % 2 != jh:
                    qg = pltpu.roll(qg, HEAD_DIM, axis=1)
                rows.append(jnp.where(keep, qg * ATTN_SCALE, 0.0))
        qs.append(_bf(jnp.concatenate(rows, axis=0)))
    sink = [sink_g[grp] for _, grp in chains]
    s_c = [jnp.where(mask_c, _nt(q_, k_), -jnp.inf) for q_, k_ in zip(qs, kc)]
    s_n = [jnp.where(mask_n, _nt(q_, k_), -jnp.inf) for q_, k_ in zip(qs, kn)]
    m = [jnp.maximum(jnp.maximum(jnp.max(a, axis=-1, keepdims=True), jnp.max(b_, axis=-1, keepdims=True)), sk)
         for a, b_, sk in zip(s_c, s_n, sink)]
    e_c = [jnp.exp(a - m_) for a, m_ in zip(s_c, m)]
    e_n = [jnp.exp(a - m_) for a, m_ in zip(s_n, m)]
    denom = [jnp.sum(a, axis=-1, keepdims=True) + jnp.sum(b_, axis=-1, keepdims=True) + jnp.exp(sk - m_)
             for a, b_, sk, m_ in zip(e_c, e_n, sink, m)]
    pv = [_nn(_bf(a / d_), va) + _nn(_bf(b_ / d_), vb)
          for a, b_, d_, va, vb in zip(e_c, e_n, denom, vc, vn)]
    for (bi, grp), pv_ in zip(chains, pv):
        for jh in range(2):
            for go in range(Q_PER_KV // 2):
                r0 = (jh * Q_PER_KV + 2 * go) * t
                pa, pb = pv_[r0:r0 + t], pv_[r0 + t:r0 + 2 * t]
                if jh == 0:
                    pb = pltpu.roll(pb, HEAD_DIM, axis=1)
                else:
                    pa = pltpu.roll(pa, HEAD_DIM, axis=1)
                og = (grp * 2 + jh) * (Q_PER_KV // 2) + go
                o_ref[bi, :, og * LANES:(og + 1) * LANES] = jnp.where(first_half, pa, pb)


def _attn_sample(q, k, v, sinks, cache_k, cache_v):
    b, t, _ = q.shape
    bt = _SAMPLE_BT
    blk = lambda i: (i, 0, 0)
    return pl.pallas_call(
        functools.partial(_attn_sample_kernel, t=t),
        grid=(b // bt,),
        in_specs=[
            pl.BlockSpec(memory_space=pltpu.SMEM),
            pl.BlockSpec((bt, t, ATT_Q), blk),
            pl.BlockSpec((bt, t, ATT_KV), blk),
            pl.BlockSpec((bt, t, ATT_KV), blk),
            pl.BlockSpec((bt, WINDOW, ATT_KV), blk),
            pl.BlockSpec((bt, WINDOW, ATT_KV), blk),
        ],
        out_specs=[
            pl.BlockSpec((bt, t, ATT_Q), blk),
            pl.BlockSpec((bt, WINDOW, ATT_KV), blk),
            pl.BlockSpec((bt, WINDOW, ATT_KV), blk),
        ],
        out_shape=[
            jax.ShapeDtypeStruct((b, t, ATT_Q), F32),
            jax.ShapeDtypeStruct((b, WINDOW, ATT_KV), F32),
            jax.ShapeDtypeStruct((b, WINDOW, ATT_KV), F32),
        ],
        compiler_params=_params(1),
    )(sinks, q, k, v, cache_k, cache_v)


def _rwkv_kernel(z_ref, s0_ref, shift_ref, mu_ref, w0_ref, wa_up_ref, a0_ref, g_up_ref,
                 kk_ref, ka_ref, rk_ref, lng_ref, lnb_ref, o_ref, sout_ref, s_ref, prev_ref, *, nb, c, nseg):
    tseg = c // nseg
    rows_all = nb * c
    step = pl.program_id(1)
    pairs = range(RW_PAIRS)
    sls = [slice(p * LANES, (p + 1) * LANES) for p in pairs]
    rsl = [slice(j * c, (j + 1) * c) for j in range(nb)]

    @pl.when(step == 0)
    def _():
        zero = jnp.zeros((RW_HEAD, RW_HEAD), F32)
        for q in range(nb * nseg):
            for p in pairs:
                top = jnp.concatenate([s0_ref[q, 2 * p], zero], axis=1)
                bot = jnp.concatenate([zero, s0_ref[q, 2 * p + 1]], axis=1)
                s_ref[q, p] = jnp.concatenate([top, bot], axis=0)
        if nseg == 1:
            prev_ref[...] = shift_ref[...].reshape(nb, RW_COLS)

    z = z_ref[...].reshape(rows_all, RW_COLS)
    row1 = lax.broadcasted_iota(jnp.int32, (rows_all, 1), 0)
    zprev = pltpu.roll(z, 1, axis=0)
    if nseg == 1:
        for j in range(nb):
            zprev = jnp.where(row1 == j * c, prev_ref[j:j + 1, :], zprev)
            prev_ref[j:j + 1, :] = z[(j + 1) * c - 1:(j + 1) * c, :]
    else:
        zprev = jnp.where(row1 % tseg == 0, shift_ref[...].reshape(rows_all, RW_COLS), zprev)
    zs = z + (zprev - z) * mu_ref[...]

    w3 = 3 * RW_WIDTH
    r, k, v = zs[:, 0:RW_WIDTH], zs[:, RW_WIDTH:2 * RW_WIDTH], zs[:, 2 * RW_WIDTH:w3]
    xwa = zs[:, w3:w3 + LANES]
    xg = zs[:, w3 + LANES:w3 + 2 * LANES]
    lane_all = lax.broadcasted_iota(jnp.int32, (rows_all, LANES), 1)
    lora = _nn(_bf(jnp.where(lane_all < RW_HEAD, jnp.tanh(xwa), xwa)), wa_up_ref[...])
    lane = lax.broadcasted_iota(jnp.int32, (c, LANES), 1)
    head0 = lane < RW_HEAD
    lw = -DECAY_SCALE * _sigmoid(w0_ref[...] + lora[:, 0:RW_WIDTH])
    a_sig = _sigmoid(a0_ref[...] + lora[:, RW_WIDTH:2 * RW_WIDTH])
    g = _nn(_bf(_sigmoid(xg)), g_up_ref[...])
    kk = k * kk_ref[...]
    k = k * (1.0 + (a_sig - 1.0) * ka_ref[...])
    rkr = r * k * rk_ref[...]

    ti = lax.broadcasted_iota(jnp.int32, (rows_all, rows_all), 0)
    tj = lax.broadcasted_iota(jnp.int32, (rows_all, rows_all), 1)
    same_seq = (ti // tseg) == (tj // tseg)
    tri = jnp.where((tj <= ti) & same_seq, 1.0, 0.0).astype(BF16)
    lw3 = _split3(lw)
    cum = _nn(tri, lw3[0]) + _nn(tri, lw3[1]) + _nn(tri, lw3[2])
    if nseg == 1:
        ends = [jnp.broadcast_to(cum[(j + 1) * c - 1:(j + 1) * c, :], (c, RW_WIDTH)) for j in range(nb)]
        c_end = ends[0] if nb == 1 else jnp.concatenate(ends, axis=0)
    else:
        seq1 = jnp.where(same_seq, 1.0, 0.0).astype(BF16)
        c_end = _nn(seq1, lw3[0]) + _nn(seq1, lw3[1]) + _nn(seq1, lw3[2])
    e_c, e_ci, e_cm = jnp.exp(cum), jnp.exp(-cum), jnp.exp(cum - lw)
    e_ce, w_end = jnp.exp(c_end - cum), jnp.exp(c_end)

    gi = lax.broadcasted_iota(jnp.int32, (2 * LANES, LANES), 0) % LANES
    gj = lax.broadcasted_iota(jnp.int32, (2 * LANES, LANES), 1)
    ones2 = jnp.where((gi // RW_HEAD) == (gj // RW_HEAD), 1.0, 0.0).astype(BF16)
    bi_ = lax.broadcasted_iota(jnp.int32, (LANES, LANES), 0)
    bj_ = lax.broadcasted_iota(jnp.int32, (LANES, LANES), 1)
    same_head = (bi_ // RW_HEAD) == (bj_ // RW_HEAD)
    ci = lax.broadcasted_iota(jnp.int32, (c, 2 * c), 0)
    cj = lax.broadcasted_iota(jnp.int32, (c, 2 * c), 1)
    cjm = cj % c
    seq_ok = (ci // tseg) == (cjm // tseg)
    strict = (cjm < ci) & seq_ok
    incl = (cjm <= ci) & seq_ok
    eye_cat = jnp.where(cjm == ci, 1.0, 0.0)
    left = cj < c

    def seg_sum(x):
        hi, lo = _split2(x)
        return _nn(jnp.concatenate([hi, lo], axis=1), ones2)

    def rows2(x):
        return jnp.concatenate([jnp.where(head0, x, 0.0), jnp.where(head0, 0.0, x)], axis=0)

    def bd(cat):
        return _bf(jnp.concatenate([jnp.where(left, cat, 0.0), jnp.where(left, 0.0, cat)], axis=0))

    def pair_mm(cat, x):
        return _nn(_bf(cat), _bf(rows2(x)))

    def seg_sums(xs):
        tot = seg_sum(jnp.concatenate(xs, axis=0))
        return [tot[i * c:(i + 1) * c] for i in range(len(xs))]

    segs = range(nseg)
    rs = [slice(q * tseg, (q + 1) * tseg) for q in segs]

    def run(chains):
        gs = [(rsl[j], sls[p]) for j, p in chains]
        idx = range(len(chains))
        ss = seg_sums([kk[s] * kk[s] for s in gs])
        kkn = [kk[s] * lax.rsqrt(jnp.maximum(q, 1e-24)) for s, q in zip(gs, ss)]
        bv = [n_ * a_sig[s] for s, n_ in zip(gs, kkn)]
        rt = [r[s] * e_c[s] for s in gs]
        kt = [k[s] * e_ci[s] for s in gs]
        at = [-n_ * e_cm[s] for s, n_ in zip(gs, kkn)]
        bt = [b_ * e_ci[s] for s, b_ in zip(gs, bv)]
        bh = [b_ * e_ce[s] for s, b_ in zip(gs, bv)]
        kh = [k[s] * e_ce[s] for s in gs]
        vv = [v[s] for s in gs]

        ar = [_bf(jnp.concatenate([a_, r_], axis=0)) for a_, r_ in zip(at, rt)]
        xbk = [_nt(x, _bf(jnp.concatenate([rows2(b_), rows2(k_)], axis=0)))
               for x, b_, k_ in zip(ar, bt, kt)]
        l_ab = [jnp.where(strict, x[0:c, 0:2 * c], 0.0) for x in xbk]
        l_ak = [jnp.where(strict, x[0:c, 2 * c:4 * c], 0.0) for x in xbk]
        m_rb = [jnp.where(incl, x[c:2 * c, 0:2 * c], 0.0) for x in xbk]
        m_rk = [jnp.where(incl, x[c:2 * c, 2 * c:4 * c], 0.0) for x in xbk]

        t_inv = [l + eye_cat for l in l_ab]
        pw = l_ab
        pw_bd = [bd(x) for x in pw]
        for _ in range(int(math.log2(tseg)) - 1):
            pw = [_nn(_bf(x), xb_) for x, xb_ in zip(pw, pw_bd)]
            pw_bd = [bd(x) for x in pw]
            t_inv = [t + _nn(_bf(t), xb_) for t, xb_ in zip(t_inv, pw_bd)]

        lvy = [pair_mm(jnp.concatenate([l, m], axis=0), v_) for l, m, v_ in zip(l_ak, m_rk, vv)]
        y1 = [x[c:2 * c] for x in lvy]
        au = [_nn(_bf(t), _bf(jnp.concatenate([rows2(a_), rows2(x[0:c])], axis=1)))
              for t, a_, x in zip(t_inv, at, lvy)]
        a_hat = [x[:, 0:LANES] for x in au]
        u0 = [x[:, LANES:2 * LANES] for x in au]

        s_old = [[s_ref[j * nseg + q, p] for q in segs] for j, p in chains]
        pp = [[_nt(_bf(jnp.concatenate([a_hat[i][rs[q]], rt[i][rs[q]]], axis=0)), _bf(s_old[i][q]))
               for q in segs] for i in idx]
        u = [jnp.concatenate([pp[i][q][0:tseg] for q in segs], axis=0) + u0[i] for i in idx]
        y0 = [jnp.concatenate([pp[i][q][tseg:2 * tseg] for q in segs], axis=0) for i in idx]
        y = [y0[i] + pair_mm(m_rb[i], u[i]) + y1[i] for i in idx]
        for i in idx:
            for q in segs:
                upd = _tn(_bf(jnp.concatenate([u[i][rs[q]], vv[i][rs[q]]], axis=0)),
                          _bf(jnp.concatenate([bh[i][rs[q]], kh[i][rs[q]]], axis=0)))
                j, p = chains[i]
                row0 = (j * c + q * tseg) % w_end.shape[0]
                w_q = w_end[row0:row0 + 1, sls[p]]
                s_ref[j * nseg + q, p] = s_old[i][q] * w_q + jnp.where(same_head, upd, 0.0)

        mean = [x * (1.0 / RW_HEAD) for x in seg_sums(y)]
        d = [x - m for x, m in zip(y, mean)]
        var = [x * (1.0 / RW_HEAD) for x in seg_sums([x * x for x in d])]
        bonus = [x * v_ for x, v_ in zip(seg_sums([rkr[s] for s in gs]), vv)]
        for i in idx:
            j, p = chains[i]
            yn = d[i] * lax.rsqrt(var[i] + LNX_EPS) * lng_ref[:, sls[p]] + lnb_ref[:, sls[p]]
            o_ref[j, :, sls[p]] = ((yn + bonus[i]) * g[gs[i]]).astype(o_ref.dtype)

    run([(j, p) for j in range(nb) for p in pairs])

    @pl.when(step == pl.num_programs(1) - 1)
    def _():
        for q in range(nb * nseg):
            for p in pairs:
                tile = s_ref[q, p]
                sout_ref[q, 2 * p] = tile[0:RW_HEAD, 0:RW_HEAD]
                sout_ref[q, 2 * p + 1] = tile[RW_HEAD:2 * RW_HEAD, RW_HEAD:2 * RW_HEAD]


def _rwkv(zr, s0, shift0, lw, nb, c, nseg):
    b, t, _ = zr.shape
    if nseg == 1:
        ngrp, nchunk = b // nb, t // c
        z3 = zr
        shift = shift0.reshape(b, 1, RW_COLS)
        shift_spec = pl.BlockSpec((nb, 1, RW_COLS), lambda bi, i: (bi, 0, 0))
    else:
        assert c == nseg * t
        nsets = b // nseg
        ngrp, nchunk = nsets // nb, 1
        z3 = zr.reshape(nsets, c, RW_COLS)
        shift = jnp.pad(shift0[:, None, :], ((0, 0), (0, t - 1), (0, 0))).reshape(nsets, c, RW_COLS)
        shift_spec = pl.BlockSpec((nb, c, RW_COLS), lambda bi, i: (bi, 0, 0))
    vec = lambda name: lw[name].reshape(1, -1).astype(F32)
    cst = lambda bi, i: (0, 0)
    vspec = lambda wd: pl.BlockSpec((1, wd), cst)
    state_spec = pl.BlockSpec((nb * nseg, RW_HEADS, RW_HEAD, RW_HEAD), lambda bi, i: (bi, 0, 0, 0))
    o, s_new = pl.pallas_call(
        functools.partial(_rwkv_kernel, nb=nb, c=c, nseg=nseg),
        grid=(ngrp, nchunk),
        in_specs=[
            pl.BlockSpec((nb, c, RW_COLS), lambda bi, i: (bi, i, 0)),
            state_spec,
            shift_spec,
            vspec(RW_COLS), vspec(RW_WIDTH),
            pl.BlockSpec((LANES, 2 * RW_WIDTH), cst),
            vspec(RW_WIDTH),
            pl.BlockSpec((G_LORA, RW_WIDTH), cst),
            vspec(RW_WIDTH), vspec(RW_WIDTH), vspec(RW_WIDTH), vspec(RW_WIDTH), vspec(RW_WIDTH),
        ],
        out_specs=[
            pl.BlockSpec((nb, c, RW_WIDTH), lambda bi, i: (bi, i, 0)),
            state_spec,
        ],
        out_shape=[
            jax.ShapeDtypeStruct((ngrp * nb, nchunk * c, RW_WIDTH), BF16),
            jax.ShapeDtypeStruct((b, RW_HEADS, RW_HEAD, RW_HEAD), F32),
        ],
        scratch_shapes=[pltpu.VMEM((nb * nseg, RW_PAIRS, LANES, LANES), F32), pltpu.VMEM((nb, RW_COLS), F32)],
        compiler_params=_params(2),
    )(z3, s0, shift, vec("rw_mu"), vec("rw_w0"), lw["wa_up"], vec("rw_a0"),
      lw["g_up"], vec("rw_k_k"), vec("rw_k_a"), vec("rw_r_k"), vec("rw_lnx_g"), vec("rw_lnx_b"))
    return o.reshape(b, t, RW_WIDTH), s_new


def _mix_kernel(x_ref, oa_ref, or_ref, zg_ref, wba_ref, wbr_ref, wo_ref, g_ref, wr_ref, br_ref,
                x1_ref, hn_ref, comb_ref, cnt_ref):
    ya = _nn(_bf(oa_ref[...]), wba_ref[...])
    yr = _nn(_bf(or_ref[...]), wbr_ref[...])
    merged = _sigmoid(zg_ref[:, 0:D_MODEL]) * ya + _sigmoid(zg_ref[:, D_MODEL:2 * D_MODEL]) * yr
    x1 = x_ref[...] + _nn(_bf(merged), wo_ref[...])
    x1_ref[...] = x1
    ms = jnp.mean(x1 * x1, axis=-1, keepdims=True)
    hn = x1 * lax.rsqrt(ms + RMS_EPS) * g_ref[...]
    hn_ref[...] = _bf(hn)
    logits = _nn(_bf(hn), wr_ref[...]) + br_ref[...]
    lane = lax.broadcasted_iota(jnp.int32, logits.shape, 1).astype(F32)
    work = logits
    top = None
    for _ in range(TOP_K):
        m = jnp.max(work, axis=-1, keepdims=True)
        if top is None:
            top = m
        idx = jnp.min(jnp.where(work == m, lane, float(LANES)), axis=-1, keepdims=True)
        work = jnp.where(lane == idx, -jnp.inf, work)
    e = jnp.where(work != logits, jnp.exp(logits - top), 0.0)
    comb = e / jnp.sum(e, axis=-1, keepdims=True)
    comb_ref[...] = comb
    cnt_ref[...] = jnp.sum(jnp.where(comb > 0.0, 1.0, 0.0), axis=0, keepdims=True)


def _mix(x2d, oa, orw, zg, lw, tm):
    n = x2d.shape[0]
    row = lambda i: (i, 0)
    cst = lambda i: (0, 0)
    wspec = pl.BlockSpec((D_MODEL, D_MODEL), cst)
    return pl.pallas_call(
        _mix_kernel,
        grid=(n // tm,),
        in_specs=[
            pl.BlockSpec((tm, D_MODEL), row), pl.BlockSpec((tm, D_MODEL), row), pl.BlockSpec((tm, D_MODEL), row),
            pl.BlockSpec((tm, GATE_COLS), row),
            wspec, wspec, wspec,
            pl.BlockSpec((1, D_MODEL), cst),
            pl.BlockSpec((D_MODEL, LANES), cst),
            pl.BlockSpec((1, LANES), cst),
        ],
        out_specs=[pl.BlockSpec((tm, D_MODEL), row), pl.BlockSpec((tm, D_MODEL), row),
                   pl.BlockSpec((tm, LANES), row), pl.BlockSpec((None, 1, LANES), lambda i: (i, 0, 0))],
        out_shape=[jax.ShapeDtypeStruct((n, D_MODEL), F32), jax.ShapeDtypeStruct((n, D_MODEL), BF16),
                   jax.ShapeDtypeStruct((n, LANES), F32), jax.ShapeDtypeStruct((n // tm, 1, LANES), F32)],
        compiler_params=_params(1),
    )(x2d, oa, orw, zg, lw["w_ba"], lw["w_br"], lw["w_o"], lw["norm_ffn_g"], lw["w_r"], lw["b_r"])


_GU_GROUP = 2 * LANES


def _gu_regroup_kernel(w_ref, o_ref):
    src = lax.broadcasted_iota(jnp.int32, (_GU_GROUP, _GU_GROUP), 0)
    dst = lax.broadcasted_iota(jnp.int32, (_GU_GROUP, _GU_GROUP), 1)
    want = jnp.where(dst < LANES, 2 * dst, 2 * (dst - LANES) + 1)
    perm = jnp.where(src == want, 1.0, 0.0).astype(BF16)
    for j in range(0, 2 * D_EXPERT, _GU_GROUP):
        o_ref[:, j:j + _GU_GROUP] = _nn(_bf(w_ref[:, j:j + _GU_GROUP]), perm).astype(o_ref.dtype)


def _gu_regroup(w_gate_up):
    e, d, n = w_gate_up.shape
    tk = 512
    spec = pl.BlockSpec((None, tk, n), lambda i, j: (i, j, 0))
    return pl.pallas_call(
        _gu_regroup_kernel,
        grid=(e, d // tk),
        in_specs=[spec],
        out_specs=spec,
        out_shape=jax.ShapeDtypeStruct((e, d, n), BF16),
        compiler_params=_params(2),
    )(w_gate_up)


_MOE_DOMAIN = 1024
_MOE_ROWS = 160
_RANK_CHUNK = 256


def _moe_kernel(nsub_ref, hn_ref, comb_ref, x1_ref, wgu_ref, bgu_ref, wd_ref, bd_ref, gf_ref, out_ref,
                rank_ref, rank_t_ref, comb_t_ref, act_ref):
    blk, e = pl.program_id(0), pl.program_id(1)
    tb = hn_ref.shape[0]
    dom, rows = min(_MOE_DOMAIN, tb), _MOE_ROWS
    ndom = tb // dom

    @pl.when(e == 0)
    def _():
        out_ref[...] = x1_ref[...]
        ci = lax.broadcasted_iota(jnp.int32, (_RANK_CHUNK, _RANK_CHUNK), 0)
        cj = lax.broadcasted_iota(jnp.int32, (_RANK_CHUNK, _RANK_CHUNK), 1)
        before = jnp.where(cj < ci, 1.0, 0.0).astype(BF16)
        for d0 in range(0, tb, dom):
            seen = jnp.zeros((1, LANES), F32)
            for c0 in range(d0, d0 + dom, _RANK_CHUNK):
                routed = comb_ref[c0:c0 + _RANK_CHUNK, :] > 0.0
                hot = jnp.where(routed, 1.0, 0.0)
                rank_ref[c0:c0 + _RANK_CHUNK, :] = jnp.where(routed, _nn(before, _bf(hot)) + seen, -1.0)
                seen = seen + jnp.sum(hot, axis=0, keepdims=True)
        rank_t_ref[...] = rank_ref[...].T
        comb_t_ref[...] = comb_ref[...].T

    slot_r = lax.broadcasted_iota(jnp.int32, (rows, dom), 0).astype(F32)

    doms = [slice(d * dom, (d + 1) * dom) for d in range(ndom)]
    rank_rows = [rank_t_ref[pl.ds(e, 1), ds_] for ds_ in doms]
    w_rows_all = [comb_t_ref[pl.ds(e, 1), ds_] for ds_ in doms]
    trips = [nsub_ref[(blk * ndom + d) * N_EXPERTS + e] for d in range(ndom)]
    trip = trips[0]
    for t_ in trips[1:]:
        trip = jnp.maximum(trip, t_)

    def sub_tiles(s, carry):
        base = (s * rows).astype(F32)
        for d, ds_ in enumerate(doms):
            hit = rank_rows[d] - base == slot_r
            gather = jnp.where(hit, 1.0, 0.0).astype(BF16)
            x = _bf(_nn(gather, hn_ref[ds_, :]))
            w_rows = jnp.sum(jnp.where(hit, w_rows_all[d], 0.0), axis=1, keepdims=True)
            for j in range(D_EXPERT // LANES):
                gs = slice(j * _GU_GROUP, (j + 1) * _GU_GROUP)
                gu = _nn(x, wgu_ref[:, gs]) + bgu_ref[:, gs]
                glu = jnp.minimum(gu[:, 0:LANES], SWIGLU_LIMIT)
                lin = jnp.clip(gu[:, LANES:_GU_GROUP], -SWIGLU_LIMIT, SWIGLU_LIMIT)
                act_ref[d, :, j * LANES:(j + 1) * LANES] = _bf(glu * _sigmoid(SWIGLU_ALPHA * glu) * (lin + 1.0))
            y = (_nn(act_ref[d], _bf(wd_ref[...])) + bd_ref[...]) * w_rows
            out_ref[ds_, :] += _tn(gather, _bf(y))
        return carry

    lax.fori_loop(0, trip, sub_tiles, 0)

    @pl.when(e == N_EXPERTS - 1)
    def _():
        for r0 in range(0, tb, _RANK_CHUNK):
            xo = out_ref[r0:r0 + _RANK_CHUNK, :]
            ms = jnp.mean(xo * xo, axis=-1, keepdims=True)
            out_ref[r0:r0 + _RANK_CHUNK, :] = xo * lax.rsqrt(ms + RMS_EPS) * gf_ref[...]


def _moe(hn, comb, tile_counts, x1, lw, gf, tb):
    n = hn.shape[0]
    nblk = n // tb
    ndomains = n // min(_MOE_DOMAIN, tb)
    counts = tile_counts.reshape(ndomains, -1, LANES).sum(axis=1)[:, :N_EXPERTS].astype(jnp.int32)
    nsub = ((counts + _MOE_ROWS - 1) // _MOE_ROWS).reshape(-1)
    row = lambda i, e, ns: (i, 0)
    ex = lambda i, e, ns: (e, 0, 0)
    return pl.pallas_call(
        _moe_kernel,
        grid_spec=pltpu.PrefetchScalarGridSpec(
            num_scalar_prefetch=1,
            grid=(nblk, N_EXPERTS),
            in_specs=[
                pl.BlockSpec((tb, D_MODEL), row), pl.BlockSpec((tb, LANES), row),
                pl.BlockSpec((tb, D_MODEL), row, pipeline_mode=pl.Buffered(1)),
                pl.BlockSpec((None, D_MODEL, 2 * D_EXPERT), ex), pl.BlockSpec((None, 1, 2 * D_EXPERT), ex),
                pl.BlockSpec((None, D_EXPERT, D_MODEL), ex), pl.BlockSpec((None, 1, D_MODEL), ex),
                pl.BlockSpec((1, D_MODEL), lambda i, e, ns: (0, 0)),
            ],
            out_specs=pl.BlockSpec((tb, D_MODEL), row),
            scratch_shapes=[pltpu.VMEM((tb, LANES), F32), pltpu.VMEM((LANES, tb), F32),
                            pltpu.VMEM((LANES, tb), F32),
                            pltpu.VMEM((tb // min(_MOE_DOMAIN, tb), _MOE_ROWS, D_EXPERT), BF16)],
        ),
        out_shape=jax.ShapeDtypeStruct((n, D_MODEL), F32),
        compiler_params=_params(2),
    )(nsub, hn, comb, x1, lw["w_gu"], lw["b_gu"], lw["w_down"], lw["b_down"], gf)


def _prep_layer(l, norm_mix_g, w_in, attn_sinks, rw_mu, rw_w0, rw_w_up, rw_a0, rw_a_up, rw_g_up,
                rw_k_k, rw_k_a, rw_r_k, rw_lnx_g, rw_lnx_b, w_branch_attn, w_branch_rwkv, w_out,
                norm_ffn_g, w_router, b_router, w_gate_up, b_gate_up, w_down, b_down):
    zeros = jnp.zeros((W_LORA, RW_WIDTH), F32)
    wa_up = jnp.concatenate([jnp.concatenate([rw_w_up[l], zeros], axis=1),
                             jnp.concatenate([zeros, rw_a_up[l]], axis=1)], axis=0)
    pad = LANES - N_EXPERTS
    return {
        "norm_mix_g": norm_mix_g[l].reshape(1, D_MODEL),
        "w_in": _bf(w_in[l]),
        "attn_sinks": attn_sinks[l].astype(F32),
        "rw_mu": rw_mu[l], "rw_w0": rw_w0[l], "rw_a0": rw_a0[l], "rw_k_k": rw_k_k[l], "rw_k_a": rw_k_a[l],
        "rw_r_k": rw_r_k[l], "rw_lnx_g": rw_lnx_g[l], "rw_lnx_b": rw_lnx_b[l],
        "wa_up": _bf(wa_up), "g_up": _bf(rw_g_up[l]),
        "w_ba": _bf(w_branch_attn[l]), "w_br": _bf(w_branch_rwkv[l]), "w_o": _bf(w_out[l]),
        "norm_ffn_g": norm_ffn_g[l].reshape(1, D_MODEL),
        "w_r": _bf(jnp.pad(w_router[l], ((0, 0), (0, pad)))),
        "b_r": jnp.pad(b_router[l], (0, pad), constant_values=-jnp.inf).reshape(1, LANES),
        "w_gu": _gu_regroup(w_gate_up[l]),
        "b_gu": b_gate_up[l].reshape(N_EXPERTS, D_EXPERT // LANES, LANES, 2).transpose(0, 1, 3, 2)
                .reshape(N_EXPERTS, 1, 2 * D_EXPERT),
        "w_down": w_down[l], "b_down": b_down[l][:, None, :],
    }


def _tile(n, pref):
    tm = pref
    while n % tm:
        tm //= 2
    return tm


def _layer(x, tables, cache_k, cache_v, s0, shift0, lw, gf, rw_sets, rw_segs):
    b, t, _ = x.shape
    n = b * t
    x2d = x.reshape(n, D_MODEL)
    tm = _tile(n, IN_TILE)
    q, k, v, zr, zg = _inproj(x2d, lw["norm_mix_g"], lw["w_in"], tables, tm)
    q3, k3, v3 = q.reshape(b, t, ATT_Q), k.reshape(b, t, ATT_KV), v.reshape(b, t, ATT_KV)
    if cache_k is None:
        o_att, k_win, v_win = _attn_prompt(q3, k3, v3, lw["attn_sinks"])
    else:
        o_att, k_win, v_win = _attn_sample(q3, k3, v3, lw["attn_sinks"],
                                           cache_k.reshape(b, WINDOW, ATT_KV), cache_v.reshape(b, WINDOW, ATT_KV))
    zr3 = zr.reshape(b, t, RW_COLS)
    o_rw, s_new = _rwkv(zr3, s0, shift0, lw, rw_sets, RW_CHUNK, rw_segs)
    tm2 = _tile(n, 512)
    x1, hn, comb, tile_counts = _mix(x2d, o_att.reshape(n, ATT_Q), o_rw.reshape(n, RW_WIDTH), zg, lw, tm2)
    y = _moe(hn, comb, tile_counts, x1, lw, gf, _tile(n, MOE_BLOCK))
    kv_shape = (b, WINDOW, N_KV_HEADS, HEAD_DIM)
    return (y.reshape(b, t, D_MODEL), k_win.reshape(kv_shape), v_win.reshape(kv_shape),
            s_new, zr3[:, t - 1, :])


def kernel(x_prompt, x_sample, cache_k, cache_v, state_wkv, state_shift, norm_mix_g, w_in, attn_sinks, rw_mu, rw_w0, rw_w_up, rw_a0, rw_a_up, rw_g_up, rw_k_k, rw_k_a, rw_r_k, rw_lnx_g, rw_lnx_b, w_branch_attn, w_branch_rwkv, w_out, norm_ffn_g, w_router, b_router, w_gate_up, b_gate_up, w_down, b_down, norm_final_g):
    assert w_in.shape[0] == 1, "single-layer trunk"
    bp, tp, _ = x_prompt.shape
    bs, ts, _ = x_sample.shape
    lw = _prep_layer(0, norm_mix_g, w_in, attn_sinks, rw_mu, rw_w0, rw_w_up, rw_a0, rw_a_up, rw_g_up,
                     rw_k_k, rw_k_a, rw_r_k, rw_lnx_g, rw_lnx_b, w_branch_attn, w_branch_rwkv, w_out,
                     norm_ffn_g, w_router, b_router, w_gate_up, b_gate_up, w_down, b_down)
    gf = norm_final_g.reshape(1, D_MODEL)

    tab_p = _rope_tables(jnp.arange(tp, dtype=jnp.int32))
    tm_s = _tile(bs * ts, IN_TILE)
    pos_s = PAST_LEN + jnp.arange(ts, dtype=jnp.int32)
    tab_s = tuple(jnp.tile(u, (tm_s // ts, 1)) for u in _rope_tables(pos_s))

    s0p = jnp.zeros((bp, RW_HEADS, RW_HEAD, RW_HEAD), state_wkv.dtype)
    sh0p = jnp.zeros((bp, RW_COLS), state_shift.dtype)
    sets_p = RW_SEQS_PER_STEP if bp % RW_SEQS_PER_STEP == 0 else 1
    segs_s = RW_CHUNK // ts
    sets_s = 2 if bs % (2 * segs_s) == 0 else 1
    yp, kp, vp, sp, shp = _layer(x_prompt, tab_p, None, None, s0p, sh0p, lw, gf, sets_p, 1)
    ys, ks, vs, ss, shs = _layer(x_sample, tab_s, cache_k[0], cache_v[0], state_wkv[0], state_shift[0],
                                 lw, gf, sets_s, segs_s)
    ex = lambda u: u[None]
    return (yp, ys, ex(kp), ex(vp), ex(sp), ex(shp), ex(ks), ex(vs), ex(ss), ex(shs))
```

```python
import functools
import math

import jax
import jax.numpy as jnp
import numpy as np
from jax import lax
from jax.experimental import pallas as pl
from jax.experimental.pallas import tpu as pltpu

F32 = jnp.float32
BF16 = jnp.bfloat16

LANES = 128
SUBLANES = 8
VMEM_LIMIT_BYTES = 56 * 1024 * 1024

D_MODEL = 1024
HEAD_DIM = 64
N_Q_HEADS = 16
N_KV_HEADS = 4
Q_PER_KV = 4
WINDOW = 128
ROPE_THETA = 500000.0
ROPE_DIM = 16
ROPE_HALF = 8
ATTN_SCALE = HEAD_DIM ** -0.5
PAST_LEN = 16384
RW_HEAD = 64
RW_HEADS = 16
RW_PAIRS = RW_HEADS // 2
W_LORA = 64
A_LORA = 64
G_LORA = 128
LNX_EPS = 64e-5
N_EXPERTS = 32
TOP_K = 4
D_EXPERT = 1024
SWIGLU_LIMIT = 7.0
SWIGLU_ALPHA = 1.702
RMS_EPS = 1e-5
ATT_Q = N_Q_HEADS * HEAD_DIM
ATT_KV = N_KV_HEADS * HEAD_DIM
RW_WIDTH = RW_HEADS * RW_HEAD
RW_COLS = 3 * RW_WIDTH + W_LORA + A_LORA + G_LORA
GATE_COLS = 2 * D_MODEL
IN_COLS = ATT_Q + 2 * ATT_KV + RW_COLS + GATE_COLS
DECAY_SCALE = math.exp(-0.5)
RW_CHUNK = 64
IN_TILE = 512
MOE_BLOCK = 2048
RW_SEQS_PER_STEP = 4


def _nn(a, b):
    return jnp.dot(a, b, preferred_element_type=F32)


def _nt(a, b):
    return lax.dot_general(a, b, (((1,), (1,)), ((), ())), preferred_element_type=F32)


def _tn(a, b):
    return lax.dot_general(a, b, (((0,), (0,)), ((), ())), preferred_element_type=F32)


def _bf(x):
    return x.astype(BF16)


def _sigmoid(x):
    return 0.5 * jnp.tanh(0.5 * x) + 0.5


def _split2(x):
    hi = x.astype(BF16)
    lo = (x - hi.astype(F32)).astype(BF16)
    return hi, lo


def _split3(x):
    hi = x.astype(BF16)
    r1 = x - hi.astype(F32)
    mid = r1.astype(BF16)
    lo = (r1 - mid.astype(F32)).astype(BF16)
    return hi, mid, lo


def _params(n_axes):
    return pltpu.CompilerParams(
        dimension_semantics=("arbitrary",) * n_axes, vmem_limit_bytes=VMEM_LIMIT_BYTES)


_IN_CHUNK = 512


def _inproj_kernel(x_ref, g_ref, w_ref, cos_ref, sa_ref, sb_ref,
                   q_ref, k_ref, v_ref, zr_ref, zg_ref):
    x = x_ref[...]
    ms = jnp.mean(x * x, axis=-1, keepdims=True)
    h = _bf(x * lax.rsqrt(ms + RMS_EPS) * g_ref[...])
    cos, sa, sb = cos_ref[...], sa_ref[...], sb_ref[...]

    def rope(y):
        return (y * cos + pltpu.roll(y, LANES - ROPE_HALF, axis=1) * sa
                + pltpu.roll(y, ROPE_HALF, axis=1) * sb)

    def project(out_ref, col0, width, with_rope):
        for c in range(0, width, _IN_CHUNK):
            cw = min(_IN_CHUNK, width - c)
            acc = _nn(h, w_ref[:, col0 + c:col0 + c + cw])
            if with_rope:
                for j in range(0, cw, LANES):
                    out_ref[:, c + j:c + j + LANES] = rope(acc[:, j:j + LANES])
            else:
                out_ref[:, c:c + cw] = acc

    project(q_ref, 0, ATT_Q, True)
    project(k_ref, ATT_Q, ATT_KV, True)
    project(v_ref, ATT_Q + ATT_KV, ATT_KV, False)
    project(zr_ref, ATT_Q + 2 * ATT_KV, RW_COLS, False)
    project(zg_ref, ATT_Q + 2 * ATT_KV + RW_COLS, GATE_COLS, False)


def _rope_tables(positions, reps=1):
    f32 = np.float32
    inv_freq = (f32(ROPE_THETA) ** (-np.arange(ROPE_HALF, dtype=f32) / f32(ROPE_HALF))).astype(f32)
    ang = positions.astype(f32)[:, None] * inv_freq[None, :]
    cos, sin = np.cos(ang).astype(f32), np.sin(ang).astype(f32)
    t = positions.shape[0]
    one = np.ones((t, HEAD_DIM - ROPE_DIM), f32)
    zero = np.zeros((t, HEAD_DIM - ROPE_DIM), f32)
    z8 = np.zeros((t, ROPE_HALF), f32)
    cos_h = np.concatenate([cos, cos, one], axis=1)
    sa_h = np.concatenate([-sin, z8, zero], axis=1)
    sb_h = np.concatenate([z8, sin, zero], axis=1)
    full = lambda u: jnp.asarray(np.tile(np.concatenate([u, u], axis=1), (reps, 1)))
    return full(cos_h), full(sa_h), full(sb_h)


def _inproj(x2d, g, w_bf, tables, tm):
    n = x2d.shape[0]
    cos, sa, sb = tables
    nper = cos.shape[0] // tm
    row = lambda i: (i, 0)
    tab = lambda i: (i % nper, 0)
    const = lambda i: (0, 0)
    out_shapes = [jax.ShapeDtypeStruct((n, w), F32) for w in (ATT_Q, ATT_KV, ATT_KV, RW_COLS, GATE_COLS)]
    return pl.pallas_call(
        _inproj_kernel,
        grid=(n // tm,),
        in_specs=[
            pl.BlockSpec((tm, D_MODEL), row),
            pl.BlockSpec((1, D_MODEL), const),
            pl.BlockSpec((D_MODEL, IN_COLS), const, pipeline_mode=pl.Buffered(1)),
            pl.BlockSpec((tm, LANES), tab),
            pl.BlockSpec((tm, LANES), tab),
            pl.BlockSpec((tm, LANES), tab),
        ],
        out_specs=[pl.BlockSpec((tm, w), row) for w in (ATT_Q, ATT_KV, ATT_KV, RW_COLS, GATE_COLS)],
        out_shape=out_shapes,
        compiler_params=_params(1),
    )(x2d, g, w_bf, cos, sa, sb)


def _attn_prompt_kernel(sink_ref, q_ref, kp_ref, kc_ref, vp_ref, vc_ref, o_ref, kw_ref, vw_ref):
    n = pl.program_id(1)
    w = WINDOW
    lane = lax.broadcasted_iota(jnp.int32, (w, LANES), 1)
    first_half = lane < HEAD_DIM
    qi = lax.broadcasted_iota(jnp.int32, (Q_PER_KV * w, 2 * w), 0) % w
    kj = lax.broadcasted_iota(jnp.int32, (Q_PER_KV * w, 2 * w), 1)
    first_key = jnp.where(n > 0, 0, w)
    band4 = (kj > qi) & (kj <= qi + w) & (kj >= first_key)

    kw_ref[...] = kc_ref[...]
    vw_ref[...] = vc_ref[...]

    lane2 = lax.broadcasted_iota(jnp.int32, (2 * w, LANES), 1)
    kvs = range(N_KV_HEADS)
    k2, v2 = [], []
    for kv in kvs:
        gs = slice((kv // 2) * LANES, (kv // 2 + 1) * LANES)
        mine = (lane2 < HEAD_DIM) if kv % 2 == 0 else (lane2 >= HEAD_DIM)
        k2.append(_bf(jnp.concatenate([kp_ref[:, gs], kc_ref[:, gs]], axis=0)))
        v2.append(_bf(jnp.where(mine, jnp.concatenate([vp_ref[:, gs], vc_ref[:, gs]], axis=0), 1.0)))
    qs, sink = [], []
    for kv in kvs:
        jh = kv % 2
        keep = first_half if jh == 0 else jnp.logical_not(first_half)
        rows, sinks = [], []
        for gq in range(Q_PER_KV):
            hq = kv * Q_PER_KV + gq
            qg = q_ref[:, (hq // 2) * LANES:(hq // 2 + 1) * LANES]
            if hq % 2 != jh:
                qg = pltpu.roll(qg, HEAD_DIM, axis=1)
            rows.append(jnp.where(keep, qg * ATTN_SCALE, 0.0))
            sinks.append(jnp.full((w, 1), sink_ref[hq], F32))
        qs.append(_bf(jnp.concatenate(rows, axis=0)))
        sink.append(jnp.concatenate(sinks, axis=0))
    s = [jnp.where(band4, _nt(q_, k_), -jnp.inf) for q_, k_ in zip(qs, k2)]
    m = [jnp.maximum(jnp.max(s_, axis=-1, keepdims=True), sk) for s_, sk in zip(s, sink)]
    e = [_bf(jnp.exp(s_ - m_)) for s_, m_ in zip(s, m)]
    e_sink = [jnp.exp(sk - m_) for sk, m_ in zip(sink, m)]
    pv = [_nn(e_, v_) for e_, v_ in zip(e, v2)]
    for kv in kvs:
        jh = kv % 2
        for go in range(Q_PER_KV // 2):
            ra, rb = slice(2 * go * w, (2 * go + 1) * w), slice((2 * go + 1) * w, (2 * go + 2) * w)
            pa, pb = pv[kv][ra], pv[kv][rb]
            pa_r, pb_r = pltpu.roll(pa, HEAD_DIM, axis=1), pltpu.roll(pb, HEAD_DIM, axis=1)
            if jh == 0:
                oa, ob = pa / (pa_r + e_sink[kv][ra]), pb_r / (pb + e_sink[kv][rb])
            else:
                oa, ob = pa_r / (pa + e_sink[kv][ra]), pb / (pb_r + e_sink[kv][rb])
            og = kv * (Q_PER_KV // 2) + go
            o_ref[:, og * LANES:(og + 1) * LANES] = jnp.where(first_half, oa, ob).astype(o_ref.dtype)


def _attn_prompt(q, k, v, sinks):
    b, t, _ = q.shape
    nb = t // WINDOW
    cur = lambda bi, n: (bi, n, 0)
    prev = lambda bi, n: (bi, jnp.maximum(n - 1, 0), 0)
    win = lambda bi, n: (bi, 0, 0)
    return pl.pallas_call(
        _attn_prompt_kernel,
        grid=(b, nb),
        in_specs=[
            pl.BlockSpec(memory_space=pltpu.SMEM),
            pl.BlockSpec((None, WINDOW, ATT_Q), cur),
            pl.BlockSpec((None, WINDOW, ATT_KV), prev),
            pl.BlockSpec((None, WINDOW, ATT_KV), cur),
            pl.BlockSpec((None, WINDOW, ATT_KV), prev),
            pl.BlockSpec((None, WINDOW, ATT_KV), cur),
        ],
        out_specs=[
            pl.BlockSpec((None, WINDOW, ATT_Q), cur),
            pl.BlockSpec((None, WINDOW, ATT_KV), win),
            pl.BlockSpec((None, WINDOW, ATT_KV), win),
        ],
        out_shape=[
            jax.ShapeDtypeStruct((b, t, ATT_Q), BF16),
            jax.ShapeDtypeStruct((b, WINDOW, ATT_KV), F32),
            jax.ShapeDtypeStruct((b, WINDOW, ATT_KV), F32),
        ],
        compiler_params=_params(2),
    )(sinks, q, k, k, v, v)


_SAMPLE_BT = 8


def _attn_sample_kernel(sink_ref, q_ref, k_ref, v_ref, ck_ref, cv_ref, o_ref, nk_ref, nv_ref, *, t):
    w = WINDOW
    rows_per_grp = 2 * Q_PER_KV * t
    lane = lax.broadcasted_iota(jnp.int32, (t, LANES), 1)
    first_half = lane < HEAD_DIM
    r_c = lax.broadcasted_iota(jnp.int32, (rows_per_grp, w), 0) % t
    c_c = lax.broadcasted_iota(jnp.int32, (rows_per_grp, w), 1)
    mask_c = c_c > r_c
    r_n = lax.broadcasted_iota(jnp.int32, (rows_per_grp, t), 0) % t
    c_n = lax.broadcasted_iota(jnp.int32, (rows_per_grp, t), 1)
    mask_n = c_n <= r_n

    for bi in range(_SAMPLE_BT):
        nk_ref[bi, 0:w - t, :] = ck_ref[bi, t:w, :]
        nk_ref[bi, w - t:w, :] = k_ref[bi]
        nv_ref[bi, 0:w - t, :] = cv_ref[bi, t:w, :]
        nv_ref[bi, w - t:w, :] = v_ref[bi]

    chains = [(bi, grp) for bi in range(_SAMPLE_BT) for grp in range(ATT_KV // LANES)]
    gsl = lambda grp: slice(grp * LANES, (grp + 1) * LANES)
    kc = [_bf(ck_ref[bi, :, gsl(grp)]) for bi, grp in chains]
    vc = [_bf(cv_ref[bi, :, gsl(grp)]) for bi, grp in chains]
    kn = [_bf(k_ref[bi, :, gsl(grp)]) for bi, grp in chains]
    vn = [_bf(v_ref[bi, :, gsl(grp)]) for bi, grp in chains]
    sink_rows = [[jnp.full((t, 1), sink_ref[(grp * 2 + jh) * Q_PER_KV + gq], F32)
                  for jh in range(2) for gq in range(Q_PER_KV)] for grp in range(ATT_KV // LANES)]
    sink_g = [jnp.concatenate(r_, axis=0) for r_ in sink_rows]
    qs = []
    for bi, grp in chains:
        rows = []
        for jh in range(2):
            keep = first_half if jh == 0 else jnp.logical_not(first_half)
            for gq in range(Q_PER_KV):
                hq = (grp * 2 + jh) * Q_PER_KV + gq
                qg = q_ref[bi, :, (hq // 2) * LANES:(hq // 2 + 1) * LANES]
                if hq % 2 != jh:
                    qg = pltpu.roll(qg, HEAD_DIM, axis=1)
                rows.append(jnp.where(keep, qg * ATTN_SCALE, 0.0))
        qs.append(_bf(jnp.concatenate(rows, axis=0)))
    sink = [sink_g[grp] for _, grp in chains]
    s_c = [jnp.where(mask_c, _nt(q_, k_), -jnp.inf) for q_, k_ in zip(qs, kc)]
    s_n = [jnp.where(mask_n, _nt(q_, k_), -jnp.inf) for q_, k_ in zip(qs, kn)]
    m = [jnp.maximum(jnp.maximum(jnp.max(a, axis=-1, keepdims=True), jnp.max(b_, axis=-1, keepdims=True)), sk)
         for a, b_, sk in zip(s_c, s_n, sink)]
    e_c = [jnp.exp(a - m_) for a, m_ in zip(s_c, m)]
    e_n = [jnp.exp(a - m_) for a, m_ in zip(s_n, m)]
    denom = [jnp.sum(a, axis=-1, keepdims=True) + jnp.sum(b_, axis=-1, keepdims=True) + jnp.exp(sk - m_)
             for a, b_, sk, m_ in zip(e_c, e_n, sink, m)]
    pv = [_nn(_bf(a / d_), va) + _nn(_bf(b_ / d_), vb)
          for a, b_, d_, va, vb in zip(e_c, e_n, denom, vc, vn)]
    for (bi, grp), pv_ in zip(chains, pv):
        for jh in range(2):
            for go in range(Q_PER_KV // 2):
                r0 = (jh * Q_PER_KV + 2 * go) * t
                pa, pb = pv_[r0:r0 + t], pv_[r0 + t:r0 + 2 * t]
                if jh == 0:
                    pb = pltpu.roll(pb, HEAD_DIM, axis=1)
                else:
                    pa = pltpu.roll(pa, HEAD_DIM, axis=1)
                og = (grp * 2 + jh) * (Q_PER_KV // 2) + go
                o_ref[bi, :, og * LANES:(og + 1) * LANES] = jnp.where(first_half, pa, pb)


def _attn_sample(q, k, v, sinks, cache_k, cache_v):
    b, t, _ = q.shape
    bt = _SAMPLE_BT
    blk = lambda i: (i, 0, 0)
    return pl.pallas_call(
        functools.partial(_attn_sample_kernel, t=t),
        grid=(b // bt,),
        in_specs=[
            pl.BlockSpec(memory_space=pltpu.SMEM),
            pl.BlockSpec((bt, t, ATT_Q), blk),
            pl.BlockSpec((bt, t, ATT_KV), blk),
            pl.BlockSpec((bt, t, ATT_KV), blk),
            pl.BlockSpec((bt, WINDOW, ATT_KV), blk),
            pl.BlockSpec((bt, WINDOW, ATT_KV), blk),
        ],
        out_specs=[
            pl.BlockSpec((bt, t, ATT_Q), blk),
            pl.BlockSpec((bt, WINDOW, ATT_KV), blk),
            pl.BlockSpec((bt, WINDOW, ATT_KV), blk),
        ],
        out_shape=[
            jax.ShapeDtypeStruct((b, t, ATT_Q), F32),
            jax.ShapeDtypeStruct((b, WINDOW, ATT_KV), F32),
            jax.ShapeDtypeStruct((b, WINDOW, ATT_KV), F32),
        ],
        compiler_params=_params(1),
    )(sinks, q, k, v, cache_k, cache_v)


def _rwkv_kernel(z_ref, s0_ref, shift_ref, mu_ref, w0_ref, wa_up_ref, a0_ref, g_up_ref,
                 kk_ref, ka_ref, rk_ref, lng_ref, lnb_ref, o_ref, sout_ref, s_ref, prev_ref, *, nb, c, nseg):
    tseg = c // nseg
    rows_all = nb * c
    step = pl.program_id(1)
    pairs = range(RW_PAIRS)
    sls = [slice(p * LANES, (p + 1) * LANES) for p in pairs]
    rsl = [slice(j * c, (j + 1) * c) for j in range(nb)]

    @pl.when(step == 0)
    def _():
        zero = jnp.zeros((RW_HEAD, RW_HEAD), F32)
        for q in range(nb * nseg):
            for p in pairs:
                top = jnp.concatenate([s0_ref[q, 2 * p], zero], axis=1)
                bot = jnp.concatenate([zero, s0_ref[q, 2 * p + 1]], axis=1)
                s_ref[q, p] = jnp.concatenate([top, bot], axis=0)
        if nseg == 1:
            prev_ref[...] = shift_ref[...].reshape(nb, RW_COLS)

    z = z_ref[...].reshape(rows_all, RW_COLS)
    row1 = lax.broadcasted_iota(jnp.int32, (rows_all, 1), 0)
    zprev = pltpu.roll(z, 1, axis=0)
    if nseg == 1:
        for j in range(nb):
            zprev = jnp.where(row1 == j * c, prev_ref[j:j + 1, :], zprev)
            prev_ref[j:j + 1, :] = z[(j + 1) * c - 1:(j + 1) * c, :]
    else:
        zprev = jnp.where(row1 % tseg == 0, shift_ref[...].reshape(rows_all, RW_COLS), zprev)
    zs = z + (zprev - z) * mu_ref[...]

    w3 = 3 * RW_WIDTH
    r, k, v = zs[:, 0:RW_WIDTH], zs[:, RW_WIDTH:2 * RW_WIDTH], zs[:, 2 * RW_WIDTH:w3]
    xwa = zs[:, w3:w3 + LANES]
    xg = zs[:, w3 + LANES:w3 + 2 * LANES]
    lane_all = lax.broadcasted_iota(jnp.int32, (rows_all, LANES), 1)
    lora = _nn(_bf(jnp.where(lane_all < RW_HEAD, jnp.tanh(xwa), xwa)), wa_up_ref[...])
    lane = lax.broadcasted_iota(jnp.int32, (c, LANES), 1)
    head0 = lane < RW_HEAD
    lw = -DECAY_SCALE * _sigmoid(w0_ref[...] + lora[:, 0:RW_WIDTH])
    a_sig = _sigmoid(a0_ref[...] + lora[:, RW_WIDTH:2 * RW_WIDTH])
    g = _nn(_bf(_sigmoid(xg)), g_up_ref[...])
    kk = k * kk_ref[...]
    k = k * (1.0 + (a_sig - 1.0) * ka_ref[...])
    rkr = r * k * rk_ref[...]

    ti = lax.broadcasted_iota(jnp.int32, (rows_all, rows_all), 0)
    tj = lax.broadcasted_iota(jnp.int32, (rows_all, rows_all), 1)
    same_seq = (ti // tseg) == (tj // tseg)
    tri = jnp.where((tj <= ti) & same_seq, 1.0, 0.0).astype(BF16)
    lw3 = _split3(lw)
    cum = _nn(tri, lw3[0]) + _nn(tri, lw3[1]) + _nn(tri, lw3[2])
    if nseg == 1:
        ends = [jnp.broadcast_to(cum[(j + 1) * c - 1:(j + 1) * c, :], (c, RW_WIDTH)) for j in range(nb)]
        c_end = ends[0] if nb == 1 else jnp.concatenate(ends, axis=0)
    else:
        seq1 = jnp.where(same_seq, 1.0, 0.0).astype(BF16)
        c_end = _nn(seq1, lw3[0]) + _nn(seq1, lw3[1]) + _nn(seq1, lw3[2])
    e_c, e_ci, e_cm = jnp.exp(cum), jnp.exp(-cum), jnp.exp(cum - lw)
    e_ce, w_end = jnp.exp(c_end - cum), jnp.exp(c_end)

    gi = lax.broadcasted_iota(jnp.int32, (2 * LANES, LANES), 0) % LANES
    gj = lax.broadcasted_iota(jnp.int32, (2 * LANES, LANES), 1)
    ones2 = jnp.where((gi // RW_HEAD) == (gj // RW_HEAD), 1.0, 0.0).astype(BF16)
    bi_ = lax.broadcasted_iota(jnp.int32, (LANES, LANES), 0)
    bj_ = lax.broadcasted_iota(jnp.int32, (LANES, LANES), 1)
    same_head = (bi_ // RW_HEAD) == (bj_ // RW_HEAD)
    ci = lax.broadcasted_iota(jnp.int32, (c, 2 * c), 0)
    cj = lax.broadcasted_iota(jnp.int32, (c, 2 * c), 1)
    cjm = cj % c
    seq_ok = (ci // tseg) == (cjm // tseg)
    strict = (cjm < ci) & seq_ok
    incl = (cjm <= ci) & seq_ok
    eye_cat = jnp.where(cjm == ci, 1.0, 0.0)
    left = cj < c

    def seg_sum(x):
        hi, lo = _split2(x)
        return _nn(jnp.concatenate([hi, lo], axis=1), ones2)

    def rows2(x):
        return jnp.concatenate([jnp.where(head0, x, 0.0), jnp.where(head0, 0.0, x)], axis=0)

    def bd(cat):
        return _bf(jnp.concatenate([jnp.where(left, cat, 0.0), jnp.where(left, 0.0, cat)], axis=0))

    def pair_mm(cat, x):
        return _nn(_bf(cat), _bf(rows2(x)))

    def seg_sums(xs):
        tot = seg_sum(jnp.concatenate(xs, axis=0))
        return [tot[i * c:(i + 1) * c] for i in range(len(xs))]

    segs = range(nseg)
    rs = [slice(q * tseg, (q + 1) * tseg) for q in segs]

    def run(chains):
        gs = [(rsl[j], sls[p]) for j, p in chains]
        idx = range(len(chains))
        ss = seg_sums([kk[s] * kk[s] for s in gs])
        kkn = [kk[s] * lax.rsqrt(jnp.maximum(q, 1e-24)) for s, q in zip(gs, ss)]
        bv = [n_ * a_sig[s] for s, n_ in zip(gs, kkn)]
        rt = [r[s] * e_c[s] for s in gs]
        kt = [k[s] * e_ci[s] for s in gs]
        at = [-n_ * e_cm[s] for s, n_ in zip(gs, kkn)]
        bt = [b_ * e_ci[s] for s, b_ in zip(gs, bv)]
        bh = [b_ * e_ce[s] for s, b_ in zip(gs, bv)]
        kh = [k[s] * e_ce[s] for s in gs]
        vv = [v[s] for s in gs]

        ar = [_bf(jnp.concatenate([a_, r_], axis=0)) for a_, r_ in zip(at, rt)]
        xbk = [_nt(x, _bf(jnp.concatenate([rows2(b_), rows2(k_)], axis=0)))
               for x, b_, k_ in zip(ar, bt, kt)]
        l_ab = [jnp.where(strict, x[0:c, 0:2 * c], 0.0) for x in xbk]
        l_ak = [jnp.where(strict, x[0:c, 2 * c:4 * c], 0.0) for x in xbk]
        m_rb = [jnp.where(incl, x[c:2 * c, 0:2 * c], 0.0) for x in xbk]
        m_rk = [jnp.where(incl, x[c:2 * c, 2 * c:4 * c], 0.0) for x in xbk]

        t_inv = [l + eye_cat for l in l_ab]
        pw = l_ab
        pw_bd = [bd(x) for x in pw]
        for _ in range(int(math.log2(tseg)) - 1):
            pw = [_nn(_bf(x), xb_) for x, xb_ in zip(pw, pw_bd)]
            pw_bd = [bd(x) for x in pw]
            t_inv = [t + _nn(_bf(t), xb_) for t, xb_ in zip(t_inv, pw_bd)]

        lvy = [pair_mm(jnp.concatenate([l, m], axis=0), v_) for l, m, v_ in zip(l_ak, m_rk, vv)]
        y1 = [x[c:2 * c] for x in lvy]
        au = [_nn(_bf(t), _bf(jnp.concatenate([rows2(a_), rows2(x[0:c])], axis=1)))
              for t, a_, x in zip(t_inv, at, lvy)]
        a_hat = [x[:, 0:LANES] for x in au]
        u0 = [x[:, LANES:2 * LANES] for x in au]

        s_old = [[s_ref[j * nseg + q, p] for q in segs] for j, p in chains]
        pp = [[_nt(_bf(jnp.concatenate([a_hat[i][rs[q]], rt[i][rs[q]]], axis=0)), _bf(s_old[i][q]))
               for q in segs] for i in idx]
        u = [jnp.concatenate([pp[i][q][0:tseg] for q in segs], axis=0) + u0[i] for i in idx]
        y0 = [jnp.concatenate([pp[i][q][tseg:2 * tseg] for q in segs], axis=0) for i in idx]
        y = [y0[i] + pair_mm(m_rb[i], u[i]) + y1[i] for i in idx]
        for i in idx:
            for q in segs:
                upd = _tn(_bf(jnp.concatenate([u[i][rs[q]], vv[i][rs[q]]], axis=0)),
                          _bf(jnp.concatenate([bh[i][rs[q]], kh[i][rs[q]]], axis=0)))
                j, p = chains[i]
                row0 = (j * c + q * tseg) % w_end.shape[0]
                w_q = w_end[row0:row0 + 1, sls[p]]
                s_ref[j * nseg + q, p] = s_old[i][q] * w_q + jnp.where(same_head, upd, 0.0)

        mean = [x * (1.0 / RW_HEAD) for x in seg_sums(y)]
        d = [x - m for x, m in zip(y, mean)]
        var = [x * (1.0 / RW_HEAD) for x in seg_sums([x * x for x in d])]
        bonus = [x * v_ for x, v_ in zip(seg_sums([rkr[s] for s in gs]), vv)]
        for i in idx:
            j, p = chains[i]
            yn = d[i] * lax.rsqrt(var[i] + LNX_EPS) * lng_ref[:, sls[p]] + lnb_ref[:, sls[p]]
            o_ref[j, :, sls[p]] = ((yn + bonus[i]) * g[gs[i]]).astype(o_ref.dtype)

    run([(j, p) for j in range(nb) for p in pairs])

    @pl.when(step == pl.num_programs(1) - 1)
    def _():
        for q in range(nb * nseg):
            for p in pairs:
                tile = s_ref[q, p]
                sout_ref[q, 2 * p] = tile[0:RW_HEAD, 0:RW_HEAD]
                sout_ref[q, 2 * p + 1] = tile[RW_HEAD:2 * RW_HEAD, RW_HEAD:2 * RW_HEAD]


def _rwkv(zr, s0, shift0, lw, nb, c, nseg):
    b, t, _ = zr.shape
    if nseg == 1:
        ngrp, nchunk = b // nb, t // c
        z3 = zr
        shift = shift0.reshape(b, 1, RW_COLS)
        shift_spec = pl.BlockSpec((nb, 1, RW_COLS), lambda bi, i: (bi, 0, 0))
    else:
        assert c == nseg * t
        nsets = b // nseg
        ngrp, nchunk = nsets // nb, 1
        z3 = zr.reshape(nsets, c, RW_COLS)
        shift = jnp.pad(shift0[:, None, :], ((0, 0), (0, t - 1), (0, 0))).reshape(nsets, c, RW_COLS)
        shift_spec = pl.BlockSpec((nb, c, RW_COLS), lambda bi, i: (bi, 0, 0))
    vec = lambda name: lw[name].reshape(1, -1).astype(F32)
    cst = lambda bi, i: (0, 0)
    vspec = lambda wd: pl.BlockSpec((1, wd), cst)
    state_spec = pl.BlockSpec((nb * nseg, RW_HEADS, RW_HEAD, RW_HEAD), lambda bi, i: (bi, 0, 0, 0))
    o, s_new = pl.pallas_call(
        functools.partial(_rwkv_kernel, nb=nb, c=c, nseg=nseg),
        grid=(ngrp, nchunk),
        in_specs=[
            pl.BlockSpec((nb, c, RW_COLS), lambda bi, i: (bi, i, 0)),
            state_spec,
            shift_spec,
            vspec(RW_COLS), vspec(RW_WIDTH),
            pl.BlockSpec((LANES, 2 * RW_WIDTH), cst),
            vspec(RW_WIDTH),
            pl.BlockSpec((G_LORA, RW_WIDTH), cst),
            vspec(RW_WIDTH), vspec(RW_WIDTH), vspec(RW_WIDTH), vspec(RW_WIDTH), vspec(RW_WIDTH),
        ],
        out_specs=[
            pl.BlockSpec((nb, c, RW_WIDTH), lambda bi, i: (bi, i, 0)),
            state_spec,
        ],
        out_shape=[
            jax.ShapeDtypeStruct((ngrp * nb, nchunk * c, RW_WIDTH), BF16),
            jax.ShapeDtypeStruct((b, RW_HEADS, RW_HEAD, RW_HEAD), F32),
        ],
        scratch_shapes=[pltpu.VMEM((nb * nseg, RW_PAIRS, LANES, LANES), F32), pltpu.VMEM((nb, RW_COLS), F32)],
        compiler_params=_params(2),
    )(z3, s0, shift, vec("rw_mu"), vec("rw_w0"), lw["wa_up"], vec("rw_a0"),
      lw["g_up"], vec("rw_k_k"), vec("rw_k_a"), vec("rw_r_k"), vec("rw_lnx_g"), vec("rw_lnx_b"))
    return o.reshape(b, t, RW_WIDTH), s_new


def _mix_kernel(x_ref, oa_ref, or_ref, zg_ref, wba_ref, wbr_ref, wo_ref, g_ref, wr_ref, br_ref,
                x1_ref, hn_ref, comb_ref, cnt_ref):
    ya = _nn(_bf(oa_ref[...]), wba_ref[...])
    yr = _nn(_bf(or_ref[...]), wbr_ref[...])
    merged = _sigmoid(zg_ref[:, 0:D_MODEL]) * ya + _sigmoid(zg_ref[:, D_MODEL:2 * D_MODEL]) * yr
    x1 = x_ref[...] + _nn(_bf(merged), wo_ref[...])
    x1_ref[...] = x1
    ms = jnp.mean(x1 * x1, axis=-1, keepdims=True)
    hn = x1 * lax.rsqrt(ms + RMS_EPS) * g_ref[...]
    hn_ref[...] = _bf(hn)
    logits = _nn(_bf(hn), wr_ref[...]) + br_ref[...]
    lane = lax.broadcasted_iota(jnp.int32, logits.shape, 1).astype(F32)
    work = logits
    top = None
    for _ in range(TOP_K):
        m = jnp.max(work, axis=-1, keepdims=True)
        if top is None:
            top = m
        idx = jnp.min(jnp.where(work == m, lane, float(LANES)), axis=-1, keepdims=True)
        work = jnp.where(lane == idx, -jnp.inf, work)
    e = jnp.where(work != logits, jnp.exp(logits - top), 0.0)
    comb = e / jnp.sum(e, axis=-1, keepdims=True)
    comb_ref[...] = comb
    cnt_ref[...] = jnp.sum(jnp.where(comb > 0.0, 1.0, 0.0), axis=0, keepdims=True)


def _mix(x2d, oa, orw, zg, lw, tm):
    n = x2d.shape[0]
    row = lambda i: (i, 0)
    cst = lambda i: (0, 0)
    wspec = pl.BlockSpec((D_MODEL, D_MODEL), cst)
    return pl.pallas_call(
        _mix_kernel,
        grid=(n // tm,),
        in_specs=[
            pl.BlockSpec((tm, D_MODEL), row), pl.BlockSpec((tm, D_MODEL), row), pl.BlockSpec((tm, D_MODEL), row),
            pl.BlockSpec((tm, GATE_COLS), row),
            wspec, wspec, wspec,
            pl.BlockSpec((1, D_MODEL), cst),
            pl.BlockSpec((D_MODEL, LANES), cst),
            pl.BlockSpec((1, LANES), cst),
        ],
        out_specs=[pl.BlockSpec((tm, D_MODEL), row), pl.BlockSpec((tm, D_MODEL), row),
                   pl.BlockSpec((tm, LANES), row), pl.BlockSpec((None, 1, LANES), lambda i: (i, 0, 0))],
        out_shape=[jax.ShapeDtypeStruct((n, D_MODEL), F32), jax.ShapeDtypeStruct((n, D_MODEL), BF16),
                   jax.ShapeDtypeStruct((n, LANES), F32), jax.ShapeDtypeStruct((n // tm, 1, LANES), F32)],
        compiler_params=_params(1),
    )(x2d, oa, orw, zg, lw["w_ba"], lw["w_br"], lw["w_o"], lw["norm_ffn_g"], lw["w_r"], lw["b_r"])


_GU_GROUP = 2 * LANES


def _gu_regroup_kernel(w_ref, o_ref):
    src = lax.broadcasted_iota(jnp.int32, (_GU_GROUP, _GU_GROUP), 0)
    dst = lax.broadcasted_iota(jnp.int32, (_GU_GROUP, _GU_GROUP), 1)
    want = jnp.where(dst < LANES, 2 * dst, 2 * (dst - LANES) + 1)
    perm = jnp.where(src == want, 1.0, 0.0).astype(BF16)
    for j in range(0, 2 * D_EXPERT, _GU_GROUP):
        o_ref[:, j:j + _GU_GROUP] = _nn(_bf(w_ref[:, j:j + _GU_GROUP]), perm).astype(o_ref.dtype)


def _gu_regroup(w_gate_up):
    e, d, n = w_gate_up.shape
    tk = 512
    spec = pl.BlockSpec((None, tk, n), lambda i, j: (i, j, 0))
    return pl.pallas_call(
        _gu_regroup_kernel,
        grid=(e, d // tk),
        in_specs=[spec],
        out_specs=spec,
        out_shape=jax.ShapeDtypeStruct((e, d, n), BF16),
        compiler_params=_params(2),
    )(w_gate_up)


_MOE_DOMAIN = 1024
_MOE_ROWS = 160
_RANK_CHUNK = 256


def _moe_kernel(nsub_ref, hn_ref, comb_ref, x1_ref, wgu_ref, bgu_ref, wd_ref, bd_ref, gf_ref, out_ref,
                rank_ref, rank_t_ref, comb_t_ref, act_ref):
    blk, e = pl.program_id(0), pl.program_id(1)
    tb = hn_ref.shape[0]
    dom, rows = min(_MOE_DOMAIN, tb), _MOE_ROWS
    ndom = tb // dom

    @pl.when(e == 0)
    def _():
        out_ref[...] = x1_ref[...]
        ci = lax.broadcasted_iota(jnp.int32, (_RANK_CHUNK, _RANK_CHUNK), 0)
        cj = lax.broadcasted_iota(jnp.int32, (_RANK_CHUNK, _RANK_CHUNK), 1)
        before = jnp.where(cj < ci, 1.0, 0.0).astype(BF16)
        for d0 in range(0, tb, dom):
            seen = jnp.zeros((1, LANES), F32)
            for c0 in range(d0, d0 + dom, _RANK_CHUNK):
                routed = comb_ref[c0:c0 + _RANK_CHUNK, :] > 0.0
                hot = jnp.where(routed, 1.0, 0.0)
                rank_ref[c0:c0 + _RANK_CHUNK, :] = jnp.where(routed, _nn(before, _bf(hot)) + seen, -1.0)
                seen = seen + jnp.sum(hot, axis=0, keepdims=True)
        rank_t_ref[...] = rank_ref[...].T
        comb_t_ref[...] = comb_ref[...].T

    slot_r = lax.broadcasted_iota(jnp.int32, (rows, dom), 0).astype(F32)

    doms = [slice(d * dom, (d + 1) * dom) for d in range(ndom)]
    rank_rows = [rank_t_ref[pl.ds(e, 1), ds_] for ds_ in doms]
    w_rows_all = [comb_t_ref[pl.ds(e, 1), ds_] for ds_ in doms]
    trips = [nsub_ref[(blk * ndom + d) * N_EXPERTS + e] for d in range(ndom)]
    trip = trips[0]
    for t_ in trips[1:]:
        trip = jnp.maximum(trip, t_)

    def sub_tiles(s, carry):
        base = (s * rows).astype(F32)
        for d, ds_ in enumerate(doms):
            hit = rank_rows[d] - base == slot_r
            gather = jnp.where(hit, 1.0, 0.0).astype(BF16)
            x = _bf(_nn(gather, hn_ref[ds_, :]))
            w_rows = jnp.sum(jnp.where(hit, w_rows_all[d], 0.0), axis=1, keepdims=True)
            for j in range(D_EXPERT // LANES):
                gs = slice(j * _GU_GROUP, (j + 1) * _GU_GROUP)
                gu = _nn(x, wgu_ref[:, gs]) + bgu_ref[:, gs]
                glu = jnp.minimum(gu[:, 0:LANES], SWIGLU_LIMIT)
                lin = jnp.clip(gu[:, LANES:_GU_GROUP], -SWIGLU_LIMIT, SWIGLU_LIMIT)
                act_ref[d, :, j * LANES:(j + 1) * LANES] = _bf(glu * _sigmoid(SWIGLU_ALPHA * glu) * (lin + 1.0))
            y = (_nn(act_ref[d], _bf(wd_ref[...])) + bd_ref[...]) * w_rows
            out_ref[ds_, :] += _tn(gather, _bf(y))
        return carry

    lax.fori_loop(0, trip, sub_tiles, 0)

    @pl.when(e == N_EXPERTS - 1)
    def _():
        for r0 in range(0, tb, _RANK_CHUNK):
            xo = out_ref[r0:r0 + _RANK_CHUNK, :]
            ms = jnp.mean(xo * xo, axis=-1, keepdims=True)
            out_ref[r0:r0 + _RANK_CHUNK, :] = xo * lax.rsqrt(ms + RMS_EPS) * gf_ref[...]


def _moe(hn, comb, tile_counts, x1, lw, gf, tb):
    n = hn.shape[0]
    nblk = n // tb
    ndomains = n // min(_MOE_DOMAIN, tb)
    counts = tile_counts.reshape(ndomains, -1, LANES).sum(axis=1)[:, :N_EXPERTS].astype(jnp.int32)
    nsub = ((counts + _MOE_ROWS - 1) // _MOE_ROWS).reshape(-1)
    row = lambda i, e, ns: (i, 0)
    ex = lambda i, e, ns: (e, 0, 0)
    return pl.pallas_call(
        _moe_kernel,
        grid_spec=pltpu.PrefetchScalarGridSpec(
            num_scalar_prefetch=1,
            grid=(nblk, N_EXPERTS),
            in_specs=[
                pl.BlockSpec((tb, D_MODEL), row), pl.BlockSpec((tb, LANES), row),
                pl.BlockSpec((tb, D_MODEL), row, pipeline_mode=pl.Buffered(1)),
                pl.BlockSpec((None, D_MODEL, 2 * D_EXPERT), ex), pl.BlockSpec((None, 1, 2 * D_EXPERT), ex),
                pl.BlockSpec((None, D_EXPERT, D_MODEL), ex), pl.BlockSpec((None, 1, D_MODEL), ex),
                pl.BlockSpec((1, D_MODEL), lambda i, e, ns: (0, 0)),
            ],
            out_specs=pl.BlockSpec((tb, D_MODEL), row),
            scratch_shapes=[pltpu.VMEM((tb, LANES), F32), pltpu.VMEM((LANES, tb), F32),
                            pltpu.VMEM((LANES, tb), F32),
                            pltpu.VMEM((tb // min(_MOE_DOMAIN, tb), _MOE_ROWS, D_EXPERT), BF16)],
        ),
        out_shape=jax.ShapeDtypeStruct((n, D_MODEL), F32),
        compiler_params=_params(2),
    )(nsub, hn, comb, x1, lw["w_gu"], lw["b_gu"], lw["w_down"], lw["b_down"], gf)


def _prep_layer(l, norm_mix_g, w_in, attn_sinks, rw_mu, rw_w0, rw_w_up, rw_a0, rw_a_up, rw_g_up,
                rw_k_k, rw_k_a, rw_r_k, rw_lnx_g, rw_lnx_b, w_branch_attn, w_branch_rwkv, w_out,
                norm_ffn_g, w_router, b_router, w_gate_up, b_gate_up, w_down, b_down):
    zeros = jnp.zeros((W_LORA, RW_WIDTH), F32)
    wa_up = jnp.concatenate([jnp.concatenate([rw_w_up[l], zeros], axis=1),
                             jnp.concatenate([zeros, rw_a_up[l]], axis=1)], axis=0)
    pad = LANES - N_EXPERTS
    return {
        "norm_mix_g": norm_mix_g[l].reshape(1, D_MODEL),
        "w_in": _bf(w_in[l]),
        "attn_sinks": attn_sinks[l].astype(F32),
        "rw_mu": rw_mu[l], "rw_w0": rw_w0[l], "rw_a0": rw_a0[l], "rw_k_k": rw_k_k[l], "rw_k_a": rw_k_a[l],
        "rw_r_k": rw_r_k[l], "rw_lnx_g": rw_lnx_g[l], "rw_lnx_b": rw_lnx_b[l],
        "wa_up": _bf(wa_up), "g_up": _bf(rw_g_up[l]),
        "w_ba": _bf(w_branch_attn[l]), "w_br": _bf(w_branch_rwkv[l]), "w_o": _bf(w_out[l]),
        "norm_ffn_g": norm_ffn_g[l].reshape(1, D_MODEL),
        "w_r": _bf(jnp.pad(w_router[l], ((0, 0), (0, pad)))),
        "b_r": jnp.pad(b_router[l], (0, pad), constant_values=-jnp.inf).reshape(1, LANES),
        "w_gu": _gu_regroup(w_gate_up[l]),
        "b_gu": b_gate_up[l].reshape(N_EXPERTS, D_EXPERT // LANES, LANES, 2).transpose(0, 1, 3, 2)
                .reshape(N_EXPERTS, 1, 2 * D_EXPERT),
        "w_down": w_down[l], "b_down": b_down[l][:, None, :],
    }


def _tile(n, pref):
    tm = pref
    while n % tm:
        tm //= 2
    return tm


def _layer(x, tables, cache_k, cache_v, s0, shift0, lw, gf, rw_sets, rw_segs):
    b, t, _ = x.shape
    n = b * t
    x2d = x.reshape(n, D_MODEL)
    tm = _tile(n, IN_TILE)
    q, k, v, zr, zg = _inproj(x2d, lw["norm_mix_g"], lw["w_in"], tables, tm)
    q3, k3, v3 = q.reshape(b, t, ATT_Q), k.reshape(b, t, ATT_KV), v.reshape(b, t, ATT_KV)
    if cache_k is None:
        o_att, k_win, v_win = _attn_prompt(q3, k3, v3, lw["attn_sinks"])
    else:
        o_att, k_win, v_win = _attn_sample(q3, k3, v3, lw["attn_sinks"],
                                           cache_k.reshape(b, WINDOW, ATT_KV), cache_v.reshape(b, WINDOW, ATT_KV))
    zr3 = zr.reshape(b, t, RW_COLS)
    o_rw, s_new = _rwkv(zr3, s0, shift0, lw, rw_sets, RW_CHUNK, rw_segs)
    tm2 = _tile(n, 512)
    x1, hn, comb, tile_counts = _mix(x2d, o_att.reshape(n, ATT_Q), o_rw.reshape(n, RW_WIDTH), zg, lw, tm2)
    y = _moe(hn, comb, tile_counts, x1, lw, gf, _tile(n, MOE_BLOCK))
    kv_shape = (b, WINDOW, N_KV_HEADS, HEAD_DIM)
    return (y.reshape(b, t, D_MODEL), k_win.reshape(kv_shape), v_win.reshape(kv_shape),
            s_new, zr3[:, t - 1, :])


def kernel(x_prompt, x_sample, cache_k, cache_v, state_wkv, state_shift, norm_mix_g, w_in, attn_sinks, rw_mu, rw_w0, rw_w_up, rw_a0, rw_a_up, rw_g_up, rw_k_k, rw_k_a, rw_r_k, rw_lnx_g, rw_lnx_b, w_branch_attn, w_branch_rwkv, w_out, norm_ffn_g, w_router, b_router, w_gate_up, b_gate_up, w_down, b_down, norm_final_g):
    assert w_in.shape[0] == 1, "single-layer trunk"
    bp, tp, _ = x_prompt.shape
    bs, ts, _ = x_sample.shape
    lw = _prep_layer(0, norm_mix_g, w_in, attn_sinks, rw_mu, rw_w0, rw_w_up, rw_a0, rw_a_up, rw_g_up,
                     rw_k_k, rw_k_a, rw_r_k, rw_lnx_g, rw_lnx_b, w_branch_attn, w_branch_rwkv, w_out,
                     norm_ffn_g, w_router, b_router, w_gate_up, b_gate_up, w_down, b_down)
    gf = norm_final_g.reshape(1, D_MODEL)

    tab_p = _rope_tables(np.arange(tp))
    tab_s = _rope_tables(PAST_LEN + np.arange(ts), reps=_tile(bs * ts, IN_TILE) // ts)

    s0p = jnp.zeros((bp, RW_HEADS, RW_HEAD, RW_HEAD), state_wkv.dtype)
    sh0p = jnp.zeros((bp, RW_COLS), state_shift.dtype)
    sets_p = RW_SEQS_PER_STEP if bp % RW_SEQS_PER_STEP == 0 else 1
    segs_s = RW_CHUNK // ts
    sets_s = 2 if bs % (2 * segs_s) == 0 else 1
    yp, kp, vp, sp, shp = _layer(x_prompt, tab_p, None, None, s0p, sh0p, lw, gf, sets_p, 1)
    ys, ks, vs, ss, shs = _layer(x_sample, tab_s, cache_k[0], cache_v[0], state_wkv[0], state_shift[0],
                                 lw, gf, sets_s, segs_s)
    ex = lambda u: u[None]
    return (yp, ys, ex(kp), ex(vp), ex(sp), ex(shp), ex(ks), ex(vs), ex(ss), ex(shs))
```

```python
import functools
import math

import jax
import jax.numpy as jnp
import numpy as np
from jax import lax
from jax.experimental import pallas as pl
from jax.experimental.pallas import tpu as pltpu

F32 = jnp.float32
BF16 = jnp.bfloat16

LANES = 128
SUBLANES = 8
VMEM_LIMIT_BYTES = 56 * 1024 * 1024

D_MODEL = 1024
HEAD_DIM = 64
N_Q_HEADS = 16
N_KV_HEADS = 4
Q_PER_KV = 4
WINDOW = 128
ROPE_THETA = 500000.0
ROPE_DIM = 16
ROPE_HALF = 8
ATTN_SCALE = HEAD_DIM ** -0.5
PAST_LEN = 16384
RW_HEAD = 64
RW_HEADS = 16
RW_PAIRS = RW_HEADS // 2
W_LORA = 64
A_LORA = 64
G_LORA = 128
LNX_EPS = 64e-5
N_EXPERTS = 32
TOP_K = 4
D_EXPERT = 1024
SWIGLU_LIMIT = 7.0
SWIGLU_ALPHA = 1.702
RMS_EPS = 1e-5
ATT_Q = N_Q_HEADS * HEAD_DIM
ATT_KV = N_KV_HEADS * HEAD_DIM
RW_WIDTH = RW_HEADS * RW_HEAD
RW_COLS = 3 * RW_WIDTH + W_LORA + A_LORA + G_LORA
GATE_COLS = 2 * D_MODEL
IN_COLS = ATT_Q + 2 * ATT_KV + RW_COLS + GATE_COLS
DECAY_SCALE = math.exp(-0.5)
RW_CHUNK = 64
IN_TILE = 512
MOE_BLOCK = 2048
RW_SEQS_PER_STEP = 4


def _nn(a, b):
    return jnp.dot(a, b, preferred_element_type=F32)


def _nt(a, b):
    return lax.dot_general(a, b, (((1,), (1,)), ((), ())), preferred_element_type=F32)


def _tn(a, b):
    return lax.dot_general(a, b, (((0,), (0,)), ((), ())), preferred_element_type=F32)


def _bf(x):
    return x.astype(BF16)


def _sigmoid(x):
    return 0.5 * jnp.tanh(0.5 * x) + 0.5


def _split2(x):
    hi = x.astype(BF16)
    lo = (x - hi.astype(F32)).astype(BF16)
    return hi, lo


def _split3(x):
    hi = x.astype(BF16)
    r1 = x - hi.astype(F32)
    mid = r1.astype(BF16)
    lo = (r1 - mid.astype(F32)).astype(BF16)
    return hi, mid, lo


def _params(n_axes):
    return pltpu.CompilerParams(
        dimension_semantics=("arbitrary",) * n_axes, vmem_limit_bytes=VMEM_LIMIT_BYTES)


_IN_CHUNK = 512


def _inproj_kernel(x_ref, g_ref, w_ref, cos_ref, sa_ref, sb_ref,
                   q_ref, k_ref, v_ref, zr_ref, zg_ref):
    x = x_ref[...]
    ms = jnp.mean(x * x, axis=-1, keepdims=True)
    h = _bf(x * lax.rsqrt(ms + RMS_EPS) * g_ref[...])
    cos, sa, sb = cos_ref[...], sa_ref[...], sb_ref[...]

    def rope(y):
        return (y * cos + pltpu.roll(y, LANES - ROPE_HALF, axis=1) * sa
                + pltpu.roll(y, ROPE_HALF, axis=1) * sb)

    def project(out_ref, col0, width, with_rope):
        for c in range(0, width, _IN_CHUNK):
            cw = min(_IN_CHUNK, width - c)
            acc = _nn(h, w_ref[:, col0 + c:col0 + c + cw])
            if with_rope:
                for j in range(0, cw, LANES):
                    out_ref[:, c + j:c + j + LANES] = rope(acc[:, j:j + LANES])
            else:
                out_ref[:, c:c + cw] = acc

    project(q_ref, 0, ATT_Q, True)
    project(k_ref, ATT_Q, ATT_KV, True)
    project(v_ref, ATT_Q + ATT_KV, ATT_KV, False)
    project(zr_ref, ATT_Q + 2 * ATT_KV, RW_COLS, False)
    project(zg_ref, ATT_Q + 2 * ATT_KV + RW_COLS, GATE_COLS, False)


def _rope_tables(positions, reps=1):
    f32 = np.float32
    inv_freq = (f32(ROPE_THETA) ** (-np.arange(ROPE_HALF, dtype=f32) / f32(ROPE_HALF))).astype(f32)
    ang = positions.astype(f32)[:, None] * inv_freq[None, :]
    cos, sin = np.cos(ang).astype(f32), np.sin(ang).astype(f32)
    t = positions.shape[0]
    one = np.ones((t, HEAD_DIM - ROPE_DIM), f32)
    zero = np.zeros((t, HEAD_DIM - ROPE_DIM), f32)
    z8 = np.zeros((t, ROPE_HALF), f32)
    cos_h = np.concatenate([cos, cos, one], axis=1)
    sa_h = np.concatenate([-sin, z8, zero], axis=1)
    sb_h = np.concatenate([z8, sin, zero], axis=1)
    full = lambda u: jnp.asarray(np.tile(np.concatenate([u, u], axis=1), (reps, 1)))
    return full(cos_h), full(sa_h), full(sb_h)


def _inproj(x2d, g, w_bf, tables, tm):
    n = x2d.shape[0]
    cos, sa, sb = tables
    nper = cos.shape[0] // tm
    row = lambda i: (i, 0)
    tab = lambda i: (i % nper, 0)
    const = lambda i: (0, 0)
    out_shapes = [jax.ShapeDtypeStruct((n, w), F32) for w in (ATT_Q, ATT_KV, ATT_KV, RW_COLS, GATE_COLS)]
    return pl.pallas_call(
        _inproj_kernel,
        grid=(n // tm,),
        in_specs=[
            pl.BlockSpec((tm, D_MODEL), row),
            pl.BlockSpec((1, D_MODEL), const),
            pl.BlockSpec((D_MODEL, IN_COLS), const, pipeline_mode=pl.Buffered(1)),
            pl.BlockSpec((tm, LANES), tab),
            pl.BlockSpec((tm, LANES), tab),
            pl.BlockSpec((tm, LANES), tab),
        ],
        out_specs=[pl.BlockSpec((tm, w), row) for w in (ATT_Q, ATT_KV, ATT_KV, RW_COLS, GATE_COLS)],
        out_shape=out_shapes,
        compiler_params=_params(1),
    )(x2d, g, w_bf, cos, sa, sb)


def _attn_prompt_kernel(sink_ref, q_ref, kp_ref, kc_ref, vp_ref, vc_ref, o_ref, kw_ref, vw_ref):
    n = pl.program_id(1)
    w = WINDOW
    lane = lax.broadcasted_iota(jnp.int32, (w, LANES), 1)
    first_half = lane < HEAD_DIM
    qi = lax.broadcasted_iota(jnp.int32, (Q_PER_KV * w, 2 * w), 0) % w
    kj = lax.broadcasted_iota(jnp.int32, (Q_PER_KV * w, 2 * w), 1)
    first_key = jnp.where(n > 0, 0, w)
    band4 = (kj > qi) & (kj <= qi + w) & (kj >= first_key)

    kw_ref[...] = kc_ref[...]
    vw_ref[...] = vc_ref[...]

    lane2 = lax.broadcasted_iota(jnp.int32, (2 * w, LANES), 1)
    kvs = range(N_KV_HEADS)
    k2, v2 = [], []
    for kv in kvs:
        gs = slice((kv // 2) * LANES, (kv // 2 + 1) * LANES)
        mine = (lane2 < HEAD_DIM) if kv % 2 == 0 else (lane2 >= HEAD_DIM)
        k2.append(_bf(jnp.concatenate([kp_ref[:, gs], kc_ref[:, gs]], axis=0)))
        v2.append(_bf(jnp.where(mine, jnp.concatenate([vp_ref[:, gs], vc_ref[:, gs]], axis=0), 1.0)))
    qs, sink = [], []
    for kv in kvs:
        jh = kv % 2
        keep = first_half if jh == 0 else jnp.logical_not(first_half)
        rows, sinks = [], []
        for gq in range(Q_PER_KV):
            hq = kv * Q_PER_KV + gq
            qg = q_ref[:, (hq // 2) * LANES:(hq // 2 + 1) * LANES]
            if hq % 2 != jh:
                qg = pltpu.roll(qg, HEAD_DIM, axis=1)
            rows.append(jnp.where(keep, qg * ATTN_SCALE, 0.0))
            sinks.append(jnp.full((w, 1), sink_ref[hq], F32))
        qs.append(_bf(jnp.concatenate(rows, axis=0)))
        sink.append(jnp.concatenate(sinks, axis=0))
    s = [jnp.where(band4, _nt(q_, k_), -jnp.inf) for q_, k_ in zip(qs, k2)]
    m = [jnp.maximum(jnp.max(s_, axis=-1, keepdims=True), sk) for s_, sk in zip(s, sink)]
    e = [_bf(jnp.exp(s_ - m_)) for s_, m_ in zip(s, m)]
    e_sink = [jnp.exp(sk - m_) for sk, m_ in zip(sink, m)]
    pv = [_nn(e_, v_) for e_, v_ in zip(e, v2)]
    for kv in kvs:
        jh = kv % 2
        for go in range(Q_PER_KV // 2):
            ra, rb = slice(2 * go * w, (2 * go + 1) * w), slice((2 * go + 1) * w, (2 * go + 2) * w)
            pa, pb = pv[kv][ra], pv[kv][rb]
            pa_r, pb_r = pltpu.roll(pa, HEAD_DIM, axis=1), pltpu.roll(pb, HEAD_DIM, axis=1)
            if jh == 0:
                oa, ob = pa / (pa_r + e_sink[kv][ra]), pb_r / (pb + e_sink[kv][rb])
            else:
                oa, ob = pa_r / (pa + e_sink[kv][ra]), pb / (pb_r + e_sink[kv][rb])
            og = kv * (Q_PER_KV // 2) + go
            o_ref[:, og * LANES:(og + 1) * LANES] = jnp.where(first_half, oa, ob).astype(o_ref.dtype)


def _attn_prompt(q, k, v, sinks):
    b, t, _ = q.shape
    nb = t // WINDOW
    cur = lambda bi, n: (bi, n, 0)
    prev = lambda bi, n: (bi, jnp.maximum(n - 1, 0), 0)
    win = lambda bi, n: (bi, 0, 0)
    return pl.pallas_call(
        _attn_prompt_kernel,
        grid=(b, nb),
        in_specs=[
            pl.BlockSpec(memory_space=pltpu.SMEM),
            pl.BlockSpec((None, WINDOW, ATT_Q), cur),
            pl.BlockSpec((None, WINDOW, ATT_KV), prev),
            pl.BlockSpec((None, WINDOW, ATT_KV), cur),
            pl.BlockSpec((None, WINDOW, ATT_KV), prev),
            pl.BlockSpec((None, WINDOW, ATT_KV), cur),
        ],
        out_specs=[
            pl.BlockSpec((None, WINDOW, ATT_Q), cur),
            pl.BlockSpec((None, WINDOW, ATT_KV), win),
            pl.BlockSpec((None, WINDOW, ATT_KV), win),
        ],
        out_shape=[
            jax.ShapeDtypeStruct((b, t, ATT_Q), BF16),
            jax.ShapeDtypeStruct((b, WINDOW, ATT_KV), F32),
            jax.ShapeDtypeStruct((b, WINDOW, ATT_KV), F32),
        ],
        compiler_params=_params(2),
    )(sinks, q, k, k, v, v)


_SAMPLE_BT = 8


def _attn_sample_kernel(sink_ref, q_ref, k_ref, v_ref, ck_ref, cv_ref, o_ref, nk_ref, nv_ref, *, t):
    w = WINDOW
    rows_per_grp = 2 * Q_PER_KV * t
    lane = lax.broadcasted_iota(jnp.int32, (t, LANES), 1)
    first_half = lane < HEAD_DIM
    r_c = lax.broadcasted_iota(jnp.int32, (rows_per_grp, w), 0) % t
    c_c = lax.broadcasted_iota(jnp.int32, (rows_per_grp, w), 1)
    mask_c = c_c > r_c
    r_n = lax.broadcasted_iota(jnp.int32, (rows_per_grp, t), 0) % t
    c_n = lax.broadcasted_iota(jnp.int32, (rows_per_grp, t), 1)
    mask_n = c_n <= r_n

    for bi in range(_SAMPLE_BT):
        nk_ref[bi, 0:w - t, :] = ck_ref[bi, t:w, :]
        nk_ref[bi, w - t:w, :] = k_ref[bi]
        nv_ref[bi, 0:w - t, :] = cv_ref[bi, t:w, :]
        nv_ref[bi, w - t:w, :] = v_ref[bi]

    chains = [(bi, grp) for bi in range(_SAMPLE_BT) for grp in range(ATT_KV // LANES)]
    gsl = lambda grp: slice(grp * LANES, (grp + 1) * LANES)
    kc = [_bf(ck_ref[bi, :, gsl(grp)]) for bi, grp in chains]
    vc = [_bf(cv_ref[bi, :, gsl(grp)]) for bi, grp in chains]
    kn = [_bf(k_ref[bi, :, gsl(grp)]) for bi, grp in chains]
    vn = [_bf(v_ref[bi, :, gsl(grp)]) for bi, grp in chains]
    sink_rows = [[jnp.full((t, 1), sink_ref[(grp * 2 + jh) * Q_PER_KV + gq], F32)
                  for jh in range(2) for gq in range(Q_PER_KV)] for grp in range(ATT_KV // LANES)]
    sink_g = [jnp.concatenate(r_, axis=0) for r_ in sink_rows]
    qs = []
    for bi, grp in chains:
        rows = []
        for jh in range(2):
            keep = first_half if jh == 0 else jnp.logical_not(first_half)
            for gq in range(Q_PER_KV):
                hq = (grp * 2 + jh) * Q_PER_KV + gq
                qg = q_ref[bi, :, (hq // 2) * LANES:(hq // 2 + 1) * LANES]
                if hq % 2 != jh:
                    qg = pltpu.roll(qg, HEAD_DIM, axis=1)
                rows.append(jnp.where(keep, qg * ATTN_SCALE, 0.0))
        qs.append(_bf(jnp.concatenate(rows, axis=0)))
    sink = [sink_g[grp] for _, grp in chains]
    s_c = [jnp.where(mask_c, _nt(q_, k_), -jnp.inf) for q_, k_ in zip(qs, kc)]
    s_n = [jnp.where(mask_n, _nt(q_, k_), -jnp.inf) for q_, k_ in zip(qs, kn)]
    m = [jnp.maximum(jnp.maximum(jnp.max(a, axis=-1, keepdims=True), jnp.max(b_, axis=-1, keepdims=True)), sk)
         for a, b_, sk in zip(s_c, s_n, sink)]
    e_c = [jnp.exp(a - m_) for a, m_ in zip(s_c, m)]
    e_n = [jnp.exp(a - m_) for a, m_ in zip(s_n, m)]
    denom = [jnp.sum(a, axis=-1, keepdims=True) + jnp.sum(b_, axis=-1, keepdims=True) + jnp.exp(sk - m_)
             for a, b_, sk, m_ in zip(e_c, e_n, sink, m)]
    pv = [_nn(_bf(a / d_), va) + _nn(_bf(b_ / d_), vb)
          for a, b_, d_, va, vb in zip(e_c, e_n, denom, vc, vn)]
    for (bi, grp), pv_ in zip(chains, pv):
        for jh in range(2):
            for go in range(Q_PER_KV // 2):
                r0 = (jh * Q_PER_KV + 2 * go) * t
                pa, pb = pv_[r0:r0 + t], pv_[r0 + t:r0 + 2 * t]
                if jh == 0:
                    pb = pltpu.roll(pb, HEAD_DIM, axis=1)
                else:
                    pa = pltpu.roll(pa, HEAD_DIM, axis=1)
                og = (grp * 2 + jh) * (Q_PER_KV // 2) + go
                o_ref[bi, :, og * LANES:(og + 1) * LANES] = jnp.where(first_half, pa, pb)


def _attn_sample(q, k, v, sinks, cache_k, cache_v):
    b, t, _ = q.shape
    bt = _SAMPLE_BT
    blk = lambda i: (i, 0, 0)
    return pl.pallas_call(
        functools.partial(_attn_sample_kernel, t=t),
        grid=(b // bt,),
        in_specs=[
            pl.BlockSpec(memory_space=pltpu.SMEM),
            pl.BlockSpec((bt, t, ATT_Q), blk),
            pl.BlockSpec((bt, t, ATT_KV), blk),
            pl.BlockSpec((bt, t, ATT_KV), blk),
            pl.BlockSpec((bt, WINDOW, ATT_KV), blk),
            pl.BlockSpec((bt, WINDOW, ATT_KV), blk),
        ],
        out_specs=[
            pl.BlockSpec((bt, t, ATT_Q), blk),
            pl.BlockSpec((bt, WINDOW, ATT_KV), blk),
            pl.BlockSpec((bt, WINDOW, ATT_KV), blk),
        ],
        out_shape=[
            jax.ShapeDtypeStruct((b, t, ATT_Q), F32),
            jax.ShapeDtypeStruct((b, WINDOW, ATT_KV), F32),
            jax.ShapeDtypeStruct((b, WINDOW, ATT_KV), F32),
        ],
        compiler_params=_params(1),
    )(sinks, q, k, v, cache_k, cache_v)


def _rwkv_kernel(z_ref, s0_ref, shift_ref, mu_ref, w0_ref, wa_up_ref, a0_ref, g_up_ref,
                 kk_ref, ka_ref, rk_ref, lng_ref, lnb_ref, o_ref, sout_ref, s_ref, prev_ref, *, nb, c, nseg):
    tseg = c // nseg
    rows_all = nb * c
    step = pl.program_id(1)
    pairs = range(RW_PAIRS)
    sls = [slice(p * LANES, (p + 1) * LANES) for p in pairs]
    rsl = [slice(j * c, (j + 1) * c) for j in range(nb)]

    @pl.when(step == 0)
    def _():
        zero = jnp.zeros((RW_HEAD, RW_HEAD), F32)
        for q in range(nb * nseg):
            for p in pairs:
                top = jnp.concatenate([s0_ref[q, 2 * p], zero], axis=1)
                bot = jnp.concatenate([zero, s0_ref[q, 2 * p + 1]], axis=1)
                s_ref[q, p] = jnp.concatenate([top, bot], axis=0)
        if nseg == 1:
            prev_ref[...] = shift_ref[...].reshape(nb, RW_COLS)

    z = z_ref[...].reshape(rows_all, RW_COLS)
    row1 = lax.broadcasted_iota(jnp.int32, (rows_all, 1), 0)
    zprev = pltpu.roll(z, 1, axis=0)
    if nseg == 1:
        for j in range(nb):
            zprev = jnp.where(row1 == j * c, prev_ref[j:j + 1, :], zprev)
            prev_ref[j:j + 1, :] = z[(j + 1) * c - 1:(j + 1) * c, :]
    else:
        zprev = jnp.where(row1 % tseg == 0, shift_ref[...].reshape(rows_all, RW_COLS), zprev)
    zs = z + (zprev - z) * mu_ref[...]

    w3 = 3 * RW_WIDTH
    r, k, v = zs[:, 0:RW_WIDTH], zs[:, RW_WIDTH:2 * RW_WIDTH], zs[:, 2 * RW_WIDTH:w3]
    xwa = zs[:, w3:w3 + LANES]
    xg = zs[:, w3 + LANES:w3 + 2 * LANES]
    lane_all = lax.broadcasted_iota(jnp.int32, (rows_all, LANES), 1)
    lora = _nn(_bf(jnp.where(lane_all < RW_HEAD, jnp.tanh(xwa), xwa)), wa_up_ref[...])
    lane = lax.broadcasted_iota(jnp.int32, (c, LANES), 1)
    head0 = lane < RW_HEAD
    lw = -DECAY_SCALE * _sigmoid(w0_ref[...] + lora[:, 0:RW_WIDTH])
    a_sig = _sigmoid(a0_ref[...] + lora[:, RW_WIDTH:2 * RW_WIDTH])
    g = _nn(_bf(_sigmoid(xg)), g_up_ref[...])
    kk = k * kk_ref[...]
    k = k * (1.0 + (a_sig - 1.0) * ka_ref[...])
    rkr = r * k * rk_ref[...]

    ti = lax.broadcasted_iota(jnp.int32, (rows_all, rows_all), 0)
    tj = lax.broadcasted_iota(jnp.int32, (rows_all, rows_all), 1)
    same_seq = (ti // tseg) == (tj // tseg)
    tri = jnp.where((tj <= ti) & same_seq, 1.0, 0.0).astype(BF16)
    lw3 = _split3(lw)
    cum = _nn(tri, lw3[0]) + _nn(tri, lw3[1]) + _nn(tri, lw3[2])
    if nseg == 1:
        ends = [jnp.broadcast_to(cum[(j + 1) * c - 1:(j + 1) * c, :], (c, RW_WIDTH)) for j in range(nb)]
        c_end = ends[0] if nb == 1 else jnp.concatenate(ends, axis=0)
    else:
        seq1 = jnp.where(same_seq, 1.0, 0.0).astype(BF16)
        c_end = _nn(seq1, lw3[0]) + _nn(seq1, lw3[1]) + _nn(seq1, lw3[2])
    e_c, e_ci, e_cm = jnp.exp(cum), jnp.exp(-cum), jnp.exp(cum - lw)
    e_ce, w_end = jnp.exp(c_end - cum), jnp.exp(c_end)

    gi = lax.broadcasted_iota(jnp.int32, (2 * LANES, LANES), 0) % LANES
    gj = lax.broadcasted_iota(jnp.int32, (2 * LANES, LANES), 1)
    ones2 = jnp.where((gi // RW_HEAD) == (gj // RW_HEAD), 1.0, 0.0).astype(BF16)
    bi_ = lax.broadcasted_iota(jnp.int32, (LANES, LANES), 0)
    bj_ = lax.broadcasted_iota(jnp.int32, (LANES, LANES), 1)
    same_head = (bi_ // RW_HEAD) == (bj_ // RW_HEAD)
    ci = lax.broadcasted_iota(jnp.int32, (c, 2 * c), 0)
    cj = lax.broadcasted_iota(jnp.int32, (c, 2 * c), 1)
    cjm = cj % c
    seq_ok = (ci // tseg) == (cjm // tseg)
    strict = (cjm < ci) & seq_ok
    incl = (cjm <= ci) & seq_ok
    eye_cat = jnp.where(cjm == ci, 1.0, 0.0)
    left = cj < c

    def seg_sum(x):
        hi, lo = _split2(x)
        return _nn(jnp.concatenate([hi, lo], axis=1), ones2)

    def rows2(x):
        return jnp.concatenate([jnp.where(head0, x, 0.0), jnp.where(head0, 0.0, x)], axis=0)

    def bd(cat):
        return _bf(jnp.concatenate([jnp.where(left, cat, 0.0), jnp.where(left, 0.0, cat)], axis=0))

    def pair_mm(cat, x):
        return _nn(_bf(cat), _bf(rows2(x)))

    def seg_sums(xs):
        tot = seg_sum(jnp.concatenate(xs, axis=0))
        return [tot[i * c:(i + 1) * c] for i in range(len(xs))]

    segs = range(nseg)
    rs = [slice(q * tseg, (q + 1) * tseg) for q in segs]

    def run(chains):
        gs = [(rsl[j], sls[p]) for j, p in chains]
        idx = range(len(chains))
        ss = seg_sums([kk[s] * kk[s] for s in gs])
        kkn = [kk[s] * lax.rsqrt(jnp.maximum(q, 1e-24)) for s, q in zip(gs, ss)]
        bv = [n_ * a_sig[s] for s, n_ in zip(gs, kkn)]
        rt = [r[s] * e_c[s] for s in gs]
        kt = [k[s] * e_ci[s] for s in gs]
        at = [-n_ * e_cm[s] for s, n_ in zip(gs, kkn)]
        bt = [b_ * e_ci[s] for s, b_ in zip(gs, bv)]
        bh = [b_ * e_ce[s] for s, b_ in zip(gs, bv)]
        kh = [k[s] * e_ce[s] for s in gs]
        vv = [v[s] for s in gs]

        ar = [_bf(jnp.concatenate([a_, r_], axis=0)) for a_, r_ in zip(at, rt)]
        xbk = [_nt(x, _bf(jnp.concatenate([rows2(b_), rows2(k_)], axis=0)))
               for x, b_, k_ in zip(ar, bt, kt)]
        l_ab = [jnp.where(strict, x[0:c, 0:2 * c], 0.0) for x in xbk]
        l_ak = [jnp.where(strict, x[0:c, 2 * c:4 * c], 0.0) for x in xbk]
        m_rb = [jnp.where(incl, x[c:2 * c, 0:2 * c], 0.0) for x in xbk]
        m_rk = [jnp.where(incl, x[c:2 * c, 2 * c:4 * c], 0.0) for x in xbk]

        t_inv = [l + eye_cat for l in l_ab]
        pw = l_ab
        pw_bd = [bd(x) for x in pw]
        for _ in range(int(math.log2(tseg)) - 1):
            pw = [_nn(_bf(x), xb_) for x, xb_ in zip(pw, pw_bd)]
            pw_bd = [bd(x) for x in pw]
            t_inv = [t + _nn(_bf(t), xb_) for t, xb_ in zip(t_inv, pw_bd)]

        lvy = [pair_mm(jnp.concatenate([l, m], axis=0), v_) for l, m, v_ in zip(l_ak, m_rk, vv)]
        y1 = [x[c:2 * c] for x in lvy]
        au = [_nn(_bf(t), _bf(jnp.concatenate([rows2(a_), rows2(x[0:c])], axis=1)))
              for t, a_, x in zip(t_inv, at, lvy)]
        a_hat = [x[:, 0:LANES] for x in au]
        u0 = [x[:, LANES:2 * LANES] for x in au]

        s_old = [[s_ref[j * nseg + q, p] for q in segs] for j, p in chains]
        pp = [[_nt(_bf(jnp.concatenate([a_hat[i][rs[q]], rt[i][rs[q]]], axis=0)), _bf(s_old[i][q]))
               for q in segs] for i in idx]
        u = [jnp.concatenate([pp[i][q][0:tseg] for q in segs], axis=0) + u0[i] for i in idx]
        y0 = [jnp.concatenate([pp[i][q][tseg:2 * tseg] for q in segs], axis=0) for i in idx]
        y = [y0[i] + pair_mm(m_rb[i], u[i]) + y1[i] for i in idx]
        for i in idx:
            for q in segs:
                upd = _tn(_bf(jnp.concatenate([u[i][rs[q]], vv[i][rs[q]]], axis=0)),
                          _bf(jnp.concatenate([bh[i][rs[q]], kh[i][rs[q]]], axis=0)))
                j, p = chains[i]
                row0 = (j * c + q * tseg) % w_end.shape[0]
                w_q = w_end[row0:row0 + 1, sls[p]]
                s_ref[j * nseg + q, p] = s_old[i][q] * w_q + jnp.where(same_head, upd, 0.0)

        mean = [x * (1.0 / RW_HEAD) for x in seg_sums(y)]
        d = [x - m for x, m in zip(y, mean)]
        var = [x * (1.0 / RW_HEAD) for x in seg_sums([x * x for x in d])]
        bonus = [x * v_ for x, v_ in zip(seg_sums([rkr[s] for s in gs]), vv)]
        for i in idx:
            j, p = chains[i]
            yn = d[i] * lax.rsqrt(var[i] + LNX_EPS) * lng_ref[:, sls[p]] + lnb_ref[:, sls[p]]
            o_ref[j, :, sls[p]] = ((yn + bonus[i]) * g[gs[i]]).astype(o_ref.dtype)

    run([(j, p) for j in range(nb) for p in pairs])

    @pl.when(step == pl.num_programs(1) - 1)
    def _():
        for q in range(nb * nseg):
            for p in pairs:
                tile = s_ref[q, p]
                sout_ref[q, 2 * p] = tile[0:RW_HEAD, 0:RW_HEAD]
                sout_ref[q, 2 * p + 1] = tile[RW_HEAD:2 * RW_HEAD, RW_HEAD:2 * RW_HEAD]


def _rwkv(zr, s0, shift0, lw, nb, c, nseg):
    b, t, _ = zr.shape
    if nseg == 1:
        ngrp, nchunk = b // nb, t // c
        z3 = zr
        shift = shift0.reshape(b, 1, RW_COLS)
        shift_spec = pl.BlockSpec((nb, 1, RW_COLS), lambda bi, i: (bi, 0, 0))
    else:
        assert c == nseg * t
        nsets = b // nseg
        ngrp, nchunk = nsets // nb, 1
        z3 = zr.reshape(nsets, c, RW_COLS)
        shift = jnp.pad(shift0[:, None, :], ((0, 0), (0, t - 1), (0, 0))).reshape(nsets, c, RW_COLS)
        shift_spec = pl.BlockSpec((nb, c, RW_COLS), lambda bi, i: (bi, 0, 0))
    vec = lambda name: lw[name].reshape(1, -1).astype(F32)
    cst = lambda bi, i: (0, 0)
    vspec = lambda wd: pl.BlockSpec((1, wd), cst)
    state_spec = pl.BlockSpec((nb * nseg, RW_HEADS, RW_HEAD, RW_HEAD), lambda bi, i: (bi, 0, 0, 0))
    o, s_new = pl.pallas_call(
        functools.partial(_rwkv_kernel, nb=nb, c=c, nseg=nseg),
        grid=(ngrp, nchunk),
        in_specs=[
            pl.BlockSpec((nb, c, RW_COLS), lambda bi, i: (bi, i, 0)),
            state_spec,
            shift_spec,
            vspec(RW_COLS), vspec(RW_WIDTH),
            pl.BlockSpec((LANES, 2 * RW_WIDTH), cst),
            vspec(RW_WIDTH),
            pl.BlockSpec((G_LORA, RW_WIDTH), cst),
            vspec(RW_WIDTH), vspec(RW_WIDTH), vspec(RW_WIDTH), vspec(RW_WIDTH), vspec(RW_WIDTH),
        ],
        out_specs=[
            pl.BlockSpec((nb, c, RW_WIDTH), lambda bi, i: (bi, i, 0)),
            state_spec,
        ],
        out_shape=[
            jax.ShapeDtypeStruct((ngrp * nb, nchunk * c, RW_WIDTH), BF16),
            jax.ShapeDtypeStruct((b, RW_HEADS, RW_HEAD, RW_HEAD), F32),
        ],
        scratch_shapes=[pltpu.VMEM((nb * nseg, RW_PAIRS, LANES, LANES), F32), pltpu.VMEM((nb, RW_COLS), F32)],
        compiler_params=_params(2),
    )(z3, s0, shift, vec("rw_mu"), vec("rw_w0"), lw["wa_up"], vec("rw_a0"),
      lw["g_up"], vec("rw_k_k"), vec("rw_k_a"), vec("rw_r_k"), vec("rw_lnx_g"), vec("rw_lnx_b"))
    return o.reshape(b, t, RW_WIDTH), s_new


def _mix_kernel(x_ref, oa_ref, or_ref, zg_ref, wba_ref, wbr_ref, wo_ref, g_ref, wr_ref, br_ref,
                x1_ref, hn_ref, comb_ref, cnt_ref):
    ya = _nn(_bf(oa_ref[...]), wba_ref[...])
    yr = _nn(_bf(or_ref[...]), wbr_ref[...])
    merged = _sigmoid(zg_ref[:, 0:D_MODEL]) * ya + _sigmoid(zg_ref[:, D_MODEL:2 * D_MODEL]) * yr
    x1 = x_ref[...] + _nn(_bf(merged), wo_ref[...])
    x1_ref[...] = x1
    ms = jnp.mean(x1 * x1, axis=-1, keepdims=True)
    hn = x1 * lax.rsqrt(ms + RMS_EPS) * g_ref[...]
    hn_ref[...] = _bf(hn)
    logits = _nn(_bf(hn), wr_ref[...]) + br_ref[...]
    lane = lax.broadcasted_iota(jnp.int32, logits.shape, 1).astype(F32)
    work = logits
    top = None
    for _ in range(TOP_K):
        m = jnp.max(work, axis=-1, keepdims=True)
        if top is None:
            top = m
        idx = jnp.min(jnp.where(work == m, lane, float(LANES)), axis=-1, keepdims=True)
        work = jnp.where(lane == idx, -jnp.inf, work)
    e = jnp.where(work != logits, jnp.exp(logits - top), 0.0)
    comb = e / jnp.sum(e, axis=-1, keepdims=True)
    comb_ref[...] = comb
    cnt_ref[...] = jnp.sum(jnp.where(comb > 0.0, 1.0, 0.0), axis=0, keepdims=True)


def _mix(x2d, oa, orw, zg, lw, tm):
    n = x2d.shape[0]
    row = lambda i: (i, 0)
    cst = lambda i: (0, 0)
    wspec = pl.BlockSpec((D_MODEL, D_MODEL), cst)
    return pl.pallas_call(
        _mix_kernel,
        grid=(n // tm,),
        in_specs=[
            pl.BlockSpec((tm, D_MODEL), row), pl.BlockSpec((tm, D_MODEL), row), pl.BlockSpec((tm, D_MODEL), row),
            pl.BlockSpec((tm, GATE_COLS), row),
            wspec, wspec, wspec,
            pl.BlockSpec((1, D_MODEL), cst),
            pl.BlockSpec((D_MODEL, LANES), cst),
            pl.BlockSpec((1, LANES), cst),
        ],
        out_specs=[pl.BlockSpec((tm, D_MODEL), row), pl.BlockSpec((tm, D_MODEL), row),
                   pl.BlockSpec((tm, LANES), row), pl.BlockSpec((None, 1, LANES), lambda i: (i, 0, 0))],
        out_shape=[jax.ShapeDtypeStruct((n, D_MODEL), F32), jax.ShapeDtypeStruct((n, D_MODEL), BF16),
                   jax.ShapeDtypeStruct((n, LANES), F32), jax.ShapeDtypeStruct((n // tm, 1, LANES), F32)],
        compiler_params=_params(1),
    )(x2d, oa, orw, zg, lw["w_ba"], lw["w_br"], lw["w_o"], lw["norm_ffn_g"], lw["w_r"], lw["b_r"])


_GU_GROUP = 2 * LANES


def _gu_regroup_kernel(w_ref, o_ref):
    src = lax.broadcasted_iota(jnp.int32, (_GU_GROUP, _GU_GROUP), 0)
    dst = lax.broadcasted_iota(jnp.int32, (_GU_GROUP, _GU_GROUP), 1)
    want = jnp.where(dst < LANES, 2 * dst, 2 * (dst - LANES) + 1)
    perm = jnp.where(src == want, 1.0, 0.0).astype(BF16)
    for j in range(0, 2 * D_EXPERT, _GU_GROUP):
        o_ref[:, j:j + _GU_GROUP] = _nn(_bf(w_ref[:, j:j + _GU_GROUP]), perm).astype(o_ref.dtype)


def _gu_regroup(w_gate_up):
    e, d, n = w_gate_up.shape
    tk = 512
    spec = pl.BlockSpec((None, tk, n), lambda i, j: (i, j, 0))
    return pl.pallas_call(
        _gu_regroup_kernel,
        grid=(e, d // tk),
        in_specs=[spec],
        out_specs=spec,
        out_shape=jax.ShapeDtypeStruct((e, d, n), BF16),
        compiler_params=_params(2),
    )(w_gate_up)


_MOE_DOMAIN = 1024
_MOE_ROWS = 160
_RANK_CHUNK = 256


def _moe_kernel(nsub_ref, hn_ref, comb_ref, x1_ref, wgu_ref, bgu_ref, wd_ref, bd_ref, gf_ref, out_ref,
                rank_ref, rank_t_ref, comb_t_ref, act_ref):
    blk, e = pl.program_id(0), pl.program_id(1)
    tb = hn_ref.shape[0]
    dom, rows = min(_MOE_DOMAIN, tb), _MOE_ROWS
    ndom = tb // dom

    @pl.when(e == 0)
    def _():
        out_ref[...] = jnp.zeros_like(out_ref)
        ci = lax.broadcasted_iota(jnp.int32, (_RANK_CHUNK, _RANK_CHUNK), 0)
        cj = lax.broadcasted_iota(jnp.int32, (_RANK_CHUNK, _RANK_CHUNK), 1)
        before = jnp.where(cj < ci, 1.0, 0.0).astype(BF16)
        for d0 in range(0, tb, dom):
            seen = jnp.zeros((1, LANES), F32)
            for c0 in range(d0, d0 + dom, _RANK_CHUNK):
                routed = comb_ref[c0:c0 + _RANK_CHUNK, :] > 0.0
                hot = jnp.where(routed, 1.0, 0.0)
                rank_ref[c0:c0 + _RANK_CHUNK, :] = jnp.where(routed, _nn(before, _bf(hot)) + seen, -1.0)
                seen = seen + jnp.sum(hot, axis=0, keepdims=True)
        rank_t_ref[...] = rank_ref[...].T
        comb_t_ref[...] = comb_ref[...].T

    slot_r = lax.broadcasted_iota(jnp.int32, (rows, dom), 0).astype(F32)

    doms = [slice(d * dom, (d + 1) * dom) for d in range(ndom)]
    rank_rows = [rank_t_ref[pl.ds(e, 1), ds_] for ds_ in doms]
    w_rows_all = [comb_t_ref[pl.ds(e, 1), ds_] for ds_ in doms]
    trips = [nsub_ref[(blk * ndom + d) * N_EXPERTS + e] for d in range(ndom)]
    trip = trips[0]
    for t_ in trips[1:]:
        trip = jnp.maximum(trip, t_)

    def sub_tiles(s, carry):
        base = (s * rows).astype(F32)
        for d, ds_ in enumerate(doms):
            hit = rank_rows[d] - base == slot_r
            gather = jnp.where(hit, 1.0, 0.0).astype(BF16)
            x = _bf(_nn(gather, hn_ref[ds_, :]))
            w_rows = jnp.sum(jnp.where(hit, w_rows_all[d], 0.0), axis=1, keepdims=True)
            for j in range(D_EXPERT // LANES):
                gs = slice(j * _GU_GROUP, (j + 1) * _GU_GROUP)
                gu = _nn(x, wgu_ref[:, gs]) + bgu_ref[:, gs]
                glu = jnp.minimum(gu[:, 0:LANES], SWIGLU_LIMIT)
                lin = jnp.clip(gu[:, LANES:_GU_GROUP], -SWIGLU_LIMIT, SWIGLU_LIMIT)
                act_ref[d, :, j * LANES:(j + 1) * LANES] = _bf(glu * _sigmoid(SWIGLU_ALPHA * glu) * (lin + 1.0))
            y = (_nn(act_ref[d], _bf(wd_ref[...])) + bd_ref[...]) * w_rows
            out_ref[ds_, :] += _tn(gather, _bf(y))
        return carry

    lax.fori_loop(0, trip, sub_tiles, 0)

    part = x1_ref.shape[0]
    res_rows = pl.ds(pl.multiple_of(e * part, part), part)
    out_ref[res_rows, :] += x1_ref[...]

    @pl.when(e == N_EXPERTS - 1)
    def _():
        for r0 in range(0, tb, _RANK_CHUNK):
            xo = out_ref[r0:r0 + _RANK_CHUNK, :]
            ms = jnp.mean(xo * xo, axis=-1, keepdims=True)
            out_ref[r0:r0 + _RANK_CHUNK, :] = xo * lax.rsqrt(ms + RMS_EPS) * gf_ref[...]


def _moe(hn, comb, tile_counts, x1, lw, gf, tb):
    n = hn.shape[0]
    nblk = n // tb
    ndomains = n // min(_MOE_DOMAIN, tb)
    counts = tile_counts.reshape(ndomains, -1, LANES).sum(axis=1)[:, :N_EXPERTS].astype(jnp.int32)
    nsub = ((counts + _MOE_ROWS - 1) // _MOE_ROWS).reshape(-1)
    row = lambda i, e, ns: (i, 0)
    ex = lambda i, e, ns: (e, 0, 0)
    return pl.pallas_call(
        _moe_kernel,
        grid_spec=pltpu.PrefetchScalarGridSpec(
            num_scalar_prefetch=1,
            grid=(nblk, N_EXPERTS),
            in_specs=[
                pl.BlockSpec((tb, D_MODEL), row), pl.BlockSpec((tb, LANES), row),
                pl.BlockSpec((tb // N_EXPERTS, D_MODEL), lambda i, e, ns: (i * N_EXPERTS + e, 0)),
                pl.BlockSpec((None, D_MODEL, 2 * D_EXPERT), ex), pl.BlockSpec((None, 1, 2 * D_EXPERT), ex),
                pl.BlockSpec((None, D_EXPERT, D_MODEL), ex), pl.BlockSpec((None, 1, D_MODEL), ex),
                pl.BlockSpec((1, D_MODEL), lambda i, e, ns: (0, 0)),
            ],
            out_specs=pl.BlockSpec((tb, D_MODEL), row),
            scratch_shapes=[pltpu.VMEM((tb, LANES), F32), pltpu.VMEM((LANES, tb), F32),
                            pltpu.VMEM((LANES, tb), F32),
                            pltpu.VMEM((tb // min(_MOE_DOMAIN, tb), _MOE_ROWS, D_EXPERT), BF16)],
        ),
        out_shape=jax.ShapeDtypeStruct((n, D_MODEL), F32),
        compiler_params=_params(2),
    )(nsub, hn, comb, x1, lw["w_gu"], lw["b_gu"], lw["w_down"], lw["b_down"], gf)


def _prep_layer(l, norm_mix_g, w_in, attn_sinks, rw_mu, rw_w0, rw_w_up, rw_a0, rw_a_up, rw_g_up,
                rw_k_k, rw_k_a, rw_r_k, rw_lnx_g, rw_lnx_b, w_branch_attn, w_branch_rwkv, w_out,
                norm_ffn_g, w_router, b_router, w_gate_up, b_gate_up, w_down, b_down):
    zeros = jnp.zeros((W_LORA, RW_WIDTH), F32)
    wa_up = jnp.concatenate([jnp.concatenate([rw_w_up[l], zeros], axis=1),
                             jnp.concatenate([zeros, rw_a_up[l]], axis=1)], axis=0)
    pad = LANES - N_EXPERTS
    return {
        "norm_mix_g": norm_mix_g[l].reshape(1, D_MODEL),
        "w_in": _bf(w_in[l]),
        "attn_sinks": attn_sinks[l].astype(F32),
        "rw_mu": rw_mu[l], "rw_w0": rw_w0[l], "rw_a0": rw_a0[l], "rw_k_k": rw_k_k[l], "rw_k_a": rw_k_a[l],
        "rw_r_k": rw_r_k[l], "rw_lnx_g": rw_lnx_g[l], "rw_lnx_b": rw_lnx_b[l],
        "wa_up": _bf(wa_up), "g_up": _bf(rw_g_up[l]),
        "w_ba": _bf(w_branch_attn[l]), "w_br": _bf(w_branch_rwkv[l]), "w_o": _bf(w_out[l]),
        "norm_ffn_g": norm_ffn_g[l].reshape(1, D_MODEL),
        "w_r": _bf(jnp.pad(w_router[l], ((0, 0), (0, pad)))),
        "b_r": jnp.pad(b_router[l], (0, pad), constant_values=-jnp.inf).reshape(1, LANES),
        "w_gu": _gu_regroup(w_gate_up[l]),
        "b_gu": b_gate_up[l].reshape(N_EXPERTS, D_EXPERT // LANES, LANES, 2).transpose(0, 1, 3, 2)
                .reshape(N_EXPERTS, 1, 2 * D_EXPERT),
        "w_down": w_down[l], "b_down": b_down[l][:, None, :],
    }


def _tile(n, pref):
    tm = pref
    while n % tm:
        tm //= 2
    return tm


def _layer(x, tables, cache_k, cache_v, s0, shift0, lw, gf, rw_sets, rw_segs):
    b, t, _ = x.shape
    n = b * t
    x2d = x.reshape(n, D_MODEL)
    tm = _tile(n, IN_TILE)
    q, k, v, zr, zg = _inproj(x2d, lw["norm_mix_g"], lw["w_in"], tables, tm)
    q3, k3, v3 = q.reshape(b, t, ATT_Q), k.reshape(b, t, ATT_KV), v.reshape(b, t, ATT_KV)
    if cache_k is None:
        o_att, k_win, v_win = _attn_prompt(q3, k3, v3, lw["attn_sinks"])
    else:
        o_att, k_win, v_win = _attn_sample(q3, k3, v3, lw["attn_sinks"],
                                           cache_k.reshape(b, WINDOW, ATT_KV), cache_v.reshape(b, WINDOW, ATT_KV))
    zr3 = zr.reshape(b, t, RW_COLS)
    o_rw, s_new = _rwkv(zr3, s0, shift0, lw, rw_sets, RW_CHUNK, rw_segs)
    tm2 = _tile(n, 512)
    x1, hn, comb, tile_counts = _mix(x2d, o_att.reshape(n, ATT_Q), o_rw.reshape(n, RW_WIDTH), zg, lw, tm2)
    y = _moe(hn, comb, tile_counts, x1, lw, gf, _tile(n, MOE_BLOCK))
    kv_shape = (b, WINDOW, N_KV_HEADS, HEAD_DIM)
    return (y.reshape(b, t, D_MODEL), k_win.reshape(kv_shape), v_win.reshape(kv_shape),
            s_new, zr3[:, t - 1, :])


def kernel(x_prompt, x_sample, cache_k, cache_v, state_wkv, state_shift, norm_mix_g, w_in, attn_sinks, rw_mu, rw_w0, rw_w_up, rw_a0, rw_a_up, rw_g_up, rw_k_k, rw_k_a, rw_r_k, rw_lnx_g, rw_lnx_b, w_branch_attn, w_branch_rwkv, w_out, norm_ffn_g, w_router, b_router, w_gate_up, b_gate_up, w_down, b_down, norm_final_g):
    assert w_in.shape[0] == 1, "single-layer trunk"
    bp, tp, _ = x_prompt.shape
    bs, ts, _ = x_sample.shape
    lw = _prep_layer(0, norm_mix_g, w_in, attn_sinks, rw_mu, rw_w0, rw_w_up, rw_a0, rw_a_up, rw_g_up,
                     rw_k_k, rw_k_a, rw_r_k, rw_lnx_g, rw_lnx_b, w_branch_attn, w_branch_rwkv, w_out,
                     norm_ffn_g, w_router, b_router, w_gate_up, b_gate_up, w_down, b_down)
    gf = norm_final_g.reshape(1, D_MODEL)

    tab_p = _rope_tables(np.arange(tp))
    tab_s = _rope_tables(PAST_LEN + np.arange(ts), reps=_tile(bs * ts, IN_TILE) // ts)

    s0p = jnp.zeros((bp, RW_HEADS, RW_HEAD, RW_HEAD), state_wkv.dtype)
    sh0p = jnp.zeros((bp, RW_COLS), state_shift.dtype)
    sets_p = RW_SEQS_PER_STEP if bp % RW_SEQS_PER_STEP == 0 else 1
    segs_s = RW_CHUNK // ts
    sets_s = 2 if bs % (2 * segs_s) == 0 else 1
    yp, kp, vp, sp, shp = _layer(x_prompt, tab_p, None, None, s0p, sh0p, lw, gf, sets_p, 1)
    ys, ks, vs, ss, shs = _layer(x_sample, tab_s, cache_k[0], cache_v[0], state_wkv[0], state_shift[0],
                                 lw, gf, sets_s, segs_s)
    ex = lambda u: u[None]
    return (yp, ys, ex(kp), ex(vp), ex(sp), ex(shp), ex(ks), ex(vs), ex(ss), ex(shs))
```

```python
import functools
import math

import jax
import jax.numpy as jnp
import numpy as np
from jax import lax
from jax.experimental import pallas as pl
from jax.experimental.pallas import tpu as pltpu

F32 = jnp.float32
BF16 = jnp.bfloat16

LANES = 128
SUBLANES = 8
VMEM_LIMIT_BYTES = 56 * 1024 * 1024

D_MODEL = 1024
HEAD_DIM = 64
N_Q_HEADS = 16
N_KV_HEADS = 4
Q_PER_KV = 4
WINDOW = 128
ROPE_THETA = 500000.0
ROPE_DIM = 16
ROPE_HALF = 8
ATTN_SCALE = HEAD_DIM ** -0.5
PAST_LEN = 16384
RW_HEAD = 64
RW_HEADS = 16
RW_PAIRS = RW_HEADS // 2
W_LORA = 64
A_LORA = 64
G_LORA = 128
LNX_EPS = 64e-5
N_EXPERTS = 32
TOP_K = 4
D_EXPERT = 1024
SWIGLU_LIMIT = 7.0
SWIGLU_ALPHA = 1.702
RMS_EPS = 1e-5
ATT_Q = N_Q_HEADS * HEAD_DIM
ATT_KV = N_KV_HEADS * HEAD_DIM
RW_WIDTH = RW_HEADS * RW_HEAD
RW_COLS = 3 * RW_WIDTH + W_LORA + A_LORA + G_LORA
GATE_COLS = 2 * D_MODEL
IN_COLS = ATT_Q + 2 * ATT_KV + RW_COLS + GATE_COLS
DECAY_SCALE = math.exp(-0.5)
RW_CHUNK = 64
IN_TILE = 512
MOE_BLOCK = 2048
RW_SEQS_PER_STEP = 4


def _nn(a, b):
    return jnp.dot(a, b, preferred_element_type=F32)


def _nt(a, b):
    return lax.dot_general(a, b, (((1,), (1,)), ((), ())), preferred_element_type=F32)


def _tn(a, b):
    return lax.dot_general(a, b, (((0,), (0,)), ((), ())), preferred_element_type=F32)


def _bf(x):
    return x.astype(BF16)


def _sigmoid(x):
    return 0.5 * jnp.tanh(0.5 * x) + 0.5


def _split2(x):
    hi = x.astype(BF16)
    lo = (x - hi.astype(F32)).astype(BF16)
    return hi, lo


def _split3(x):
    hi = x.astype(BF16)
    r1 = x - hi.astype(F32)
    mid = r1.astype(BF16)
    lo = (r1 - mid.astype(F32)).astype(BF16)
    return hi, mid, lo


def _params(n_axes):
    return pltpu.CompilerParams(
        dimension_semantics=("arbitrary",) * n_axes, vmem_limit_bytes=VMEM_LIMIT_BYTES)


_IN_CHUNK = 512


def _inproj_kernel(x_ref, g_ref, w_ref, cos_ref, sa_ref, sb_ref,
                   q_ref, k_ref, v_ref, zr_ref, zg_ref):
    x = x_ref[...]
    ms = jnp.mean(x * x, axis=-1, keepdims=True)
    h = _bf(x * lax.rsqrt(ms + RMS_EPS) * g_ref[...])
    cos, sa, sb = cos_ref[...], sa_ref[...], sb_ref[...]

    def rope(y):
        return (y * cos + pltpu.roll(y, LANES - ROPE_HALF, axis=1) * sa
                + pltpu.roll(y, ROPE_HALF, axis=1) * sb)

    def project(out_ref, col0, width, with_rope):
        for c in range(0, width, _IN_CHUNK):
            cw = min(_IN_CHUNK, width - c)
            acc = _nn(h, w_ref[:, col0 + c:col0 + c + cw])
            if with_rope:
                for j in range(0, cw, LANES):
                    out_ref[:, c + j:c + j + LANES] = rope(acc[:, j:j + LANES])
            else:
                out_ref[:, c:c + cw] = acc

    project(q_ref, 0, ATT_Q, True)
    project(k_ref, ATT_Q, ATT_KV, True)
    project(v_ref, ATT_Q + ATT_KV, ATT_KV, False)
    project(zr_ref, ATT_Q + 2 * ATT_KV, RW_COLS, False)
    project(zg_ref, ATT_Q + 2 * ATT_KV + RW_COLS, GATE_COLS, False)


def _rope_tables(positions, reps=1):
    f32 = np.float32
    inv_freq = (f32(ROPE_THETA) ** (-np.arange(ROPE_HALF, dtype=f32) / f32(ROPE_HALF))).astype(f32)
    ang = positions.astype(f32)[:, None] * inv_freq[None, :]
    cos, sin = np.cos(ang).astype(f32), np.sin(ang).astype(f32)
    t = positions.shape[0]
    one = np.ones((t, HEAD_DIM - ROPE_DIM), f32)
    zero = np.zeros((t, HEAD_DIM - ROPE_DIM), f32)
    z8 = np.zeros((t, ROPE_HALF), f32)
    cos_h = np.concatenate([cos, cos, one], axis=1)
    sa_h = np.concatenate([-sin, z8, zero], axis=1)
    sb_h = np.concatenate([z8, sin, zero], axis=1)
    full = lambda u: jnp.asarray(np.tile(np.concatenate([u, u], axis=1), (reps, 1)))
    return full(cos_h), full(sa_h), full(sb_h)


def _inproj(x2d, g, w_bf, tables, tm):
    n = x2d.shape[0]
    cos, sa, sb = tables
    nper = cos.shape[0] // tm
    row = lambda i: (i, 0)
    tab = lambda i: (i % nper, 0)
    const = lambda i: (0, 0)
    out_shapes = [jax.ShapeDtypeStruct((n, w), F32) for w in (ATT_Q, ATT_KV, ATT_KV, RW_COLS, GATE_COLS)]
    return pl.pallas_call(
        _inproj_kernel,
        grid=(n // tm,),
        in_specs=[
            pl.BlockSpec((tm, D_MODEL), row),
            pl.BlockSpec((1, D_MODEL), const),
            pl.BlockSpec((D_MODEL, IN_COLS), const, pipeline_mode=pl.Buffered(1)),
            pl.BlockSpec((tm, LANES), tab),
            pl.BlockSpec((tm, LANES), tab),
            pl.BlockSpec((tm, LANES), tab),
        ],
        out_specs=[pl.BlockSpec((tm, w), row) for w in (ATT_Q, ATT_KV, ATT_KV, RW_COLS, GATE_COLS)],
        out_shape=out_shapes,
        compiler_params=_params(1),
    )(x2d, g, w_bf, cos, sa, sb)


def _attn_prompt_kernel(sink_ref, q_ref, kp_ref, kc_ref, vp_ref, vc_ref, o_ref, kw_ref, vw_ref):
    n = pl.program_id(1)
    w = WINDOW
    lane = lax.broadcasted_iota(jnp.int32, (w, LANES), 1)
    first_half = lane < HEAD_DIM
    qi = lax.broadcasted_iota(jnp.int32, (Q_PER_KV * w, 2 * w), 0) % w
    kj = lax.broadcasted_iota(jnp.int32, (Q_PER_KV * w, 2 * w), 1)
    first_key = jnp.where(n > 0, 0, w)
    band4 = (kj > qi) & (kj <= qi + w) & (kj >= first_key)

    kw_ref[...] = kc_ref[...]
    vw_ref[...] = vc_ref[...]

    lane2 = lax.broadcasted_iota(jnp.int32, (2 * w, LANES), 1)
    kvs = range(N_KV_HEADS)
    k2, v2 = [], []
    for kv in kvs:
        gs = slice((kv // 2) * LANES, (kv // 2 + 1) * LANES)
        mine = (lane2 < HEAD_DIM) if kv % 2 == 0 else (lane2 >= HEAD_DIM)
        k2.append(_bf(jnp.concatenate([kp_ref[:, gs], kc_ref[:, gs]], axis=0)))
        v2.append(_bf(jnp.where(mine, jnp.concatenate([vp_ref[:, gs], vc_ref[:, gs]], axis=0), 1.0)))
    qs, sink = [], []
    for kv in kvs:
        jh = kv % 2
        keep = first_half if jh == 0 else jnp.logical_not(first_half)
        rows, sinks = [], []
        for gq in range(Q_PER_KV):
            hq = kv * Q_PER_KV + gq
            qg = q_ref[:, (hq // 2) * LANES:(hq // 2 + 1) * LANES]
            if hq % 2 != jh:
                qg = pltpu.roll(qg, HEAD_DIM, axis=1)
            rows.append(jnp.where(keep, qg * ATTN_SCALE, 0.0))
            sinks.append(jnp.full((w, 1), sink_ref[hq], F32))
        qs.append(_bf(jnp.concatenate(rows, axis=0)))
        sink.append(jnp.concatenate(sinks, axis=0))
    s = [jnp.where(band4, _nt(q_, k_), -jnp.inf) for q_, k_ in zip(qs, k2)]
    m = [jnp.maximum(jnp.max(s_, axis=-1, keepdims=True), sk) for s_, sk in zip(s, sink)]
    e = [_bf(jnp.exp(s_ - m_)) for s_, m_ in zip(s, m)]
    e_sink = [jnp.exp(sk - m_) for sk, m_ in zip(sink, m)]
    pv = [_nn(e_, v_) for e_, v_ in zip(e, v2)]
    for kv in kvs:
        jh = kv % 2
        for go in range(Q_PER_KV // 2):
            ra, rb = slice(2 * go * w, (2 * go + 1) * w), slice((2 * go + 1) * w, (2 * go + 2) * w)
            pa, pb = pv[kv][ra], pv[kv][rb]
            pa_r, pb_r = pltpu.roll(pa, HEAD_DIM, axis=1), pltpu.roll(pb, HEAD_DIM, axis=1)
            if jh == 0:
                oa, ob = pa / (pa_r + e_sink[kv][ra]), pb_r / (pb + e_sink[kv][rb])
            else:
                oa, ob = pa_r / (pa + e_sink[kv][ra]), pb / (pb_r + e_sink[kv][rb])
            og = kv * (Q_PER_KV // 2) + go
            o_ref[:, og * LANES:(og + 1) * LANES] = jnp.where(first_half, oa, ob).astype(o_ref.dtype)


def _attn_prompt(q, k, v, sinks):
    b, t, _ = q.shape
    nb = t // WINDOW
    cur = lambda bi, n: (bi, n, 0)
    prev = lambda bi, n: (bi, jnp.maximum(n - 1, 0), 0)
    win = lambda bi, n: (bi, 0, 0)
    return pl.pallas_call(
        _attn_prompt_kernel,
        grid=(b, nb),
        in_specs=[
            pl.BlockSpec(memory_space=pltpu.SMEM),
            pl.BlockSpec((None, WINDOW, ATT_Q), cur),
            pl.BlockSpec((None, WINDOW, ATT_KV), prev),
            pl.BlockSpec((None, WINDOW, ATT_KV), cur),
            pl.BlockSpec((None, WINDOW, ATT_KV), prev),
            pl.BlockSpec((None, WINDOW, ATT_KV), cur),
        ],
        out_specs=[
            pl.BlockSpec((None, WINDOW, ATT_Q), cur),
            pl.BlockSpec((None, WINDOW, ATT_KV), win),
            pl.BlockSpec((None, WINDOW, ATT_KV), win),
        ],
        out_shape=[
            jax.ShapeDtypeStruct((b, t, ATT_Q), BF16),
            jax.ShapeDtypeStruct((b, WINDOW, ATT_KV), F32),
            jax.ShapeDtypeStruct((b, WINDOW, ATT_KV), F32),
        ],
        compiler_params=_params(2),
    )(sinks, q, k, k, v, v)


_SAMPLE_BT = 8


def _attn_sample_kernel(sink_ref, q_ref, k_ref, v_ref, ck_ref, cv_ref, o_ref, nk_ref, nv_ref, *, t):
    w = WINDOW
    rows_per_grp = 2 * Q_PER_KV * t
    lane = lax.broadcasted_iota(jnp.int32, (t, LANES), 1)
    first_half = lane < HEAD_DIM
    r_c = lax.broadcasted_iota(jnp.int32, (rows_per_grp, w), 0) % t
    c_c = lax.broadcasted_iota(jnp.int32, (rows_per_grp, w), 1)
    mask_c = c_c > r_c
    r_n = lax.broadcasted_iota(jnp.int32, (rows_per_grp, t), 0) % t
    c_n = lax.broadcasted_iota(jnp.int32, (rows_per_grp, t), 1)
    mask_n = c_n <= r_n

    for bi in range(_SAMPLE_BT):
        nk_ref[bi, 0:w - t, :] = ck_ref[bi, t:w, :]
        nk_ref[bi, w - t:w, :] = k_ref[bi]
        nv_ref[bi, 0:w - t, :] = cv_ref[bi, t:w, :]
        nv_ref[bi, w - t:w, :] = v_ref[bi]

    chains = [(bi, grp) for bi in range(_SAMPLE_BT) for grp in range(ATT_KV // LANES)]
    gsl = lambda grp: slice(grp * LANES, (grp + 1) * LANES)
    kc = [_bf(ck_ref[bi, :, gsl(grp)]) for bi, grp in chains]
    vc = [_bf(cv_ref[bi, :, gsl(grp)]) for bi, grp in chains]
    kn = [_bf(k_ref[bi, :, gsl(grp)]) for bi, grp in chains]
    vn = [_bf(v_ref[bi, :, gsl(grp)]) for bi, grp in chains]
    sink_rows = [[jnp.full((t, 1), sink_ref[(grp * 2 + jh) * Q_PER_KV + gq], F32)
                  for jh in range(2) for gq in range(Q_PER_KV)] for grp in range(ATT_KV // LANES)]
    sink_g = [jnp.concatenate(r_, axis=0) for r_ in sink_rows]
    qs = []
    for bi, grp in chains:
        rows = []
        for jh in range(2):
            keep = first_half if jh == 0 else jnp.logical_not(first_half)
            for gq in range(Q_PER_KV):
                hq = (grp * 2 + jh) * Q_PER_KV + gq
                qg = q_ref[bi, :, (hq // 2) * LANES:(hq // 2 + 1) * LANES]
                if hq % 2 != jh:
                    qg = pltpu.roll(qg, HEAD_DIM, axis=1)
                rows.append(jnp.where(keep, qg * ATTN_SCALE, 0.0))
        qs.append(_bf(jnp.concatenate(rows, axis=0)))
    sink = [sink_g[grp] for _, grp in chains]
    s_c = [jnp.where(mask_c, _nt(q_, k_), -jnp.inf) for q_, k_ in zip(qs, kc)]
    s_n = [jnp.where(mask_n, _nt(q_, k_), -jnp.inf) for q_, k_ in zip(qs, kn)]
    m = [jnp.maximum(jnp.maximum(jnp.max(a, axis=-1, keepdims=True), jnp.max(b_, axis=-1, keepdims=True)), sk)
         for a, b_, sk in zip(s_c, s_n, sink)]
    e_c = [jnp.exp(a - m_) for a, m_ in zip(s_c, m)]
    e_n = [jnp.exp(a - m_) for a, m_ in zip(s_n, m)]
    denom = [jnp.sum(a, axis=-1, keepdims=True) + jnp.sum(b_, axis=-1, keepdims=True) + jnp.exp(sk - m_)
             for a, b_, sk, m_ in zip(e_c, e_n, sink, m)]
    pv = [_nn(_bf(a / d_), va) + _nn(_bf(b_ / d_), vb)
          for a, b_, d_, va, vb in zip(e_c, e_n, denom, vc, vn)]
    for (bi, grp), pv_ in zip(chains, pv):
        for jh in range(2):
            for go in range(Q_PER_KV // 2):
                r0 = (jh * Q_PER_KV + 2 * go) * t
                pa, pb = pv_[r0:r0 + t], pv_[r0 + t:r0 + 2 * t]
                if jh == 0:
                    pb = pltpu.roll(pb, HEAD_DIM, axis=1)
                else:
                    pa = pltpu.roll(pa, HEAD_DIM, axis=1)
                og = (grp * 2 + jh) * (Q_PER_KV // 2) + go
                o_ref[bi, :, og * LANES:(og + 1) * LANES] = jnp.where(first_half, pa, pb)


def _attn_sample(q, k, v, sinks, cache_k, cache_v):
    b, t, _ = q.shape
    bt = _SAMPLE_BT
    blk = lambda i: (i, 0, 0)
    return pl.pallas_call(
        functools.partial(_attn_sample_kernel, t=t),
        grid=(b // bt,),
        in_specs=[
            pl.BlockSpec(memory_space=pltpu.SMEM),
            pl.BlockSpec((bt, t, ATT_Q), blk),
            pl.BlockSpec((bt, t, ATT_KV), blk),
            pl.BlockSpec((bt, t, ATT_KV), blk),
            pl.BlockSpec((bt, WINDOW, ATT_KV), blk),
            pl.BlockSpec((bt, WINDOW, ATT_KV), blk),
        ],
        out_specs=[
            pl.BlockSpec((bt, t, ATT_Q), blk),
            pl.BlockSpec((bt, WINDOW, ATT_KV), blk),
            pl.BlockSpec((bt, WINDOW, ATT_KV), blk),
        ],
        out_shape=[
            jax.ShapeDtypeStruct((b, t, ATT_Q), F32),
            jax.ShapeDtypeStruct((b, WINDOW, ATT_KV), F32),
            jax.ShapeDtypeStruct((b, WINDOW, ATT_KV), F32),
        ],
        compiler_params=_params(1),
    )(sinks, q, k, v, cache_k, cache_v)


def _rwkv_kernel(z_ref, s0_ref, shift_ref, mu_ref, w0_ref, wa_up_ref, a0_ref, g_up_ref,
                 kk_ref, ka_ref, rk_ref, lng_ref, lnb_ref, o_ref, sout_ref, s_ref, prev_ref, *, nb, c, nseg):
    tseg = c // nseg
    rows_all = nb * c
    step = pl.program_id(1)
    pairs = range(RW_PAIRS)
    sls = [slice(p * LANES, (p + 1) * LANES) for p in pairs]
    rsl = [slice(j * c, (j + 1) * c) for j in range(nb)]

    @pl.when(step == 0)
    def _():
        zero = jnp.zeros((RW_HEAD, RW_HEAD), F32)
        for q in range(nb * nseg):
            for p in pairs:
                top = jnp.concatenate([s0_ref[q, 2 * p], zero], axis=1)
                bot = jnp.concatenate([zero, s0_ref[q, 2 * p + 1]], axis=1)
                s_ref[q, p] = jnp.concatenate([top, bot], axis=0)
        if nseg == 1:
            prev_ref[...] = shift_ref[...].reshape(nb, RW_COLS)

    z = z_ref[...].reshape(rows_all, RW_COLS)
    row1 = lax.broadcasted_iota(jnp.int32, (rows_all, 1), 0)
    zprev = pltpu.roll(z, 1, axis=0)
    if nseg == 1:
        for j in range(nb):
            zprev = jnp.where(row1 == j * c, prev_ref[j:j + 1, :], zprev)
            prev_ref[j:j + 1, :] = z[(j + 1) * c - 1:(j + 1) * c, :]
    else:
        zprev = jnp.where(row1 % tseg == 0, shift_ref[...].reshape(rows_all, RW_COLS), zprev)
    zs = z + (zprev - z) * mu_ref[...]

    w3 = 3 * RW_WIDTH
    r, k, v = zs[:, 0:RW_WIDTH], zs[:, RW_WIDTH:2 * RW_WIDTH], zs[:, 2 * RW_WIDTH:w3]
    xwa = zs[:, w3:w3 + LANES]
    xg = zs[:, w3 + LANES:w3 + 2 * LANES]
    lane_all = lax.broadcasted_iota(jnp.int32, (rows_all, LANES), 1)
    lora = _nn(_bf(jnp.where(lane_all < RW_HEAD, jnp.tanh(xwa), xwa)), wa_up_ref[...])
    lane = lax.broadcasted_iota(jnp.int32, (c, LANES), 1)
    head0 = lane < RW_HEAD
    lw = -DECAY_SCALE * _sigmoid(w0_ref[...] + lora[:, 0:RW_WIDTH])
    a_sig = _sigmoid(a0_ref[...] + lora[:, RW_WIDTH:2 * RW_WIDTH])
    g = _nn(_bf(_sigmoid(xg)), g_up_ref[...])
    kk = k * kk_ref[...]
    k = k * (1.0 + (a_sig - 1.0) * ka_ref[...])
    rkr = r * k * rk_ref[...]

    ti = lax.broadcasted_iota(jnp.int32, (rows_all, rows_all), 0)
    tj = lax.broadcasted_iota(jnp.int32, (rows_all, rows_all), 1)
    same_seq = (ti // tseg) == (tj // tseg)
    tri = jnp.where((tj <= ti) & same_seq, 1.0, 0.0).astype(BF16)
    lw3 = _split3(lw)
    cum = _nn(tri, lw3[0]) + _nn(tri, lw3[1]) + _nn(tri, lw3[2])
    if nseg == 1:
        ends = [jnp.broadcast_to(cum[(j + 1) * c - 1:(j + 1) * c, :], (c, RW_WIDTH)) for j in range(nb)]
        c_end = ends[0] if nb == 1 else jnp.concatenate(ends, axis=0)
    else:
        seq1 = jnp.where(same_seq, 1.0, 0.0).astype(BF16)
        c_end = _nn(seq1, lw3[0]) + _nn(seq1, lw3[1]) + _nn(seq1, lw3[2])
    e_c, e_ci, e_cm = jnp.exp(cum), jnp.exp(-cum), jnp.exp(cum - lw)
    e_ce, w_end = jnp.exp(c_end - cum), jnp.exp(c_end)

    gi = lax.broadcasted_iota(jnp.int32, (2 * LANES, LANES), 0) % LANES
    gj = lax.broadcasted_iota(jnp.int32, (2 * LANES, LANES), 1)
    ones2 = jnp.where((gi // RW_HEAD) == (gj // RW_HEAD), 1.0, 0.0).astype(BF16)
    bi_ = lax.broadcasted_iota(jnp.int32, (LANES, LANES), 0)
    bj_ = lax.broadcasted_iota(jnp.int32, (LANES, LANES), 1)
    same_head = (bi_ // RW_HEAD) == (bj_ // RW_HEAD)
    ci = lax.broadcasted_iota(jnp.int32, (c, 2 * c), 0)
    cj = lax.broadcasted_iota(jnp.int32, (c, 2 * c), 1)
    cjm = cj % c
    seq_ok = (ci // tseg) == (cjm // tseg)
    strict = (cjm < ci) & seq_ok
    incl = (cjm <= ci) & seq_ok
    eye_cat = jnp.where(cjm == ci, 1.0, 0.0)
    left = cj < c

    def seg_sum(x):
        hi, lo = _split2(x)
        return _nn(jnp.concatenate([hi, lo], axis=1), ones2)

    def rows2(x):
        return jnp.concatenate([jnp.where(head0, x, 0.0), jnp.where(head0, 0.0, x)], axis=0)

    def bd(cat):
        return _bf(jnp.concatenate([jnp.where(left, cat, 0.0), jnp.where(left, 0.0, cat)], axis=0))

    def pair_mm(cat, x):
        return _nn(_bf(cat), _bf(rows2(x)))

    def seg_sums(xs):
        tot = seg_sum(jnp.concatenate(xs, axis=0))
        return [tot[i * c:(i + 1) * c] for i in range(len(xs))]

    segs = range(nseg)
    rs = [slice(q * tseg, (q + 1) * tseg) for q in segs]

    def run(chains):
        gs = [(rsl[j], sls[p]) for j, p in chains]
        idx = range(len(chains))
        ss = seg_sums([kk[s] * kk[s] for s in gs])
        kkn = [kk[s] * lax.rsqrt(jnp.maximum(q, 1e-24)) for s, q in zip(gs, ss)]
        bv = [n_ * a_sig[s] for s, n_ in zip(gs, kkn)]
        rt = [r[s] * e_c[s] for s in gs]
        kt = [k[s] * e_ci[s] for s in gs]
        at = [-n_ * e_cm[s] for s, n_ in zip(gs, kkn)]
        bt = [b_ * e_ci[s] for s, b_ in zip(gs, bv)]
        bh = [b_ * e_ce[s] for s, b_ in zip(gs, bv)]
        kh = [k[s] * e_ce[s] for s in gs]
        vv = [v[s] for s in gs]

        ar = [_bf(jnp.concatenate([a_, r_], axis=0)) for a_, r_ in zip(at, rt)]
        xbk = [_nt(x, _bf(jnp.concatenate([rows2(b_), rows2(k_)], axis=0)))
               for x, b_, k_ in zip(ar, bt, kt)]
        l_ab = [jnp.where(strict, x[0:c, 0:2 * c], 0.0) for x in xbk]
        l_ak = [jnp.where(strict, x[0:c, 2 * c:4 * c], 0.0) for x in xbk]
        m_rb = [jnp.where(incl, x[c:2 * c, 0:2 * c], 0.0) for x in xbk]
        m_rk = [jnp.where(incl, x[c:2 * c, 2 * c:4 * c], 0.0) for x in xbk]

        t_inv = [l + eye_cat for l in l_ab]
        pw = l_ab
        pw_bd = [bd(x) for x in pw]
        for _ in range(int(math.log2(tseg)) - 1):
            pw = [_nn(_bf(x), xb_) for x, xb_ in zip(pw, pw_bd)]
            pw_bd = [bd(x) for x in pw]
            t_inv = [t + _nn(_bf(t), xb_) for t, xb_ in zip(t_inv, pw_bd)]

        lvy = [pair_mm(jnp.concatenate([l, m], axis=0), v_) for l, m, v_ in zip(l_ak, m_rk, vv)]
        y1 = [x[c:2 * c] for x in lvy]
        au = [_nn(_bf(t), _bf(jnp.concatenate([rows2(a_), rows2(x[0:c])], axis=1)))
              for t, a_, x in zip(t_inv, at, lvy)]
        a_hat = [x[:, 0:LANES] for x in au]
        u0 = [x[:, LANES:2 * LANES] for x in au]

        s_old = [[s_ref[j * nseg + q, p] for q in segs] for j, p in chains]
        pp = [[_nt(_bf(jnp.concatenate([a_hat[i][rs[q]], rt[i][rs[q]]], axis=0)), _bf(s_old[i][q]))
               for q in segs] for i in idx]
        u = [jnp.concatenate([pp[i][q][0:tseg] for q in segs], axis=0) + u0[i] for i in idx]
        y0 = [jnp.concatenate([pp[i][q][tseg:2 * tseg] for q in segs], axis=0) for i in idx]
        y = [y0[i] + pair_mm(m_rb[i], u[i]) + y1[i] for i in idx]
        for i in idx:
            for q in segs:
                upd = _tn(_bf(jnp.concatenate([u[i][rs[q]], vv[i][rs[q]]], axis=0)),
                          _bf(jnp.concatenate([bh[i][rs[q]], kh[i][rs[q]]], axis=0)))
                j, p = chains[i]
                row0 = (j * c + q * tseg) % w_end.shape[0]
                w_q = w_end[row0:row0 + 1, sls[p]]
                s_ref[j * nseg + q, p] = s_old[i][q] * w_q + jnp.where(same_head, upd, 0.0)

        mean = [x * (1.0 / RW_HEAD) for x in seg_sums(y)]
        d = [x - m for x, m in zip(y, mean)]
        var = [x * (1.0 / RW_HEAD) for x in seg_sums([x * x for x in d])]
        bonus = [x * v_ for x, v_ in zip(seg_sums([rkr[s] for s in gs]), vv)]
        for i in idx:
            j, p = chains[i]
            yn = d[i] * lax.rsqrt(var[i] + LNX_EPS) * lng_ref[:, sls[p]] + lnb_ref[:, sls[p]]
            o_ref[j, :, sls[p]] = ((yn + bonus[i]) * g[gs[i]]).astype(o_ref.dtype)

    run([(j, p) for j in range(nb) for p in pairs])

    @pl.when(step == pl.num_programs(1) - 1)
    def _():
        for q in range(nb * nseg):
            for p in pairs:
                tile = s_ref[q, p]
                sout_ref[q, 2 * p] = tile[0:RW_HEAD, 0:RW_HEAD]
                sout_ref[q, 2 * p + 1] = tile[RW_HEAD:2 * RW_HEAD, RW_HEAD:2 * RW_HEAD]


def _rwkv(zr, s0, shift0, lw, nb, c, nseg):
    b, t, _ = zr.shape
    if nseg == 1:
        ngrp, nchunk = b // nb, t // c
        z3 = zr
        shift = shift0.reshape(b, 1, RW_COLS)
        shift_spec = pl.BlockSpec((nb, 1, RW_COLS), lambda bi, i: (bi, 0, 0))
    else:
        assert c == nseg * t
        nsets = b // nseg
        ngrp, nchunk = nsets // nb, 1
        z3 = zr.reshape(nsets, c, RW_COLS)
        shift = jnp.pad(shift0[:, None, :], ((0, 0), (0, t - 1), (0, 0))).reshape(nsets, c, RW_COLS)
        shift_spec = pl.BlockSpec((nb, c, RW_COLS), lambda bi, i: (bi, 0, 0))
    vec = lambda name: lw[name].reshape(1, -1).astype(F32)
    cst = lambda bi, i: (0, 0)
    vspec = lambda wd: pl.BlockSpec((1, wd), cst)
    state_spec = pl.BlockSpec((nb * nseg, RW_HEADS, RW_HEAD, RW_HEAD), lambda bi, i: (bi, 0, 0, 0))
    o, s_new = pl.pallas_call(
        functools.partial(_rwkv_kernel, nb=nb, c=c, nseg=nseg),
        grid=(ngrp, nchunk),
        in_specs=[
            pl.BlockSpec((nb, c, RW_COLS), lambda bi, i: (bi, i, 0)),
            state_spec,
            shift_spec,
            vspec(RW_COLS), vspec(RW_WIDTH),
            pl.BlockSpec((LANES, 2 * RW_WIDTH), cst),
            vspec(RW_WIDTH),
            pl.BlockSpec((G_LORA, RW_WIDTH), cst),
            vspec(RW_WIDTH), vspec(RW_WIDTH), vspec(RW_WIDTH), vspec(RW_WIDTH), vspec(RW_WIDTH),
        ],
        out_specs=[
            pl.BlockSpec((nb, c, RW_WIDTH), lambda bi, i: (bi, i, 0)),
            state_spec,
        ],
        out_shape=[
            jax.ShapeDtypeStruct((ngrp * nb, nchunk * c, RW_WIDTH), BF16),
            jax.ShapeDtypeStruct((b, RW_HEADS, RW_HEAD, RW_HEAD), F32),
        ],
        scratch_shapes=[pltpu.VMEM((nb * nseg, RW_PAIRS, LANES, LANES), F32), pltpu.VMEM((nb, RW_COLS), F32)],
        compiler_params=_params(2),
    )(z3, s0, shift, vec("rw_mu"), vec("rw_w0"), lw["wa_up"], vec("rw_a0"),
      lw["g_up"], vec("rw_k_k"), vec("rw_k_a"), vec("rw_r_k"), vec("rw_lnx_g"), vec("rw_lnx_b"))
    return o.reshape(b, t, RW_WIDTH), s_new


def _mix_kernel(x_ref, oa_ref, or_ref, zg_ref, wba_ref, wbr_ref, wo_ref, g_ref, wrt_ref, brt_ref,
                x1_ref, hn_ref, combt_ref):
    ya = _nn(_bf(oa_ref[...]), wba_ref[...])
    yr = _nn(_bf(or_ref[...]), wbr_ref[...])
    merged = _sigmoid(zg_ref[:, 0:D_MODEL]) * ya + _sigmoid(zg_ref[:, D_MODEL:2 * D_MODEL]) * yr
    x1 = x_ref[...] + _nn(_bf(merged), wo_ref[...])
    x1_ref[...] = x1
    ms = jnp.mean(x1 * x1, axis=-1, keepdims=True)
    hn = x1 * lax.rsqrt(ms + RMS_EPS) * g_ref[...]
    hn_ref[...] = _bf(hn)
    logits = _nt(wrt_ref[...], _bf(hn)) + brt_ref[...]
    row = lax.broadcasted_iota(jnp.int32, logits.shape, 0).astype(F32)
    work = logits
    top = None
    for _ in range(TOP_K):
        m = jnp.max(work, axis=0, keepdims=True)
        if top is None:
            top = m
        idx = jnp.min(jnp.where(work == m, row, float(N_EXPERTS)), axis=0, keepdims=True)
        work = jnp.where(row == idx, -jnp.inf, work)
    e = jnp.where(work != logits, jnp.exp(logits - top), 0.0)
    combt_ref[...] = e / jnp.sum(e, axis=0, keepdims=True)


def _mix(x2d, oa, orw, zg, lw, tm):
    n = x2d.shape[0]
    row = lambda i: (i, 0)
    cst = lambda i: (0, 0)
    wspec = pl.BlockSpec((D_MODEL, D_MODEL), cst)
    return pl.pallas_call(
        _mix_kernel,
        grid=(n // tm,),
        in_specs=[
            pl.BlockSpec((tm, D_MODEL), row), pl.BlockSpec((tm, D_MODEL), row), pl.BlockSpec((tm, D_MODEL), row),
            pl.BlockSpec((tm, GATE_COLS), row),
            wspec, wspec, wspec,
            pl.BlockSpec((1, D_MODEL), cst),
            pl.BlockSpec((N_EXPERTS, D_MODEL), cst),
            pl.BlockSpec((N_EXPERTS, 1), cst),
        ],
        out_specs=[pl.BlockSpec((tm, D_MODEL), row), pl.BlockSpec((tm, D_MODEL), row),
                   pl.BlockSpec((N_EXPERTS, tm), lambda i: (0, i))],
        out_shape=[jax.ShapeDtypeStruct((n, D_MODEL), F32), jax.ShapeDtypeStruct((n, D_MODEL), BF16),
                   jax.ShapeDtypeStruct((N_EXPERTS, n), F32)],
        compiler_params=_params(1),
    )(x2d, oa, orw, zg, lw["w_ba"], lw["w_br"], lw["w_o"], lw["norm_ffn_g"], lw["w_rt"], lw["b_rt"])


_GU_GROUP = 2 * LANES


def _gu_regroup_kernel(w_ref, o_ref):
    src = lax.broadcasted_iota(jnp.int32, (_GU_GROUP, _GU_GROUP), 0)
    dst = lax.broadcasted_iota(jnp.int32, (_GU_GROUP, _GU_GROUP), 1)
    want = jnp.where(dst < LANES, 2 * dst, 2 * (dst - LANES) + 1)
    perm = jnp.where(src == want, 1.0, 0.0).astype(BF16)
    for j in range(0, 2 * D_EXPERT, _GU_GROUP):
        o_ref[:, j:j + _GU_GROUP] = _nn(_bf(w_ref[:, j:j + _GU_GROUP]), perm).astype(o_ref.dtype)


def _gu_regroup(w_gate_up):
    e, d, n = w_gate_up.shape
    tk = 512
    spec = pl.BlockSpec((None, tk, n), lambda i, j: (i, j, 0))
    return pl.pallas_call(
        _gu_regroup_kernel,
        grid=(e, d // tk),
        in_specs=[spec],
        out_specs=spec,
        out_shape=jax.ShapeDtypeStruct((e, d, n), BF16),
        compiler_params=_params(2),
    )(w_gate_up)


_MOE_DOMAIN = 1024
_MOE_ROWS = 160
_RANK_CHUNK = 256


def _moe_kernel(nsub_ref, hn_ref, comb_t_ref, x1_ref, wgu_ref, bgu_ref, wd_ref, bd_ref, gf_ref, out_ref,
                rank_t_ref, act_ref):
    blk, e = pl.program_id(0), pl.program_id(1)
    tb = hn_ref.shape[0]
    dom, rows = min(_MOE_DOMAIN, tb), _MOE_ROWS
    ndom = tb // dom

    @pl.when(e == 0)
    def _():
        out_ref[...] = jnp.zeros_like(out_ref)
        ci = lax.broadcasted_iota(jnp.int32, (_RANK_CHUNK, _RANK_CHUNK), 0)
        cj = lax.broadcasted_iota(jnp.int32, (_RANK_CHUNK, _RANK_CHUNK), 1)
        earlier = jnp.where(ci < cj, 1.0, 0.0).astype(BF16)
        for d0 in range(0, tb, dom):
            seen = jnp.zeros((N_EXPERTS, 1), F32)
            for c0 in range(d0, d0 + dom, _RANK_CHUNK):
                routed = comb_t_ref[:, c0:c0 + _RANK_CHUNK] > 0.0
                hot = jnp.where(routed, 1.0, 0.0)
                rank_t_ref[:, c0:c0 + _RANK_CHUNK] = jnp.where(routed, _nn(_bf(hot), earlier) + seen, -1.0)
                seen = seen + jnp.sum(hot, axis=1, keepdims=True)

    slot_r = lax.broadcasted_iota(jnp.int32, (rows, dom), 0).astype(F32)

    doms = [slice(d * dom, (d + 1) * dom) for d in range(ndom)]
    rank_rows = [rank_t_ref[pl.ds(e, 1), ds_] for ds_ in doms]
    w_rows_all = [comb_t_ref[pl.ds(e, 1), ds_] for ds_ in doms]
    trips = [nsub_ref[(blk * ndom + d) * N_EXPERTS + e] for d in range(ndom)]
    trip = trips[0]
    for t_ in trips[1:]:
        trip = jnp.maximum(trip, t_)

    def sub_tiles(s, carry):
        base = (s * rows).astype(F32)
        for d, ds_ in enumerate(doms):
            hit = rank_rows[d] - base == slot_r
            gather = jnp.where(hit, 1.0, 0.0).astype(BF16)
            x = _bf(_nn(gather, hn_ref[ds_, :]))
            w_rows = jnp.sum(jnp.where(hit, w_rows_all[d], 0.0), axis=1, keepdims=True)
            for j in range(D_EXPERT // LANES):
                gs = slice(j * _GU_GROUP, (j + 1) * _GU_GROUP)
                gu = _nn(x, wgu_ref[:, gs]) + bgu_ref[:, gs]
                glu = jnp.minimum(gu[:, 0:LANES], SWIGLU_LIMIT)
                lin = jnp.clip(gu[:, LANES:_GU_GROUP], -SWIGLU_LIMIT, SWIGLU_LIMIT)
                act_ref[d, :, j * LANES:(j + 1) * LANES] = _bf(glu * _sigmoid(SWIGLU_ALPHA * glu) * (lin + 1.0))
            y = (_nn(act_ref[d], _bf(wd_ref[...])) + bd_ref[...]) * w_rows
            out_ref[ds_, :] += _tn(gather, _bf(y))
        return carry

    lax.fori_loop(0, trip, sub_tiles, 0)

    part = x1_ref.shape[0]
    res_rows = pl.ds(pl.multiple_of(e * part, part), part)
    out_ref[res_rows, :] += x1_ref[...]

    @pl.when(e == N_EXPERTS - 1)
    def _():
        for r0 in range(0, tb, _RANK_CHUNK):
            xo = out_ref[r0:r0 + _RANK_CHUNK, :]
            ms = jnp.mean(xo * xo, axis=-1, keepdims=True)
            out_ref[r0:r0 + _RANK_CHUNK, :] = xo * lax.rsqrt(ms + RMS_EPS) * gf_ref[...]


def _moe(hn, comb_t, x1, lw, gf, tb):
    n = hn.shape[0]
    nblk = n // tb
    dom = min(_MOE_DOMAIN, tb)
    counts = jnp.sum((comb_t > 0.0).reshape(N_EXPERTS, n // dom, dom), axis=-1, dtype=jnp.int32)
    nsub = ((counts.T + _MOE_ROWS - 1) // _MOE_ROWS).reshape(-1)
    row = lambda i, e, ns: (i, 0)
    ex = lambda i, e, ns: (e, 0, 0)
    return pl.pallas_call(
        _moe_kernel,
        grid_spec=pltpu.PrefetchScalarGridSpec(
            num_scalar_prefetch=1,
            grid=(nblk, N_EXPERTS),
            in_specs=[
                pl.BlockSpec((tb, D_MODEL), row), pl.BlockSpec((N_EXPERTS, tb), lambda i, e, ns: (0, i)),
                pl.BlockSpec((tb // N_EXPERTS, D_MODEL), lambda i, e, ns: (i * N_EXPERTS + e, 0)),
                pl.BlockSpec((None, D_MODEL, 2 * D_EXPERT), ex), pl.BlockSpec((None, 1, 2 * D_EXPERT), ex),
                pl.BlockSpec((None, D_EXPERT, D_MODEL), ex), pl.BlockSpec((None, 1, D_MODEL), ex),
                pl.BlockSpec((1, D_MODEL), lambda i, e, ns: (0, 0)),
            ],
            out_specs=pl.BlockSpec((tb, D_MODEL), row),
            scratch_shapes=[pltpu.VMEM((N_EXPERTS, tb), F32),
                            pltpu.VMEM((tb // dom, _MOE_ROWS, D_EXPERT), BF16)],
        ),
        out_shape=jax.ShapeDtypeStruct((n, D_MODEL), F32),
        compiler_params=_params(2),
    )(nsub, hn, comb_t, x1, lw["w_gu"], lw["b_gu"], lw["w_down"], lw["b_down"], gf)


def _prep_layer(l, norm_mix_g, w_in, attn_sinks, rw_mu, rw_w0, rw_w_up, rw_a0, rw_a_up, rw_g_up,
                rw_k_k, rw_k_a, rw_r_k, rw_lnx_g, rw_lnx_b, w_branch_attn, w_branch_rwkv, w_out,
                norm_ffn_g, w_router, b_router, w_gate_up, b_gate_up, w_down, b_down):
    zeros = jnp.zeros((W_LORA, RW_WIDTH), F32)
    wa_up = jnp.concatenate([jnp.concatenate([rw_w_up[l], zeros], axis=1),
                             jnp.concatenate([zeros, rw_a_up[l]], axis=1)], axis=0)
    return {
        "norm_mix_g": norm_mix_g[l].reshape(1, D_MODEL),
        "w_in": _bf(w_in[l]),
        "attn_sinks": attn_sinks[l].astype(F32),
        "rw_mu": rw_mu[l], "rw_w0": rw_w0[l], "rw_a0": rw_a0[l], "rw_k_k": rw_k_k[l], "rw_k_a": rw_k_a[l],
        "rw_r_k": rw_r_k[l], "rw_lnx_g": rw_lnx_g[l], "rw_lnx_b": rw_lnx_b[l],
        "wa_up": _bf(wa_up), "g_up": _bf(rw_g_up[l]),
        "w_ba": _bf(w_branch_attn[l]), "w_br": _bf(w_branch_rwkv[l]), "w_o": _bf(w_out[l]),
        "norm_ffn_g": norm_ffn_g[l].reshape(1, D_MODEL),
        "w_rt": _bf(w_router[l].T), "b_rt": b_router[l].reshape(N_EXPERTS, 1).astype(F32),
        "w_gu": _gu_regroup(w_gate_up[l]),
        "b_gu": b_gate_up[l].reshape(N_EXPERTS, D_EXPERT // LANES, LANES, 2).transpose(0, 1, 3, 2)
                .reshape(N_EXPERTS, 1, 2 * D_EXPERT),
        "w_down": w_down[l], "b_down": b_down[l][:, None, :],
    }


def _tile(n, pref):
    tm = pref
    while n % tm:
        tm //= 2
    return tm


def _layer(x, tables, cache_k, cache_v, s0, shift0, lw, gf, rw_sets, rw_segs):
    b, t, _ = x.shape
    n = b * t
    x2d = x.reshape(n, D_MODEL)
    tm = _tile(n, IN_TILE)
    q, k, v, zr, zg = _inproj(x2d, lw["norm_mix_g"], lw["w_in"], tables, tm)
    q3, k3, v3 = q.reshape(b, t, ATT_Q), k.reshape(b, t, ATT_KV), v.reshape(b, t, ATT_KV)
    if cache_k is None:
        o_att, k_win, v_win = _attn_prompt(q3, k3, v3, lw["attn_sinks"])
    else:
        o_att, k_win, v_win = _attn_sample(q3, k3, v3, lw["attn_sinks"],
                                           cache_k.reshape(b, WINDOW, ATT_KV), cache_v.reshape(b, WINDOW, ATT_KV))
    zr3 = zr.reshape(b, t, RW_COLS)
    o_rw, s_new = _rwkv(zr3, s0, shift0, lw, rw_sets, RW_CHUNK, rw_segs)
    tm2 = _tile(n, 512)
    x1, hn, comb_t = _mix(x2d, o_att.reshape(n, ATT_Q), o_rw.reshape(n, RW_WIDTH), zg, lw, tm2)
    y = _moe(hn, comb_t, x1, lw, gf, _tile(n, MOE_BLOCK))
    kv_shape = (b, WINDOW, N_KV_HEADS, HEAD_DIM)
    return (y.reshape(b, t, D_MODEL), k_win.reshape(kv_shape), v_win.reshape(kv_shape),
            s_new, zr3[:, t - 1, :])


def kernel(x_prompt, x_sample, cache_k, cache_v, state_wkv, state_shift, norm_mix_g, w_in, attn_sinks, rw_mu, rw_w0, rw_w_up, rw_a0, rw_a_up, rw_g_up, rw_k_k, rw_k_a, rw_r_k, rw_lnx_g, rw_lnx_b, w_branch_attn, w_branch_rwkv, w_out, norm_ffn_g, w_router, b_router, w_gate_up, b_gate_up, w_down, b_down, norm_final_g):
    assert w_in.shape[0] == 1, "single-layer trunk"
    bp, tp, _ = x_prompt.shape
    bs, ts, _ = x_sample.shape
    lw = _prep_layer(0, norm_mix_g, w_in, attn_sinks, rw_mu, rw_w0, rw_w_up, rw_a0, rw_a_up, rw_g_up,
                     rw_k_k, rw_k_a, rw_r_k, rw_lnx_g, rw_lnx_b, w_branch_attn, w_branch_rwkv, w_out,
                     norm_ffn_g, w_router, b_router, w_gate_up, b_gate_up, w_down, b_down)
    gf = norm_final_g.reshape(1, D_MODEL)

    tab_p = _rope_tables(np.arange(tp))
    tab_s = _rope_tables(PAST_LEN + np.arange(ts), reps=_tile(bs * ts, IN_TILE) // ts)

    s0p = jnp.zeros((bp, RW_HEADS, RW_HEAD, RW_HEAD), state_wkv.dtype)
    sh0p = jnp.zeros((bp, RW_COLS), state_shift.dtype)
    sets_p = RW_SEQS_PER_STEP if bp % RW_SEQS_PER_STEP == 0 else 1
    segs_s = RW_CHUNK // ts
    sets_s = 2 if bs % (2 * segs_s) == 0 else 1
    yp, kp, vp, sp, shp = _layer(x_prompt, tab_p, None, None, s0p, sh0p, lw, gf, sets_p, 1)
    ys, ks, vs, ss, shs = _layer(x_sample, tab_s, cache_k[0], cache_v[0], state_wkv[0], state_shift[0],
                                 lw, gf, sets_s, segs_s)
    ex = lambda u: u[None]
    return (yp, ys, ex(kp), ex(vp), ex(sp), ex(shp), ex(ks), ex(vs), ex(ss), ex(shs))
```

```python
import functools
import math

import jax
import jax.numpy as jnp
import numpy as np
from jax import lax
from jax.experimental import pallas as pl
from jax.experimental.pallas import tpu as pltpu

F32 = jnp.float32
BF16 = jnp.bfloat16

LANES = 128
SUBLANES = 8
VMEM_LIMIT_BYTES = 56 * 1024 * 1024

D_MODEL = 1024
HEAD_DIM = 64
N_Q_HEADS = 16
N_KV_HEADS = 4
Q_PER_KV = 4
WINDOW = 128
ROPE_THETA = 500000.0
ROPE_DIM = 16
ROPE_HALF = 8
ATTN_SCALE = HEAD_DIM ** -0.5
PAST_LEN = 16384
RW_HEAD = 64
RW_HEADS = 16
RW_PAIRS = RW_HEADS // 2
W_LORA = 64
A_LORA = 64
G_LORA = 128
LNX_EPS = 64e-5
N_EXPERTS = 32
TOP_K = 4
D_EXPERT = 1024
SWIGLU_LIMIT = 7.0
SWIGLU_ALPHA = 1.702
RMS_EPS = 1e-5
ATT_Q = N_Q_HEADS * HEAD_DIM
ATT_KV = N_KV_HEADS * HEAD_DIM
RW_WIDTH = RW_HEADS * RW_HEAD
RW_COLS = 3 * RW_WIDTH + W_LORA + A_LORA + G_LORA
GATE_COLS = 2 * D_MODEL
IN_COLS = ATT_Q + 2 * ATT_KV + RW_COLS + GATE_COLS
DECAY_SCALE = math.exp(-0.5)
RW_CHUNK = 64
IN_TILE = 512
MOE_BLOCK = 2048
RW_SEQS_PER_STEP = 4


def _nn(a, b):
    return jnp.dot(a, b, preferred_element_type=F32)


def _nt(a, b):
    return lax.dot_general(a, b, (((1,), (1,)), ((), ())), preferred_element_type=F32)


def _tn(a, b):
    return lax.dot_general(a, b, (((0,), (0,)), ((), ())), preferred_element_type=F32)


def _bf(x):
    return x.astype(BF16)


def _sigmoid(x):
    return 0.5 * jnp.tanh(0.5 * x) + 0.5


def _split2(x):
    hi = x.astype(BF16)
    lo = (x - hi.astype(F32)).astype(BF16)
    return hi, lo


def _split3(x):
    hi = x.astype(BF16)
    r1 = x - hi.astype(F32)
    mid = r1.astype(BF16)
    lo = (r1 - mid.astype(F32)).astype(BF16)
    return hi, mid, lo


def _params(n_axes):
    return pltpu.CompilerParams(
        dimension_semantics=("arbitrary",) * n_axes, vmem_limit_bytes=VMEM_LIMIT_BYTES)


_IN_CHUNK = 512


def _inproj_kernel(x_ref, g_ref, w_ref, cos_ref, sa_ref, sb_ref,
                   q_ref, k_ref, v_ref, zr_ref, zg_ref):
    x = x_ref[...]
    ms = jnp.mean(x * x, axis=-1, keepdims=True)
    h = _bf(x * lax.rsqrt(ms + RMS_EPS) * g_ref[...])
    cos, sa, sb = cos_ref[...], sa_ref[...], sb_ref[...]

    def rope(y):
        return (y * cos + pltpu.roll(y, LANES - ROPE_HALF, axis=1) * sa
                + pltpu.roll(y, ROPE_HALF, axis=1) * sb)

    def project(out_ref, col0, width, with_rope):
        for c in range(0, width, _IN_CHUNK):
            cw = min(_IN_CHUNK, width - c)
            acc = _nn(h, w_ref[:, col0 + c:col0 + c + cw])
            if with_rope:
                for j in range(0, cw, LANES):
                    out_ref[:, c + j:c + j + LANES] = rope(acc[:, j:j + LANES])
            else:
                out_ref[:, c:c + cw] = acc

    project(q_ref, 0, ATT_Q, True)
    project(k_ref, ATT_Q, ATT_KV, True)
    project(v_ref, ATT_Q + ATT_KV, ATT_KV, False)
    project(zr_ref, ATT_Q + 2 * ATT_KV, RW_COLS, False)
    project(zg_ref, ATT_Q + 2 * ATT_KV + RW_COLS, GATE_COLS, False)


def _rope_tables(positions, reps=1):
    f32 = np.float32
    inv_freq = (f32(ROPE_THETA) ** (-np.arange(ROPE_HALF, dtype=f32) / f32(ROPE_HALF))).astype(f32)
    ang = positions.astype(f32)[:, None] * inv_freq[None, :]
    cos, sin = np.cos(ang).astype(f32), np.sin(ang).astype(f32)
    t = positions.shape[0]
    one = np.ones((t, HEAD_DIM - ROPE_DIM), f32)
    zero = np.zeros((t, HEAD_DIM - ROPE_DIM), f32)
    z8 = np.zeros((t, ROPE_HALF), f32)
    cos_h = np.concatenate([cos, cos, one], axis=1)
    sa_h = np.concatenate([-sin, z8, zero], axis=1)
    sb_h = np.concatenate([z8, sin, zero], axis=1)
    full = lambda u: jnp.asarray(np.tile(np.concatenate([u, u], axis=1), (reps, 1)))
    return full(cos_h), full(sa_h), full(sb_h)


def _inproj(x2d, g, w_bf, tables, tm):
    n = x2d.shape[0]
    cos, sa, sb = tables
    nper = cos.shape[0] // tm
    row = lambda i: (i, 0)
    tab = lambda i: (i % nper, 0)
    const = lambda i: (0, 0)
    out_shapes = [jax.ShapeDtypeStruct((n, w), F32) for w in (ATT_Q, ATT_KV, ATT_KV, RW_COLS, GATE_COLS)]
    return pl.pallas_call(
        _inproj_kernel,
        grid=(n // tm,),
        in_specs=[
            pl.BlockSpec((tm, D_MODEL), row),
            pl.BlockSpec((1, D_MODEL), const),
            pl.BlockSpec((D_MODEL, IN_COLS), const, pipeline_mode=pl.Buffered(1)),
            pl.BlockSpec((tm, LANES), tab),
            pl.BlockSpec((tm, LANES), tab),
            pl.BlockSpec((tm, LANES), tab),
        ],
        out_specs=[pl.BlockSpec((tm, w), row) for w in (ATT_Q, ATT_KV, ATT_KV, RW_COLS, GATE_COLS)],
        out_shape=out_shapes,
        compiler_params=_params(1),
    )(x2d, g, w_bf, cos, sa, sb)


def _attn_prompt_kernel(sink_ref, q_ref, kp_ref, kc_ref, vp_ref, vc_ref, o_ref, kw_ref, vw_ref, *, nq):
    n = pl.program_id(1)
    w = WINDOW
    lane = lax.broadcasted_iota(jnp.int32, (w, LANES), 1)
    first_half = lane < HEAD_DIM
    qi = lax.broadcasted_iota(jnp.int32, (Q_PER_KV * w, 2 * w), 0) % w
    kj = lax.broadcasted_iota(jnp.int32, (Q_PER_KV * w, 2 * w), 1)
    band = (kj > qi) & (kj <= qi + w)
    band_first = band & (kj >= jnp.where(n > 0, 0, w))

    kw_ref[...] = kc_ref[(nq - 1) * w:nq * w, :]
    vw_ref[...] = vc_ref[(nq - 1) * w:nq * w, :]

    lane2 = lax.broadcasted_iota(jnp.int32, (2 * w, LANES), 1)
    chains = [(u, kv) for u in range(nq) for kv in range(N_KV_HEADS)]

    def keys2(prev_ref, cur_ref, u, gs):
        if u == 0:
            return jnp.concatenate([prev_ref[:, gs], cur_ref[0:w, gs]], axis=0)
        return cur_ref[(u - 1) * w:(u + 1) * w, gs]

    k2, v2, qs, sink = [], [], [], []
    for u, kv in chains:
        gs = slice((kv // 2) * LANES, (kv // 2 + 1) * LANES)
        mine = (lane2 < HEAD_DIM) if kv % 2 == 0 else (lane2 >= HEAD_DIM)
        k2.append(_bf(keys2(kp_ref, kc_ref, u, gs)))
        v2.append(_bf(jnp.where(mine, keys2(vp_ref, vc_ref, u, gs), 1.0)))
        jh = kv % 2
        keep = first_half if jh == 0 else jnp.logical_not(first_half)
        rows, sinks = [], []
        for gq in range(Q_PER_KV):
            hq = kv * Q_PER_KV + gq
            qg = q_ref[u * w:(u + 1) * w, (hq // 2) * LANES:(hq // 2 + 1) * LANES]
            if hq % 2 != jh:
                qg = pltpu.roll(qg, HEAD_DIM, axis=1)
            rows.append(jnp.where(keep, qg * ATTN_SCALE, 0.0))
            sinks.append(jnp.full((w, 1), sink_ref[hq], F32))
        qs.append(_bf(jnp.concatenate(rows, axis=0)))
        sink.append(jnp.concatenate(sinks, axis=0))
    s = [jnp.where(band_first if u == 0 else band, _nt(q_, k_), -jnp.inf)
         for (u, _), q_, k_ in zip(chains, qs, k2)]
    m = [jnp.maximum(jnp.max(s_, axis=-1, keepdims=True), sk) for s_, sk in zip(s, sink)]
    e = [_bf(jnp.exp(s_ - m_)) for s_, m_ in zip(s, m)]
    e_sink = [jnp.exp(sk - m_) for sk, m_ in zip(sink, m)]
    pv = [_nn(e_, v_) for e_, v_ in zip(e, v2)]
    for i, (u, kv) in enumerate(chains):
        jh = kv % 2
        for go in range(Q_PER_KV // 2):
            ra, rb = slice(2 * go * w, (2 * go + 1) * w), slice((2 * go + 1) * w, (2 * go + 2) * w)
            pa, pb = pv[i][ra], pv[i][rb]
            pa_r, pb_r = pltpu.roll(pa, HEAD_DIM, axis=1), pltpu.roll(pb, HEAD_DIM, axis=1)
            if jh == 0:
                oa, ob = pa / (pa_r + e_sink[i][ra]), pb_r / (pb + e_sink[i][rb])
            else:
                oa, ob = pa_r / (pa + e_sink[i][ra]), pb / (pb_r + e_sink[i][rb])
            og = kv * (Q_PER_KV // 2) + go
            o_ref[u * w:(u + 1) * w, og * LANES:(og + 1) * LANES] = jnp.where(first_half, oa, ob).astype(o_ref.dtype)


def _attn_prompt(q, k, v, sinks):
    b, t, _ = q.shape
    nq = 2 if t % (2 * WINDOW) == 0 else 1
    nb = t // (nq * WINDOW)
    cur = lambda bi, n: (bi, n, 0)
    prev = lambda bi, n: (bi, jnp.maximum(nq * n - 1, 0), 0)
    win = lambda bi, n: (bi, 0, 0)
    return pl.pallas_call(
        functools.partial(_attn_prompt_kernel, nq=nq),
        grid=(b, nb),
        in_specs=[
            pl.BlockSpec(memory_space=pltpu.SMEM),
            pl.BlockSpec((None, nq * WINDOW, ATT_Q), cur),
            pl.BlockSpec((None, WINDOW, ATT_KV), prev),
            pl.BlockSpec((None, nq * WINDOW, ATT_KV), cur),
            pl.BlockSpec((None, WINDOW, ATT_KV), prev),
            pl.BlockSpec((None, nq * WINDOW, ATT_KV), cur),
        ],
        out_specs=[
            pl.BlockSpec((None, nq * WINDOW, ATT_Q), cur),
            pl.BlockSpec((None, WINDOW, ATT_KV), win),
            pl.BlockSpec((None, WINDOW, ATT_KV), win),
        ],
        out_shape=[
            jax.ShapeDtypeStruct((b, t, ATT_Q), BF16),
            jax.ShapeDtypeStruct((b, WINDOW, ATT_KV), F32),
            jax.ShapeDtypeStruct((b, WINDOW, ATT_KV), F32),
        ],
        compiler_params=_params(2),
    )(sinks, q, k, k, v, v)


_SAMPLE_BT = 8


def _attn_sample_kernel(sink_ref, q_ref, k_ref, v_ref, ck_ref, cv_ref, o_ref, nk_ref, nv_ref, *, t):
    w = WINDOW
    rows_per_grp = 2 * Q_PER_KV * t
    lane = lax.broadcasted_iota(jnp.int32, (t, LANES), 1)
    first_half = lane < HEAD_DIM
    r_c = lax.broadcasted_iota(jnp.int32, (rows_per_grp, w), 0) % t
    c_c = lax.broadcasted_iota(jnp.int32, (rows_per_grp, w), 1)
    mask_c = c_c > r_c
    r_n = lax.broadcasted_iota(jnp.int32, (rows_per_grp, t), 0) % t
    c_n = lax.broadcasted_iota(jnp.int32, (rows_per_grp, t), 1)
    mask_n = c_n <= r_n

    for bi in range(_SAMPLE_BT):
        nk_ref[bi, 0:w - t, :] = ck_ref[bi, t:w, :]
        nk_ref[bi, w - t:w, :] = k_ref[bi]
        nv_ref[bi, 0:w - t, :] = cv_ref[bi, t:w, :]
        nv_ref[bi, w - t:w, :] = v_ref[bi]

    chains = [(bi, grp) for bi in range(_SAMPLE_BT) for grp in range(ATT_KV // LANES)]
    gsl = lambda grp: slice(grp * LANES, (grp + 1) * LANES)
    kc = [_bf(ck_ref[bi, :, gsl(grp)]) for bi, grp in chains]
    vc = [_bf(cv_ref[bi, :, gsl(grp)]) for bi, grp in chains]
    kn = [_bf(k_ref[bi, :, gsl(grp)]) for bi, grp in chains]
    vn = [_bf(v_ref[bi, :, gsl(grp)]) for bi, grp in chains]
    sink_rows = [[jnp.full((t, 1), sink_ref[(grp * 2 + jh) * Q_PER_KV + gq], F32)
                  for jh in range(2) for gq in range(Q_PER_KV)] for grp in range(ATT_KV // LANES)]
    sink_g = [jnp.concatenate(r_, axis=0) for r_ in sink_rows]
    qs = []
    for bi, grp in chains:
        rows = []
        for jh in range(2):
            keep = first_half if jh == 0 else jnp.logical_not(first_half)
            for gq in range(Q_PER_KV):
                hq = (grp * 2 + jh) * Q_PER_KV + gq
                qg = q_ref[bi, :, (hq // 2) * LANES:(hq // 2 + 1) * LANES]
                if hq % 2 != jh:
                    qg = pltpu.roll(qg, HEAD_DIM, axis=1)
                rows.append(jnp.where(keep, qg * ATTN_SCALE, 0.0))
        qs.append(_bf(jnp.concatenate(rows, axis=0)))
    sink = [sink_g[grp] for _, grp in chains]
    s_c = [jnp.where(mask_c, _nt(q_, k_), -jnp.inf) for q_, k_ in zip(qs, kc)]
    s_n = [jnp.where(mask_n, _nt(q_, k_), -jnp.inf) for q_, k_ in zip(qs, kn)]
    m = [jnp.maximum(jnp.maximum(jnp.max(a, axis=-1, keepdims=True), jnp.max(b_, axis=-1, keepdims=True)), sk)
         for a, b_, sk in zip(s_c, s_n, sink)]
    e_c = [jnp.exp(a - m_) for a, m_ in zip(s_c, m)]
    e_n = [jnp.exp(a - m_) for a, m_ in zip(s_n, m)]
    denom = [jnp.sum(a, axis=-1, keepdims=True) + jnp.sum(b_, axis=-1, keepdims=True) + jnp.exp(sk - m_)
             for a, b_, sk, m_ in zip(e_c, e_n, sink, m)]
    pv = [_nn(_bf(a / d_), va) + _nn(_bf(b_ / d_), vb)
          for a, b_, d_, va, vb in zip(e_c, e_n, denom, vc, vn)]
    for (bi, grp), pv_ in zip(chains, pv):
        for jh in range(2):
            for go in range(Q_PER_KV // 2):
                r0 = (jh * Q_PER_KV + 2 * go) * t
                pa, pb = pv_[r0:r0 + t], pv_[r0 + t:r0 + 2 * t]
                if jh == 0:
                    pb = pltpu.roll(pb, HEAD_DIM, axis=1)
                else:
                    pa = pltpu.roll(pa, HEAD_DIM, axis=1)
                og = (grp * 2 + jh) * (Q_PER_KV // 2) + go
                o_ref[bi, :, og * LANES:(og + 1) * LANES] = jnp.where(first_half, pa, pb)


def _attn_sample(q, k, v, sinks, cache_k, cache_v):
    b, t, _ = q.shape
    bt = _SAMPLE_BT
    blk = lambda i: (i, 0, 0)
    return pl.pallas_call(
        functools.partial(_attn_sample_kernel, t=t),
        grid=(b // bt,),
        in_specs=[
            pl.BlockSpec(memory_space=pltpu.SMEM),
            pl.BlockSpec((bt, t, ATT_Q), blk),
            pl.BlockSpec((bt, t, ATT_KV), blk),
            pl.BlockSpec((bt, t, ATT_KV), blk),
            pl.BlockSpec((bt, WINDOW, ATT_KV), blk),
            pl.BlockSpec((bt, WINDOW, ATT_KV), blk),
        ],
        out_specs=[
            pl.BlockSpec((bt, t, ATT_Q), blk),
            pl.BlockSpec((bt, WINDOW, ATT_KV), blk),
            pl.BlockSpec((bt, WINDOW, ATT_KV), blk),
        ],
        out_shape=[
            jax.ShapeDtypeStruct((b, t, ATT_Q), F32),
            jax.ShapeDtypeStruct((b, WINDOW, ATT_KV), F32),
            jax.ShapeDtypeStruct((b, WINDOW, ATT_KV), F32),
        ],
        compiler_params=_params(1),
    )(sinks, q, k, v, cache_k, cache_v)


def _rwkv_kernel(z_ref, s0_ref, shift_ref, mu_ref, w0_ref, wa_up_ref, a0_ref, g_up_ref,
                 kk_ref, ka_ref, rk_ref, lng_ref, lnb_ref, o_ref, sout_ref, s_ref, prev_ref, *, nb, c, nseg):
    tseg = c // nseg
    rows_all = nb * c
    step = pl.program_id(1)
    pairs = range(RW_PAIRS)
    sls = [slice(p * LANES, (p + 1) * LANES) for p in pairs]
    rsl = [slice(j * c, (j + 1) * c) for j in range(nb)]

    @pl.when(step == 0)
    def _():
        zero = jnp.zeros((RW_HEAD, RW_HEAD), F32)
        for q in range(nb * nseg):
            for p in pairs:
                top = jnp.concatenate([s0_ref[q, 2 * p], zero], axis=1)
                bot = jnp.concatenate([zero, s0_ref[q, 2 * p + 1]], axis=1)
                s_ref[q, p] = jnp.concatenate([top, bot], axis=0)
        if nseg == 1:
            prev_ref[...] = shift_ref[...].reshape(nb, RW_COLS)

    z = z_ref[...].reshape(rows_all, RW_COLS)
    row1 = lax.broadcasted_iota(jnp.int32, (rows_all, 1), 0)
    zprev = pltpu.roll(z, 1, axis=0)
    if nseg == 1:
        for j in range(nb):
            zprev = jnp.where(row1 == j * c, prev_ref[j:j + 1, :], zprev)
            prev_ref[j:j + 1, :] = z[(j + 1) * c - 1:(j + 1) * c, :]
    else:
        zprev = jnp.where(row1 % tseg == 0, shift_ref[...].reshape(rows_all, RW_COLS), zprev)
    zs = z + (zprev - z) * mu_ref[...]

    w3 = 3 * RW_WIDTH
    r, k, v = zs[:, 0:RW_WIDTH], zs[:, RW_WIDTH:2 * RW_WIDTH], zs[:, 2 * RW_WIDTH:w3]
    xwa = zs[:, w3:w3 + LANES]
    xg = zs[:, w3 + LANES:w3 + 2 * LANES]
    lane_all = lax.broadcasted_iota(jnp.int32, (rows_all, LANES), 1)
    lora = _nn(_bf(jnp.where(lane_all < RW_HEAD, jnp.tanh(xwa), xwa)), wa_up_ref[...])
    lane = lax.broadcasted_iota(jnp.int32, (c, LANES), 1)
    head0 = lane < RW_HEAD
    lw = -DECAY_SCALE * _sigmoid(w0_ref[...] + lora[:, 0:RW_WIDTH])
    a_sig = _sigmoid(a0_ref[...] + lora[:, RW_WIDTH:2 * RW_WIDTH])
    g = _nn(_bf(_sigmoid(xg)), g_up_ref[...])
    kk = k * kk_ref[...]
    k = k * (1.0 + (a_sig - 1.0) * ka_ref[...])
    rkr = r * k * rk_ref[...]

    ti = lax.broadcasted_iota(jnp.int32, (rows_all, rows_all), 0)
    tj = lax.broadcasted_iota(jnp.int32, (rows_all, rows_all), 1)
    same_seq = (ti // tseg) == (tj // tseg)
    tri = jnp.where((tj <= ti) & same_seq, 1.0, 0.0).astype(BF16)
    lw3 = _split3(lw)
    cum = _nn(tri, lw3[0]) + _nn(tri, lw3[1]) + _nn(tri, lw3[2])
    if nseg == 1:
        ends = [jnp.broadcast_to(cum[(j + 1) * c - 1:(j + 1) * c, :], (c, RW_WIDTH)) for j in range(nb)]
        c_end = ends[0] if nb == 1 else jnp.concatenate(ends, axis=0)
    else:
        seq1 = jnp.where(same_seq, 1.0, 0.0).astype(BF16)
        c_end = _nn(seq1, lw3[0]) + _nn(seq1, lw3[1]) + _nn(seq1, lw3[2])
    e_c, e_ci, e_cm = jnp.exp(cum), jnp.exp(-cum), jnp.exp(cum - lw)
    e_ce, w_end = jnp.exp(c_end - cum), jnp.exp(c_end)

    gi = lax.broadcasted_iota(jnp.int32, (2 * LANES, LANES), 0) % LANES
    gj = lax.broadcasted_iota(jnp.int32, (2 * LANES, LANES), 1)
    ones2 = jnp.where((gi // RW_HEAD) == (gj // RW_HEAD), 1.0, 0.0).astype(BF16)
    bi_ = lax.broadcasted_iota(jnp.int32, (LANES, LANES), 0)
    bj_ = lax.broadcasted_iota(jnp.int32, (LANES, LANES), 1)
    same_head = (bi_ // RW_HEAD) == (bj_ // RW_HEAD)
    ci = lax.broadcasted_iota(jnp.int32, (c, 2 * c), 0)
    cj = lax.broadcasted_iota(jnp.int32, (c, 2 * c), 1)
    cjm = cj % c
    seq_ok = (ci // tseg) == (cjm // tseg)
    strict = (cjm < ci) & seq_ok
    incl = (cjm <= ci) & seq_ok
    eye_cat = jnp.where(cjm == ci, 1.0, 0.0)
    left = cj < c

    def seg_sum(x):
        hi, lo = _split2(x)
        return _nn(jnp.concatenate([hi, lo], axis=1), ones2)

    def rows2(x):
        return jnp.concatenate([jnp.where(head0, x, 0.0), jnp.where(head0, 0.0, x)], axis=0)

    def bd(cat):
        return _bf(jnp.concatenate([jnp.where(left, cat, 0.0), jnp.where(left, 0.0, cat)], axis=0))

    def pair_mm(cat, x):
        return _nn(_bf(cat), _bf(rows2(x)))

    def seg_sums(xs):
        tot = seg_sum(jnp.concatenate(xs, axis=0))
        return [tot[i * c:(i + 1) * c] for i in range(len(xs))]

    segs = range(nseg)
    rs = [slice(q * tseg, (q + 1) * tseg) for q in segs]

    def run(chains):
        gs = [(rsl[j], sls[p]) for j, p in chains]
        idx = range(len(chains))
        ss = seg_sums([kk[s] * kk[s] for s in gs])
        kkn = [kk[s] * lax.rsqrt(jnp.maximum(q, 1e-24)) for s, q in zip(gs, ss)]
        bv = [n_ * a_sig[s] for s, n_ in zip(gs, kkn)]
        rt = [r[s] * e_c[s] for s in gs]
        kt = [k[s] * e_ci[s] for s in gs]
        at = [-n_ * e_cm[s] for s, n_ in zip(gs, kkn)]
        bt = [b_ * e_ci[s] for s, b_ in zip(gs, bv)]
        bh = [b_ * e_ce[s] for s, b_ in zip(gs, bv)]
        kh = [k[s] * e_ce[s] for s in gs]
        vv = [v[s] for s in gs]

        ar = [_bf(jnp.concatenate([a_, r_], axis=0)) for a_, r_ in zip(at, rt)]
        xbk = [_nt(x, _bf(jnp.concatenate([rows2(b_), rows2(k_)], axis=0)))
               for x, b_, k_ in zip(ar, bt, kt)]
        l_ab = [jnp.where(strict, x[0:c, 0:2 * c], 0.0) for x in xbk]
        l_ak = [jnp.where(strict, x[0:c, 2 * c:4 * c], 0.0) for x in xbk]
        m_rb = [jnp.where(incl, x[c:2 * c, 0:2 * c], 0.0) for x in xbk]
        m_rk = [jnp.where(incl, x[c:2 * c, 2 * c:4 * c], 0.0) for x in xbk]

        t_inv = [l + eye_cat for l in l_ab]
        pw = l_ab
        pw_bd = [bd(x) for x in pw]
        for _ in range(int(math.log2(tseg)) - 1):
            pw = [_nn(_bf(x), xb_) for x, xb_ in zip(pw, pw_bd)]
            pw_bd = [bd(x) for x in pw]
            t_inv = [t + _nn(_bf(t), xb_) for t, xb_ in zip(t_inv, pw_bd)]

        lvy = [pair_mm(jnp.concatenate([l, m], axis=0), v_) for l, m, v_ in zip(l_ak, m_rk, vv)]
        y1 = [x[c:2 * c] for x in lvy]
        au = [_nn(_bf(t), _bf(jnp.concatenate([rows2(a_), rows2(x[0:c])], axis=1)))
              for t, a_, x in zip(t_inv, at, lvy)]
        a_hat = [x[:, 0:LANES] for x in au]
        u0 = [x[:, LANES:2 * LANES] for x in au]

        s_old = [[s_ref[j * nseg + q, p] for q in segs] for j, p in chains]
        pp = [[_nt(_bf(jnp.concatenate([a_hat[i][rs[q]], rt[i][rs[q]]], axis=0)), _bf(s_old[i][q]))
               for q in segs] for i in idx]
        u = [jnp.concatenate([pp[i][q][0:tseg] for q in segs], axis=0) + u0[i] for i in idx]
        y0 = [jnp.concatenate([pp[i][q][tseg:2 * tseg] for q in segs], axis=0) for i in idx]
        y = [y0[i] + pair_mm(m_rb[i], u[i]) + y1[i] for i in idx]
        for i in idx:
            for q in segs:
                upd = _tn(_bf(jnp.concatenate([u[i][rs[q]], vv[i][rs[q]]], axis=0)),
                          _bf(jnp.concatenate([bh[i][rs[q]], kh[i][rs[q]]], axis=0)))
                j, p = chains[i]
                row0 = (j * c + q * tseg) % w_end.shape[0]
                w_q = w_end[row0:row0 + 1, sls[p]]
                s_ref[j * nseg + q, p] = s_old[i][q] * w_q + jnp.where(same_head, upd, 0.0)

        mean = [x * (1.0 / RW_HEAD) for x in seg_sums(y)]
        d = [x - m for x, m in zip(y, mean)]
        var = [x * (1.0 / RW_HEAD) for x in seg_sums([x * x for x in d])]
        bonus = [x * v_ for x, v_ in zip(seg_sums([rkr[s] for s in gs]), vv)]
        for i in idx:
            j, p = chains[i]
            yn = d[i] * lax.rsqrt(var[i] + LNX_EPS) * lng_ref[:, sls[p]] + lnb_ref[:, sls[p]]
            o_ref[j, :, sls[p]] = ((yn + bonus[i]) * g[gs[i]]).astype(o_ref.dtype)

    run([(j, p) for j in range(nb) for p in pairs])

    @pl.when(step == pl.num_programs(1) - 1)
    def _():
        for q in range(nb * nseg):
            for p in pairs:
                tile = s_ref[q, p]
                sout_ref[q, 2 * p] = tile[0:RW_HEAD, 0:RW_HEAD]
                sout_ref[q, 2 * p + 1] = tile[RW_HEAD:2 * RW_HEAD, RW_HEAD:2 * RW_HEAD]


def _rwkv(zr, s0, shift0, lw, nb, c, nseg):
    b, t, _ = zr.shape
    if nseg == 1:
        ngrp, nchunk = b // nb, t // c
        z3 = zr
        shift = shift0.reshape(b, 1, RW_COLS)
        shift_spec = pl.BlockSpec((nb, 1, RW_COLS), lambda bi, i: (bi, 0, 0))
    else:
        assert c == nseg * t
        nsets = b // nseg
        ngrp, nchunk = nsets // nb, 1
        z3 = zr.reshape(nsets, c, RW_COLS)
        shift = jnp.pad(shift0[:, None, :], ((0, 0), (0, t - 1), (0, 0))).reshape(nsets, c, RW_COLS)
        shift_spec = pl.BlockSpec((nb, c, RW_COLS), lambda bi, i: (bi, 0, 0))
    vec = lambda name: lw[name].reshape(1, -1).astype(F32)
    cst = lambda bi, i: (0, 0)
    vspec = lambda wd: pl.BlockSpec((1, wd), cst)
    state_spec = pl.BlockSpec((nb * nseg, RW_HEADS, RW_HEAD, RW_HEAD), lambda bi, i: (bi, 0, 0, 0))
    o, s_new = pl.pallas_call(
        functools.partial(_rwkv_kernel, nb=nb, c=c, nseg=nseg),
        grid=(ngrp, nchunk),
        in_specs=[
            pl.BlockSpec((nb, c, RW_COLS), lambda bi, i: (bi, i, 0)),
            state_spec,
            shift_spec,
            vspec(RW_COLS), vspec(RW_WIDTH),
            pl.BlockSpec((LANES, 2 * RW_WIDTH), cst),
            vspec(RW_WIDTH),
            pl.BlockSpec((G_LORA, RW_WIDTH), cst),
            vspec(RW_WIDTH), vspec(RW_WIDTH), vspec(RW_WIDTH), vspec(RW_WIDTH), vspec(RW_WIDTH),
        ],
        out_specs=[
            pl.BlockSpec((nb, c, RW_WIDTH), lambda bi, i: (bi, i, 0)),
            state_spec,
        ],
        out_shape=[
            jax.ShapeDtypeStruct((ngrp * nb, nchunk * c, RW_WIDTH), BF16),
            jax.ShapeDtypeStruct((b, RW_HEADS, RW_HEAD, RW_HEAD), F32),
        ],
        scratch_shapes=[pltpu.VMEM((nb * nseg, RW_PAIRS, LANES, LANES), F32), pltpu.VMEM((nb, RW_COLS), F32)],
        compiler_params=_params(2),
    )(z3, s0, shift, vec("rw_mu"), vec("rw_w0"), lw["wa_up"], vec("rw_a0"),
      lw["g_up"], vec("rw_k_k"), vec("rw_k_a"), vec("rw_r_k"), vec("rw_lnx_g"), vec("rw_lnx_b"))
    return o.reshape(b, t, RW_WIDTH), s_new


def _mix_kernel(x_ref, oa_ref, or_ref, zg_ref, wba_ref, wbr_ref, wo_ref, g_ref, wrt_ref, brt_ref,
                x1_ref, hn_ref, combt_ref):
    ya = _nn(_bf(oa_ref[...]), wba_ref[...])
    yr = _nn(_bf(or_ref[...]), wbr_ref[...])
    merged = _sigmoid(zg_ref[:, 0:D_MODEL]) * ya + _sigmoid(zg_ref[:, D_MODEL:2 * D_MODEL]) * yr
    x1 = x_ref[...] + _nn(_bf(merged), wo_ref[...])
    x1_ref[...] = x1
    ms = jnp.mean(x1 * x1, axis=-1, keepdims=True)
    hn = x1 * lax.rsqrt(ms + RMS_EPS) * g_ref[...]
    hn_ref[...] = _bf(hn)
    logits = _nt(wrt_ref[...], _bf(hn)) + brt_ref[...]
    row = lax.broadcasted_iota(jnp.int32, logits.shape, 0).astype(F32)
    work = logits
    top = None
    for _ in range(TOP_K):
        m = jnp.max(work, axis=0, keepdims=True)
        if top is None:
            top = m
        idx = jnp.min(jnp.where(work == m, row, float(N_EXPERTS)), axis=0, keepdims=True)
        work = jnp.where(row == idx, -jnp.inf, work)
    e = jnp.where(work != logits, jnp.exp(logits - top), 0.0)
    combt_ref[...] = e / jnp.sum(e, axis=0, keepdims=True)


def _mix(x2d, oa, orw, zg, lw, tm):
    n = x2d.shape[0]
    row = lambda i: (i, 0)
    cst = lambda i: (0, 0)
    wspec = pl.BlockSpec((D_MODEL, D_MODEL), cst)
    return pl.pallas_call(
        _mix_kernel,
        grid=(n // tm,),
        in_specs=[
            pl.BlockSpec((tm, D_MODEL), row), pl.BlockSpec((tm, D_MODEL), row), pl.BlockSpec((tm, D_MODEL), row),
            pl.BlockSpec((tm, GATE_COLS), row),
            wspec, wspec, wspec,
            pl.BlockSpec((1, D_MODEL), cst),
            pl.BlockSpec((N_EXPERTS, D_MODEL), cst),
            pl.BlockSpec((N_EXPERTS, 1), cst),
        ],
        out_specs=[pl.BlockSpec((tm, D_MODEL), row), pl.BlockSpec((tm, D_MODEL), row),
                   pl.BlockSpec((N_EXPERTS, tm), lambda i: (0, i))],
        out_shape=[jax.ShapeDtypeStruct((n, D_MODEL), F32), jax.ShapeDtypeStruct((n, D_MODEL), BF16),
                   jax.ShapeDtypeStruct((N_EXPERTS, n), F32)],
        compiler_params=_params(1),
    )(x2d, oa, orw, zg, lw["w_ba"], lw["w_br"], lw["w_o"], lw["norm_ffn_g"], lw["w_rt"], lw["b_rt"])


_GU_GROUP = 2 * LANES


def _gu_regroup_kernel(w_ref, o_ref):
    src = lax.broadcasted_iota(jnp.int32, (_GU_GROUP, _GU_GROUP), 0)
    dst = lax.broadcasted_iota(jnp.int32, (_GU_GROUP, _GU_GROUP), 1)
    want = jnp.where(dst < LANES, 2 * dst, 2 * (dst - LANES) + 1)
    perm = jnp.where(src == want, 1.0, 0.0).astype(BF16)
    for j in range(0, 2 * D_EXPERT, _GU_GROUP):
        o_ref[:, j:j + _GU_GROUP] = _nn(_bf(w_ref[:, j:j + _GU_GROUP]), perm).astype(o_ref.dtype)


def _gu_regroup(w_gate_up):
    e, d, n = w_gate_up.shape
    tk = 512
    spec = pl.BlockSpec((None, tk, n), lambda i, j: (i, j, 0))
    return pl.pallas_call(
        _gu_regroup_kernel,
        grid=(e, d // tk),
        in_specs=[spec],
        out_specs=spec,
        out_shape=jax.ShapeDtypeStruct((e, d, n), BF16),
        compiler_params=_params(2),
    )(w_gate_up)


_MOE_DOMAIN = 1024
_MOE_ROWS = 160
_RANK_CHUNK = 256


def _moe_kernel(nsub_ref, hn_ref, comb_t_ref, x1_ref, wgu_ref, bgu_ref, wd_ref, bd_ref, gf_ref, out_ref,
                rank_t_ref, act_ref):
    blk, e = pl.program_id(0), pl.program_id(1)
    tb = hn_ref.shape[0]
    dom, rows = min(_MOE_DOMAIN, tb), _MOE_ROWS
    ndom = tb // dom

    @pl.when(e == 0)
    def _():
        out_ref[...] = jnp.zeros_like(out_ref)
        ci = lax.broadcasted_iota(jnp.int32, (_RANK_CHUNK, _RANK_CHUNK), 0)
        cj = lax.broadcasted_iota(jnp.int32, (_RANK_CHUNK, _RANK_CHUNK), 1)
        earlier = jnp.where(ci < cj, 1.0, 0.0).astype(BF16)
        for d0 in range(0, tb, dom):
            seen = jnp.zeros((N_EXPERTS, 1), F32)
            for c0 in range(d0, d0 + dom, _RANK_CHUNK):
                routed = comb_t_ref[:, c0:c0 + _RANK_CHUNK] > 0.0
                hot = jnp.where(routed, 1.0, 0.0)
                rank_t_ref[:, c0:c0 + _RANK_CHUNK] = jnp.where(routed, _nn(_bf(hot), earlier) + seen, -1.0)
                seen = seen + jnp.sum(hot, axis=1, keepdims=True)

    slot_r = lax.broadcasted_iota(jnp.int32, (rows, dom), 0).astype(F32)

    doms = [slice(d * dom, (d + 1) * dom) for d in range(ndom)]
    rank_rows = [rank_t_ref[pl.ds(e, 1), ds_] for ds_ in doms]
    w_rows_all = [comb_t_ref[pl.ds(e, 1), ds_] for ds_ in doms]
    trips = [nsub_ref[(blk * ndom + d) * N_EXPERTS + e] for d in range(ndom)]
    trip = trips[0]
    for t_ in trips[1:]:
        trip = jnp.maximum(trip, t_)

    def sub_tiles(s, carry):
        base = (s * rows).astype(F32)
        for d, ds_ in enumerate(doms):
            hit = rank_rows[d] - base == slot_r
            gather = jnp.where(hit, 1.0, 0.0).astype(BF16)
            x = _bf(_nn(gather, hn_ref[ds_, :]))
            w_rows = jnp.sum(jnp.where(hit, w_rows_all[d], 0.0), axis=1, keepdims=True)
            for j in range(D_EXPERT // LANES):
                gs = slice(j * _GU_GROUP, (j + 1) * _GU_GROUP)
                gu = _nn(x, wgu_ref[:, gs]) + bgu_ref[:, gs]
                glu = jnp.minimum(gu[:, 0:LANES], SWIGLU_LIMIT)
                lin = jnp.clip(gu[:, LANES:_GU_GROUP], -SWIGLU_LIMIT, SWIGLU_LIMIT)
                act_ref[d, :, j * LANES:(j + 1) * LANES] = _bf(glu * _sigmoid(SWIGLU_ALPHA * glu) * (lin + 1.0))
            y = (_nn(act_ref[d], _bf(wd_ref[...])) + bd_ref[...]) * w_rows
            out_ref[ds_, :] += _tn(gather, _bf(y))
        return carry

    lax.fori_loop(0, trip, sub_tiles, 0)

    part = x1_ref.shape[0]
    res_rows = pl.ds(pl.multiple_of(e * part, part), part)
    out_ref[res_rows, :] += x1_ref[...]

    @pl.when(e == N_EXPERTS - 1)
    def _():
        for r0 in range(0, tb, _RANK_CHUNK):
            xo = out_ref[r0:r0 + _RANK_CHUNK, :]
            ms = jnp.mean(xo * xo, axis=-1, keepdims=True)
            out_ref[r0:r0 + _RANK_CHUNK, :] = xo * lax.rsqrt(ms + RMS_EPS) * gf_ref[...]


def _moe(hn, comb_t, x1, lw, gf, tb):
    n = hn.shape[0]
    nblk = n // tb
    dom = min(_MOE_DOMAIN, tb)
    counts = jnp.sum((comb_t > 0.0).reshape(N_EXPERTS, n // dom, dom), axis=-1, dtype=jnp.int32)
    nsub = ((counts.T + _MOE_ROWS - 1) // _MOE_ROWS).reshape(-1)
    row = lambda i, e, ns: (i, 0)
    ex = lambda i, e, ns: (e, 0, 0)
    return pl.pallas_call(
        _moe_kernel,
        grid_spec=pltpu.PrefetchScalarGridSpec(
            num_scalar_prefetch=1,
            grid=(nblk, N_EXPERTS),
            in_specs=[
                pl.BlockSpec((tb, D_MODEL), row), pl.BlockSpec((N_EXPERTS, tb), lambda i, e, ns: (0, i)),
                pl.BlockSpec((tb // N_EXPERTS, D_MODEL), lambda i, e, ns: (i * N_EXPERTS + e, 0)),
                pl.BlockSpec((None, D_MODEL, 2 * D_EXPERT), ex), pl.BlockSpec((None, 1, 2 * D_EXPERT), ex),
                pl.BlockSpec((None, D_EXPERT, D_MODEL), ex), pl.BlockSpec((None, 1, D_MODEL), ex),
                pl.BlockSpec((1, D_MODEL), lambda i, e, ns: (0, 0)),
            ],
            out_specs=pl.BlockSpec((tb, D_MODEL), row),
            scratch_shapes=[pltpu.VMEM((N_EXPERTS, tb), F32),
                            pltpu.VMEM((tb // dom, _MOE_ROWS, D_EXPERT), BF16)],
        ),
        out_shape=jax.ShapeDtypeStruct((n, D_MODEL), F32),
        compiler_params=_params(2),
    )(nsub, hn, comb_t, x1, lw["w_gu"], lw["b_gu"], lw["w_down"], lw["b_down"], gf)


def _prep_layer(l, norm_mix_g, w_in, attn_sinks, rw_mu, rw_w0, rw_w_up, rw_a0, rw_a_up, rw_g_up,
                rw_k_k, rw_k_a, rw_r_k, rw_lnx_g, rw_lnx_b, w_branch_attn, w_branch_rwkv, w_out,
                norm_ffn_g, w_router, b_router, w_gate_up, b_gate_up, w_down, b_down):
    zeros = jnp.zeros((W_LORA, RW_WIDTH), F32)
    wa_up = jnp.concatenate([jnp.concatenate([rw_w_up[l], zeros], axis=1),
                             jnp.concatenate([zeros, rw_a_up[l]], axis=1)], axis=0)
    return {
        "norm_mix_g": norm_mix_g[l].reshape(1, D_MODEL),
        "w_in": _bf(w_in[l]),
        "attn_sinks": attn_sinks[l].astype(F32),
        "rw_mu": rw_mu[l], "rw_w0": rw_w0[l], "rw_a0": rw_a0[l], "rw_k_k": rw_k_k[l], "rw_k_a": rw_k_a[l],
        "rw_r_k": rw_r_k[l], "rw_lnx_g": rw_lnx_g[l], "rw_lnx_b": rw_lnx_b[l],
        "wa_up": _bf(wa_up), "g_up": _bf(rw_g_up[l]),
        "w_ba": _bf(w_branch_attn[l]), "w_br": _bf(w_branch_rwkv[l]), "w_o": _bf(w_out[l]),
        "norm_ffn_g": norm_ffn_g[l].reshape(1, D_MODEL),
        "w_rt": _bf(w_router[l].T), "b_rt": b_router[l].reshape(N_EXPERTS, 1).astype(F32),
        "w_gu": _gu_regroup(w_gate_up[l]),
        "b_gu": b_gate_up[l].reshape(N_EXPERTS, D_EXPERT // LANES, LANES, 2).transpose(0, 1, 3, 2)
                .reshape(N_EXPERTS, 1, 2 * D_EXPERT),
        "w_down": w_down[l], "b_down": b_down[l][:, None, :],
    }


def _tile(n, pref):
    tm = pref
    while n % tm:
        tm //= 2
    return tm


def _layer(x, tables, cache_k, cache_v, s0, shift0, lw, gf, rw_sets, rw_segs):
    b, t, _ = x.shape
    n = b * t
    x2d = x.reshape(n, D_MODEL)
    tm = _tile(n, IN_TILE)
    q, k, v, zr, zg = _inproj(x2d, lw["norm_mix_g"], lw["w_in"], tables, tm)
    q3, k3, v3 = q.reshape(b, t, ATT_Q), k.reshape(b, t, ATT_KV), v.reshape(b, t, ATT_KV)
    if cache_k is None:
        o_att, k_win, v_win = _attn_prompt(q3, k3, v3, lw["attn_sinks"])
    else:
        o_att, k_win, v_win = _attn_sample(q3, k3, v3, lw["attn_sinks"],
                                           cache_k.reshape(b, WINDOW, ATT_KV), cache_v.reshape(b, WINDOW, ATT_KV))
    zr3 = zr.reshape(b, t, RW_COLS)
    o_rw, s_new = _rwkv(zr3, s0, shift0, lw, rw_sets, RW_CHUNK, rw_segs)
    tm2 = _tile(n, 512)
    x1, hn, comb_t = _mix(x2d, o_att.reshape(n, ATT_Q), o_rw.reshape(n, RW_WIDTH), zg, lw, tm2)
    y = _moe(hn, comb_t, x1, lw, gf, _tile(n, MOE_BLOCK))
    kv_shape = (b, WINDOW, N_KV_HEADS, HEAD_DIM)
    return (y.reshape(b, t, D_MODEL), k_win.reshape(kv_shape), v_win.reshape(kv_shape),
            s_new, zr3[:, t - 1, :])


def kernel(x_prompt, x_sample, cache_k, cache_v, state_wkv, state_shift, norm_mix_g, w_in, attn_sinks, rw_mu, rw_w0, rw_w_up, rw_a0, rw_a_up, rw_g_up, rw_k_k, rw_k_a, rw_r_k, rw_lnx_g, rw_lnx_b, w_branch_attn, w_branch_rwkv, w_out, norm_ffn_g, w_router, b_router, w_gate_up, b_gate_up, w_down, b_down, norm_final_g):
    assert w_in.shape[0] == 1, "single-layer trunk"
    bp, tp, _ = x_prompt.shape
    bs, ts, _ = x_sample.shape
    lw = _prep_layer(0, norm_mix_g, w_in, attn_sinks, rw_mu, rw_w0, rw_w_up, rw_a0, rw_a_up, rw_g_up,
                     rw_k_k, rw_k_a, rw_r_k, rw_lnx_g, rw_lnx_b, w_branch_attn, w_branch_rwkv, w_out,
                     norm_ffn_g, w_router, b_router, w_gate_up, b_gate_up, w_down, b_down)
    gf = norm_final_g.reshape(1, D_MODEL)

    tab_p = _rope_tables(np.arange(tp))
    tab_s = _rope_tables(PAST_LEN + np.arange(ts), reps=_tile(bs * ts, IN_TILE) // ts)

    s0p = jnp.zeros((bp, RW_HEADS, RW_HEAD, RW_HEAD), state_wkv.dtype)
    sh0p = jnp.zeros((bp, RW_COLS), state_shift.dtype)
    sets_p = RW_SEQS_PER_STEP if bp % RW_SEQS_PER_STEP == 0 else 1
    segs_s = RW_CHUNK // ts
    sets_s = 2 if bs % (2 * segs_s) == 0 else 1
    yp, kp, vp, sp, shp = _layer(x_prompt, tab_p, None, None, s0p, sh0p, lw, gf, sets_p, 1)
    ys, ks, vs, ss, shs = _layer(x_sample, tab_s, cache_k[0], cache_v[0], state_wkv[0], state_shift[0],
                                 lw, gf, sets_s, segs_s)
    ex = lambda u: u[None]
    return (yp, ys, ex(kp), ex(vp), ex(sp), ex(shp), ex(ks), ex(vs), ex(ss), ex(shs))
```
